```python
import math
import jax
import jax.numpy as jnp
from jax import lax
import numpy as np

D_MODEL = 4096
BATCH = 4
SEQ = 2048
DEPTH = 2
DEC_BATCH = 16
DEC_SEQ = 32
PAST_LEN = 2048

CHUNK = 64
D_HEAD = 128
H_SB = 16
H_CB = 16
H_DIFF = D_MODEL // (2 * D_HEAD)
D_SB = H_SB * D_HEAD
D_CB = H_CB * D_HEAD
D_DIFF = H_DIFF * 2 * D_HEAD
BAND_CHUNKS = 8
BAND = BAND_CHUNKS * CHUNK
REL_CLIP = 128
ROT_DIM = D_HEAD // 4
ROPE_THETA = 500000.0
D_FF = 11008
CONV_W = 3
Q_BLOCK = 128
EPS = 1e-6
NEG = -1e30
DIFF_LAYER = 1
LAMBDA_INIT = 0.8 - 0.6 * math.exp(-0.3 * DIFF_LAYER)

kernel_name = 'hybrid_stickbreak_chunkband_diffattn_convglu_step'


def rms_norm(x, g):
    xf = x.astype(jnp.float32)
    y = xf * lax.rsqrt(jnp.mean(xf * xf, axis=-1, keepdims=True) + EPS)
    return (y * g.astype(jnp.float32)).astype(x.dtype)


def rope_partial(x, pos):
    half = ROT_DIM // 2
    inv_freq = ROPE_THETA ** (-jnp.arange(half, dtype=jnp.float32) * (2.0 / ROT_DIM))
    ang = pos.astype(jnp.float32)[:, None] * inv_freq[None, :]
    cos = jnp.cos(ang)[None, :, None, None, :]
    sin = jnp.sin(ang)[None, :, None, None, :]
    xf = x.astype(jnp.float32)
    x1 = xf[..., :half]
    x2 = xf[..., half:ROT_DIM]
    out = jnp.concatenate([x1 * cos - x2 * sin, x2 * cos + x1 * sin, xf[..., ROT_DIM:]], axis=-1)
    return out.astype(x.dtype)


def over_query_blocks(fn, q, q_pos):
    b, t = q.shape[:2]
    nb = t // Q_BLOCK
    qb = jnp.moveaxis(q.reshape((b, nb, Q_BLOCK) + q.shape[2:]), 1, 0)
    pb = q_pos.reshape(nb, Q_BLOCK)
    out = lax.map(lambda a: fn(a[0], a[1]), (qb, pb))
    return jnp.moveaxis(out, 0, 1).reshape((b, t) + out.shape[3:])


def stick_breaking(q, k, v, q_pos, k_pos):
    z = jnp.einsum('bqhd,bkhd->bhqk', q, k).astype(jnp.float32) / math.sqrt(D_HEAD)
    vis = k_pos[None, :] < q_pos[:, None]
    log_stay = jnp.where(vis, jax.nn.log_sigmoid(-z), 0.0)
    after = lax.cumsum(log_stay, axis=3, reverse=True) - log_stay
    w = jnp.where(vis, jnp.exp(jax.nn.log_sigmoid(z) + after), 0.0)
    return jnp.einsum('bhqk,bkhd->bqhd', w, v.astype(jnp.float32)).astype(v.dtype)


def rel_bias_lookup(table, rel):
    return table.astype(jnp.float32)[:, jnp.clip(rel, -REL_CLIP, REL_CLIP) + REL_CLIP]


def band_prompt(q, k, v, table):
    b, s, h, d = q.shape
    nc = s // CHUNK
    nbnd = BAND_CHUNKS + 1
    qc = q.reshape(b, nc, CHUNK, h, d)
    pad = ((0, 0), (BAND_CHUNKS, 0), (0, 0), (0, 0), (0, 0))
    kp = jnp.pad(k.reshape(b, nc, CHUNK, h, d), pad)
    vp = jnp.pad(v.reshape(b, nc, CHUNK, h, d), pad)
    idx = jnp.arange(nc)[:, None] + jnp.arange(nbnd)[None, :]
    kband = kp[:, idx].reshape(b, nc, nbnd * CHUNK, h, d)
    vband = vp[:, idx].reshape(b, nc, nbnd * CHUNK, h, d)
    valid = jnp.repeat(idx >= BAND_CHUNKS, CHUNK, axis=1)
    k_off = jnp.arange(nbnd * CHUNK) - BAND_CHUNKS * CHUNK
    rel = k_off[None, :] - jnp.arange(CHUNK)[:, None]
    bias = rel_bias_lookup(table, rel)
    sc = jnp.einsum('bcqhd,bckhd->bchqk', qc, kband).astype(jnp.float32) / math.sqrt(d)
    sc = jnp.where(valid[None, :, None, None, :], sc + bias[None, None], NEG)
    p = jax.nn.softmax(sc, axis=-1)
    o = jnp.einsum('bchqk,bckhd->bcqhd', p, vband.astype(jnp.float32))
    return o.reshape(b, s, h, d).astype(v.dtype)


def band_sample(q, k_new, v_new, k_cache, v_cache, table, past):
    w = k_cache.shape[1]
    t = q.shape[1]
    k = jnp.concatenate([k_cache, k_new], axis=1)
    v = jnp.concatenate([v_cache, v_new], axis=1)
    q_pos = past + jnp.arange(t)
    k_pos = past - w + jnp.arange(w + t)
    bias = rel_bias_lookup(table, k_pos[None, :] - q_pos[:, None])
    sc = jnp.einsum('bqhd,bkhd->bhqk', q, k).astype(jnp.float32) / math.sqrt(D_HEAD) + bias[None]
    p = jax.nn.softmax(sc, axis=-1)
    return jnp.einsum('bhqk,bkhd->bqhd', p, v.astype(jnp.float32)).astype(v.dtype)


def chunk_causal(q_pos, k_pos):
    return (k_pos[None, :] // CHUNK) <= (q_pos[:, None] // CHUNK)


def diff_lambda(lq1, lk1, lq2, lk2):
    f = lambda a: a.astype(jnp.float32)
    return jnp.exp(jnp.sum(f(lq1) * f(lk1))) - jnp.exp(jnp.sum(f(lq2) * f(lk2))) + LAMBDA_INIT


def diff_attention(q, k, v, mask, lam, g_sub):
    sc = jnp.einsum('bqhcd,bkhcd->bhcqk', q, k).astype(jnp.float32) / math.sqrt(D_HEAD)
    p = jax.nn.softmax(jnp.where(mask, sc, NEG), axis=-1)
    a = p[:, :, 0] - lam * p[:, :, 1]
    o = jnp.einsum('bhqk,bkhe->bqhe', a, v.astype(jnp.float32))
    return (rms_norm(o, g_sub) * (1.0 - LAMBDA_INIT)).astype(v.dtype)


def project_ab(xn, w_in, g_q, g_k):
    b, t, _ = xn.shape
    cuts = [D_SB, 2 * D_SB, 3 * D_SB, 3 * D_SB + D_CB, 3 * D_SB + 2 * D_CB]
    q_a, k_a, v_a, q_b, k_b, v_b = jnp.split(xn @ w_in, cuts, axis=-1)
    sh_a = (b, t, H_SB, D_HEAD)
    sh_b = (b, t, H_CB, D_HEAD)
    return (q_a.reshape(sh_a), k_a.reshape(sh_a), v_a.reshape(sh_a),
            rms_norm(q_b.reshape(sh_b), g_q), rms_norm(k_b.reshape(sh_b), g_k), v_b.reshape(sh_b))


def project_diff(xn, w_in, g_q, g_k, pos):
    b, t, _ = xn.shape
    q, k, v = jnp.split(xn @ w_in, 3, axis=-1)
    sh = (b, t, H_DIFF, 2, D_HEAD)
    q = rope_partial(rms_norm(q.reshape(sh), g_q), pos)
    k = rope_partial(rms_norm(k.reshape(sh), g_k), pos)
    return q, k, v.reshape(b, t, H_DIFF, 2 * D_HEAD)


def conv_glu(xn, w_up, w_conv, w_down, hist):
    t = xn.shape[1]
    g, u = jnp.split(xn @ w_up, 2, axis=-1)
    gp = jnp.concatenate([hist, g], axis=1)
    gc = gp[:, 0:t] * w_conv[0]
    for i in range(1, CONV_W):
        gc = gc + gp[:, i:i + t] * w_conv[i]
    y = (jax.nn.silu(gc) * u) @ w_down
    return y, gp[:, -(CONV_W - 1):]


def setup_inputs(seed: int = 0) -> dict:
    key = jax.random.key(seed)
    ks = jax.random.split(key, 28)
    f32 = jnp.float32

    def nrm(k, shape, scale=1.0):
        return jax.random.normal(k, shape, f32) * scale

    band_rows = min(BAND, PAST_LEN)
    return {
        'x_prompt': nrm(ks[0], (BATCH, SEQ, D_MODEL)),
        'x_sample': nrm(ks[1], (DEC_BATCH, DEC_SEQ, D_MODEL)),
        'cache_k_sb': nrm(ks[2], (DEC_BATCH, PAST_LEN, H_SB, D_HEAD)),
        'cache_v_sb': nrm(ks[3], (DEC_BATCH, PAST_LEN, H_SB, D_HEAD)),
        'cache_k_band': nrm(ks[4], (DEC_BATCH, band_rows, H_CB, D_HEAD)),
        'cache_v_band': nrm(ks[5], (DEC_BATCH, band_rows, H_CB, D_HEAD)),
        'cache_k_diff': nrm(ks[6], (DEC_BATCH, PAST_LEN, H_DIFF, 2 * D_HEAD)),
        'cache_v_diff': nrm(ks[7], (DEC_BATCH, PAST_LEN, H_DIFF, 2 * D_HEAD)),
        'state_conv_ffn': nrm(ks[8], (DEPTH, DEC_BATCH, CONV_W - 1, D_FF)),
        'norm_mix': 1.0 + nrm(ks[9], (DEPTH, D_MODEL), 0.05),
        'norm_ffn': 1.0 + nrm(ks[10], (DEPTH, D_MODEL), 0.05),
        'w_in_ab': nrm(ks[11], (D_MODEL, 3 * (D_SB + D_CB)), D_MODEL ** -0.5),
        'w_out_ab': nrm(ks[12], (D_SB + D_CB, D_MODEL), (D_SB + D_CB) ** -0.5),
        'g_q_band': 1.0 + nrm(ks[13], (D_HEAD,), 0.05),
        'g_k_band': 1.0 + nrm(ks[14], (D_HEAD,), 0.05),
        'rel_bias_band': nrm(ks[15], (H_CB, 2 * REL_CLIP + 1), 0.5),
        'w_in_diff': nrm(ks[16], (D_MODEL, 3 * D_DIFF), D_MODEL ** -0.5),
        'w_out_diff': nrm(ks[17], (D_DIFF, D_MODEL), D_DIFF ** -0.5),
        'g_q_diff': 1.0 + nrm(ks[18], (D_HEAD,), 0.05),
        'g_k_diff': 1.0 + nrm(ks[19], (D_HEAD,), 0.05),
        'lambda_q1': nrm(ks[20], (D_HEAD,), 0.1),
        'lambda_k1': nrm(ks[21], (D_HEAD,), 0.1),
        'lambda_q2': nrm(ks[22], (D_HEAD,), 0.1),
        'lambda_k2': nrm(ks[23], (D_HEAD,), 0.1),
        'g_sub_diff': 1.0 + nrm(ks[24], (2 * D_HEAD,), 0.05),
        'w_up': nrm(ks[25], (DEPTH, D_MODEL, 2 * D_FF), D_MODEL ** -0.5),
        'w_conv': nrm(ks[26], (DEPTH, CONV_W, D_FF), CONV_W ** -0.5),
        'w_down': nrm(ks[27], (DEPTH, D_FF, D_MODEL), D_FF ** -0.5),
    }


def reference(x_prompt, x_sample, cache_k_sb, cache_v_sb, cache_k_band, cache_v_band,
              cache_k_diff, cache_v_diff, state_conv_ffn, norm_mix, norm_ffn,
              w_in_ab, w_out_ab, g_q_band, g_k_band, rel_bias_band,
              w_in_diff, w_out_diff, g_q_diff, g_k_diff,
              lambda_q1, lambda_k1, lambda_q2, lambda_k2, g_sub_diff,
              w_up, w_conv, w_down):
    bp, tp, _ = x_prompt.shape
    bs, ts, _ = x_sample.shape
    past = cache_k_sb.shape[1]
    pos_p = jnp.arange(tp)
    pos_s = past + jnp.arange(ts)
    k_pos_s = jnp.arange(past + ts)
    hp, hs = x_prompt, x_sample
    conv_p, conv_s = [], []
    for layer in range(DEPTH):
        xp = rms_norm(hp, norm_mix[layer])
        xs = rms_norm(hs, norm_mix[layer])
        if layer % 2 == 0:
            qa, ka, va, qb, kb, vb = project_ab(xp, w_in_ab, g_q_band, g_k_band)
            oa = over_query_blocks(lambda qq, pp: stick_breaking(qq, ka, va, pp, pos_p), qa, pos_p)
            ob = band_prompt(qb, kb, vb, rel_bias_band)
            hp = hp + jnp.concatenate([oa.reshape(bp, tp, D_SB), ob.reshape(bp, tp, D_CB)], axis=-1) @ w_out_ab
            keep = min(BAND, tp)
            k_sb_p, v_sb_p = ka, va
            k_band_p, v_band_p = kb[:, tp - keep:], vb[:, tp - keep:]
            qa, ka, va, qb, kb, vb = project_ab(xs, w_in_ab, g_q_band, g_k_band)
            k_all = jnp.concatenate([cache_k_sb, ka], axis=1)
            v_all = jnp.concatenate([cache_v_sb, va], axis=1)
            oa = stick_breaking(qa, k_all, v_all, pos_s, k_pos_s)
            ob = band_sample(qb, kb, vb, cache_k_band, cache_v_band, rel_bias_band, past)
            hs = hs + jnp.concatenate([oa.reshape(bs, ts, D_SB), ob.reshape(bs, ts, D_CB)], axis=-1) @ w_out_ab
            k_sb_s, v_sb_s, k_band_s, v_band_s = ka, va, kb, vb
        else:
            lam = diff_lambda(lambda_q1, lambda_k1, lambda_q2, lambda_k2)
            q, k, v = project_diff(xp, w_in_diff, g_q_diff, g_k_diff, pos_p)
            o = over_query_blocks(
                lambda qq, pp: diff_attention(qq, k, v, chunk_causal(pp, pos_p), lam, g_sub_diff), q, pos_p)
            hp = hp + o.reshape(bp, tp, D_DIFF) @ w_out_diff
            k_diff_p, v_diff_p = k.reshape(bp, tp, H_DIFF, 2 * D_HEAD), v
            q, k, v = project_diff(xs, w_in_diff, g_q_diff, g_k_diff, pos_s)
            k_all = jnp.concatenate([cache_k_diff.reshape(bs, past, H_DIFF, 2, D_HEAD), k], axis=1)
            v_all = jnp.concatenate([cache_v_diff, v], axis=1)
            o = diff_attention(q, k_all, v_all, chunk_causal(pos_s, k_pos_s), lam, g_sub_diff)
            hs = hs + o.reshape(bs, ts, D_DIFF) @ w_out_diff
            k_diff_s, v_diff_s = k.reshape(bs, ts, H_DIFF, 2 * D_HEAD), v
        fp, cp = conv_glu(rms_norm(hp, norm_ffn[layer]), w_up[layer], w_conv[layer], w_down[layer],
                          jnp.zeros((bp, CONV_W - 1, D_FF), hp.dtype))
        fs, cs = conv_glu(rms_norm(hs, norm_ffn[layer]), w_up[layer], w_conv[layer], w_down[layer],
                          state_conv_ffn[layer])
        hp = hp + fp
        hs = hs + fs
        conv_p.append(cp)
        conv_s.append(cs)
    new_conv_p = jnp.stack(conv_p)
    new_conv_s = jnp.stack(conv_s)
    return (hp, hs, k_sb_p, v_sb_p, k_band_p, v_band_p, k_diff_p, v_diff_p, new_conv_p,
            k_sb_s, v_sb_s, k_band_s, v_band_s, k_diff_s, v_diff_s, new_conv_s)
```

```python
import functools
import math

import jax
import jax.numpy as jnp
from jax import lax
from jax.experimental import pallas as pl
from jax.experimental.pallas import tpu as pltpu

F32 = jnp.float32
BF16 = jnp.bfloat16

CHUNK = 64
D_HEAD = 128
BAND_CHUNKS = 8
REL_CLIP = 128
ROT_DIM = D_HEAD // 4
ROPE_THETA = 500000.0
CONV_W = 3
EPS = 1e-6
NEG = -1e30
LAMBDA_INIT = 0.8 - 0.6 * math.exp(-0.3 * 1)
SCALE = 1.0 / math.sqrt(D_HEAD)

V7X_VMEM_LIMIT_BYTES = 56 * 1024 * 1024
LANES = 128

ROW_TILE = 512
DOWN_ROW_TILE = 256
SB_TQ = 256
SB_TK = 128
BAND_QCHUNKS = 4
DIFF_T = 256


def _params(*sem):
    return pltpu.CompilerParams(dimension_semantics=sem, vmem_limit_bytes=V7X_VMEM_LIMIT_BYTES)


def _iota(shape, dim):
    return lax.broadcasted_iota(jnp.int32, shape, dim)


def _rms(x, g):
    return x * lax.rsqrt(jnp.mean(x * x, axis=-1, keepdims=True) + EPS) * g


def _rmsnorm_kernel(xp_ref, xs_ref, g_ref, o_ref, *, np_tiles):
    i = pl.program_id(0)

    @pl.when(i < np_tiles)
    def _():
        o_ref[...] = _rms(xp_ref[...], g_ref[...]).astype(o_ref.dtype)

    @pl.when(i >= np_tiles)
    def _():
        o_ref[...] = _rms(xs_ref[...], g_ref[...]).astype(o_ref.dtype)


def rmsnorm_rows(xp, xs, g, tm=256):
    mp, d = xp.shape
    ms = xs.shape[0]
    np_tiles, ns_tiles = mp // tm, ms // tm
    return pl.pallas_call(
        functools.partial(_rmsnorm_kernel, np_tiles=np_tiles),
        grid=(np_tiles + ns_tiles,),
        in_specs=[
            pl.BlockSpec((tm, d), lambda i: (jnp.minimum(i, np_tiles - 1), 0)),
            pl.BlockSpec((tm, d), lambda i: (jnp.maximum(i - np_tiles, 0), 0)),
            pl.BlockSpec((1, d), lambda i: (0, 0)),
        ],
        out_specs=pl.BlockSpec((tm, d), lambda i: (i, 0)),
        out_shape=jax.ShapeDtypeStruct((mp + ms, d), BF16),
        compiler_params=_params("arbitrary"),
        name="rmsnorm_rows",
    )(xp, xs, g.reshape(1, d))


def _proj_kernel(x_ref, w_ref, op_ref, os_ref, wb_ref, *, np_tiles):
    i = pl.program_id(1)

    @pl.when(i == 0)
    def _():
        wb_ref[...] = w_ref[...].astype(BF16)

    @pl.when(i < np_tiles)
    def _():
        op_ref[...] = jnp.dot(x_ref[...], wb_ref[...], preferred_element_type=F32)

    @pl.when(i >= np_tiles)
    def _():
        os_ref[...] = jnp.dot(x_ref[...], wb_ref[...], preferred_element_type=F32)


def project(x, w, col0, n, mp, tm=ROW_TILE, tn=512):
    m, k = x.shape
    ms = m - mp
    np_tiles, ns_tiles = mp // tm, ms // tm
    jb = col0 // tn
    return pl.pallas_call(
        functools.partial(_proj_kernel, np_tiles=np_tiles),
        grid=(n // tn, np_tiles + ns_tiles),
        in_specs=[
            pl.BlockSpec((tm, k), lambda j, i: (i, 0)),
            pl.BlockSpec((k, tn), lambda j, i: (0, jb + j)),
        ],
        out_specs=[
            pl.BlockSpec((tm, tn), lambda j, i: (jnp.minimum(i, np_tiles - 1), j)),
            pl.BlockSpec((tm, tn), lambda j, i: (jnp.maximum(i - np_tiles, 0), j)),
        ],
        out_shape=[jax.ShapeDtypeStruct((mp, n), F32), jax.ShapeDtypeStruct((ms, n), F32)],
        scratch_shapes=[pltpu.VMEM((k, tn), BF16)],
        compiler_params=_params("arbitrary", "arbitrary"),
        name="project",
    )(x, w)


def _outproj_kernel(*refs, np_tiles, nparts):
    ap = refs[0:nparts]
    a_s = refs[nparts:2 * nparts]
    w = refs[2 * nparts:3 * nparts]
    rp_ref, rs_ref, op_ref, os_ref, wb_ref = refs[3 * nparts:]
    i = pl.program_id(1)

    @pl.when(i == 0)
    def _():
        for p in range(nparts):
            wb_ref[p] = w[p][...].astype(BF16)

    def run(a_refs, r_ref, o_ref):
        acc = r_ref[...]
        for p in range(nparts):
            acc = acc + jnp.dot(a_refs[p][...], wb_ref[p], preferred_element_type=F32)
        o_ref[...] = acc

    @pl.when(i < np_tiles)
    def _():
        run(ap, rp_ref, op_ref)

    @pl.when(i >= np_tiles)
    def _():
        run(a_s, rs_ref, os_ref)


def out_project(parts_p, parts_s, w, res_p, res_s, tm=ROW_TILE, tn=512):
    nparts = len(parts_p)
    mp, kp = parts_p[0].shape
    ms = parts_s[0].shape[0]
    n = w.shape[1]
    np_tiles, ns_tiles = mp // tm, ms // tm
    pmap = lambda j, i: (jnp.minimum(i, np_tiles - 1), 0)
    smap = lambda j, i: (jnp.maximum(i - np_tiles, 0), 0)
    pmap_o = lambda j, i: (jnp.minimum(i, np_tiles - 1), j)
    smap_o = lambda j, i: (jnp.maximum(i - np_tiles, 0), j)
    in_specs = ([pl.BlockSpec((tm, kp), pmap)] * nparts + [pl.BlockSpec((tm, kp), smap)] * nparts
                + [pl.BlockSpec((kp, tn), functools.partial(lambda j, i, p: (p, j), p=p)) for p in range(nparts)]
                + [pl.BlockSpec((tm, tn), pmap_o), pl.BlockSpec((tm, tn), smap_o)])
    return pl.pallas_call(
        functools.partial(_outproj_kernel, np_tiles=np_tiles, nparts=nparts),
        grid=(n // tn, np_tiles + ns_tiles),
        in_specs=in_specs,
        out_specs=[pl.BlockSpec((tm, tn), pmap_o), pl.BlockSpec((tm, tn), smap_o)],
        out_shape=[jax.ShapeDtypeStruct((mp, n), F32), jax.ShapeDtypeStruct((ms, n), F32)],
        scratch_shapes=[pltpu.VMEM((nparts, kp, tn), BF16)],
        compiler_params=_params("arbitrary", "arbitrary"),
        name="out_project",
    )(*parts_p, *parts_s, *([w] * nparts), res_p, res_s)


def _silu_mul(gc, u):
    return gc * (1.0 / (1.0 + jnp.exp(-gc))) * u


def _up_kernel(x_ref, wg_ref, wu_ref, wc_ref, hist_ref, h_ref, cp_ref, cs_ref, wb_ref, carry_ref,
               *, np_tiles, tiles_per_seq, nb_s, tn):
    i = pl.program_id(1)
    tm = x_ref.shape[0]

    @pl.when(i == 0)
    def _():
        wb_ref[:, :tn] = wg_ref[...].astype(BF16)
        wb_ref[:, tn:] = wu_ref[...].astype(BF16)

    w0 = wc_ref[0:1, :]
    w1 = wc_ref[1:2, :]
    w2 = wc_ref[2:3, :]

    @pl.when(lax.rem(i, tiles_per_seq) == 0)
    def _():
        carry_ref[...] = jnp.zeros_like(carry_ref)

    @pl.when(i < np_tiles)
    def _():
        gu = jnp.dot(x_ref[...], wb_ref[...], preferred_element_type=F32)
        g = gu[:, :tn]
        u = gu[:, tn:]
        c0 = carry_ref[0:1, :]
        c1 = carry_ref[1:2, :]
        row = _iota((tm, tn), 0)
        gm1 = jnp.where(row == 0, c1, pltpu.roll(g, 1, 0))
        gm2 = jnp.where(row == 0, c0, jnp.where(row == 1, c1, pltpu.roll(g, 2, 0)))
        gc = gm2 * w0 + gm1 * w1 + g * w2
        h_ref[...] = _silu_mul(gc, u).astype(h_ref.dtype)
        tail = g[tm - 2:tm, :]
        carry_ref[0:2, :] = tail
        cp_ref[0] = tail

    @pl.when(i >= np_tiles)
    def _():
        gu = jnp.dot(x_ref[...], wb_ref[...], preferred_element_type=F32)
        g = gu[:, :tn]
        u = gu[:, tn:]
        h0 = hist_ref[0]
        h1 = hist_ref[1]
        gm1 = jnp.concatenate([h1, g[:tm - nb_s]], axis=0)
        gm2 = jnp.concatenate([h0, h1, g[:tm - 2 * nb_s]], axis=0)
        gc = gm2 * w0 + gm1 * w1 + g * w2
        h_ref[...] = _silu_mul(gc, u).astype(h_ref.dtype)
        cs_ref[0] = g[tm - 2 * nb_s:tm - nb_s]
        cs_ref[1] = g[tm - nb_s:]


def up_convglu(x, w_up, w_conv, hist_t, mp, seq_len, tm=ROW_TILE, tn=256):
    m, k = x.shape
    f = w_up.shape[1] // 2
    nb_s = hist_t.shape[1]
    nb_p = mp // seq_len
    np_tiles = mp // tm
    assert m - mp == tm and CONV_W == 3
    tiles_per_seq = seq_len // tm
    nj = f // tn
    return pl.pallas_call(
        functools.partial(_up_kernel, np_tiles=np_tiles, tiles_per_seq=tiles_per_seq, nb_s=nb_s, tn=tn),
        grid=(nj, np_tiles + 1),
        in_specs=[
            pl.BlockSpec((tm, k), lambda j, i: (i, 0)),
            pl.BlockSpec((k, tn), lambda j, i: (0, j)),
            pl.BlockSpec((k, tn), lambda j, i: (0, nj + j)),
            pl.BlockSpec((CONV_W, tn), lambda j, i: (0, j)),
            pl.BlockSpec((CONV_W - 1, nb_s, tn), lambda j, i: (0, 0, j)),
        ],
        out_specs=[
            pl.BlockSpec((tm, tn), lambda j, i: (i, j)),
            pl.BlockSpec((1, CONV_W - 1, tn), lambda j, i: (jnp.minimum(i // tiles_per_seq, nb_p - 1), 0, j)),
            pl.BlockSpec((CONV_W - 1, nb_s, tn), lambda j, i: (0, 0, j)),
        ],
        out_shape=[
            jax.ShapeDtypeStruct((m, f), BF16),
            jax.ShapeDtypeStruct((nb_p, CONV_W - 1, f), F32),
            jax.ShapeDtypeStruct((CONV_W - 1, nb_s, f), F32),
        ],
        scratch_shapes=[pltpu.VMEM((k, 2 * tn), BF16), pltpu.VMEM((8, tn), F32)],
        compiler_params=_params("arbitrary", "arbitrary"),
        name="up_convglu",
    )(x, w_up, w_up, w_conv, hist_t)


def _down_kernel(a_ref, w_ref, rp_ref, rs_ref, op_ref, os_ref, *, np_tiles):
    i = pl.program_id(1)

    @pl.when(i < np_tiles)
    def _():
        op_ref[...] = rp_ref[...] + jnp.dot(a_ref[...], w_ref[...], preferred_element_type=F32)

    @pl.when(i >= np_tiles)
    def _():
        os_ref[...] = rs_ref[...] + jnp.dot(a_ref[...], w_ref[...], preferred_element_type=F32)


def down_project(a, w, res_p, res_s, tm=DOWN_ROW_TILE, tn=512):
    m, f = a.shape
    n = w.shape[1]
    mp = res_p.shape[0]
    ms = m - mp
    np_tiles, ns_tiles = mp // tm, ms // tm
    pmap_o = lambda j, i: (jnp.minimum(i, np_tiles - 1), j)
    smap_o = lambda j, i: (jnp.maximum(i - np_tiles, 0), j)
    return pl.pallas_call(
        functools.partial(_down_kernel, np_tiles=np_tiles),
        grid=(n // tn, np_tiles + ns_tiles),
        in_specs=[
            pl.BlockSpec((tm, f), lambda j, i: (i, 0)),
            pl.BlockSpec((f, tn), lambda j, i: (0, j)),
            pl.BlockSpec((tm, tn), pmap_o),
            pl.BlockSpec((tm, tn), smap_o),
        ],
        out_specs=[pl.BlockSpec((tm, tn), pmap_o), pl.BlockSpec((tm, tn), smap_o)],
        out_shape=[jax.ShapeDtypeStruct((mp, n), F32), jax.ShapeDtypeStruct((ms, n), F32)],
        compiler_params=_params("arbitrary", "arbitrary"),
        name="down_project",
    )(a, w, res_p, res_s)


def _sb_block(q, k, v, ue, run, acc, vis):
    tk = k.shape[0]
    z = lax.dot_general(q, k, (((1,), (1,)), ((), ())), preferred_element_type=F32) * SCALE
    ls = -(jnp.maximum(z, 0.0) + jnp.log1p(jnp.exp(-jnp.abs(z))))
    if vis is not None:
        ls = jnp.where(vis, ls, 0.0)
    hi = ls.astype(BF16)
    lo = (ls - hi.astype(F32)).astype(BF16)
    cs = jnp.dot(hi, ue, preferred_element_type=F32) + jnp.dot(lo, ue, preferred_element_type=F32)
    w = jnp.exp(z + ls + cs[:, :tk] + run)
    if vis is not None:
        w = jnp.where(vis, w, 0.0)
    acc = acc + jnp.dot(w.astype(BF16), v, preferred_element_type=F32)
    return run + cs[:, tk:], acc


def _sb_core(q, kb_ref, vb_ref, ue, q0, nfull, tq, tk):
    nd = -(-tq // tk)
    run = jnp.zeros((tq, LANES), F32)
    acc = jnp.zeros((tq, D_HEAD), F32)
    qpos = q0 + _iota((tq, tk), 0)
    koff = _iota((tq, tk), 1)
    for d in reversed(range(nd)):
        k0 = q0 + d * tk
        if not isinstance(k0, int):
            k0 = pl.multiple_of(k0, tk)
        run, acc = _sb_block(q, kb_ref[pl.ds(k0, tk), :], vb_ref[pl.ds(k0, tk), :], ue, run, acc,
                             (k0 + koff) < qpos)

    def body(t, carry):
        k0 = pl.multiple_of((nfull - 1 - t) * tk, tk)
        return _sb_block(q, kb_ref[pl.ds(k0, tk), :], vb_ref[pl.ds(k0, tk), :], ue, carry[0], carry[1], None)

    run, acc = lax.fori_loop(0, nfull, body, (run, acc))
    return acc


def _sb_prompt_kernel(q_ref, k_ref, v_ref, ue_ref, o_ref, kb_ref, vb_ref, *, tq, tk):
    qi = pl.program_id(2)

    @pl.when(qi == 0)
    def _():
        kb_ref[...] = k_ref[...].astype(BF16)
        vb_ref[...] = v_ref[...].astype(BF16)

    acc = _sb_core(q_ref[...].astype(BF16), kb_ref, vb_ref, ue_ref[...], qi * tq, qi * (tq // tk), tq, tk)
    o_ref[...] = acc.astype(o_ref.dtype)


def _suffix_matrix(tk):
    j = jnp.arange(tk)[:, None]
    s = jnp.arange(tk)[None, :]
    return jnp.concatenate([(j > s).astype(BF16), jnp.ones((tk, LANES), BF16)], axis=1)


def sb_attention_prompt(q, k, v, nb, seq, tq=SB_TQ, tk=SB_TK):
    nh = q.shape[1] // D_HEAD
    kv_spec = pl.BlockSpec((seq, D_HEAD), lambda b, h, qi: (b, h))
    q_spec = pl.BlockSpec((tq, D_HEAD), lambda b, h, qi: (b * (seq // tq) + qi, h))
    return pl.pallas_call(
        functools.partial(_sb_prompt_kernel, tq=tq, tk=tk),
        grid=(nb, nh, seq // tq),
        in_specs=[q_spec, kv_spec, kv_spec, pl.BlockSpec((tk, tk + LANES), lambda b, h, qi: (0, 0))],
        out_specs=q_spec,
        out_shape=jax.ShapeDtypeStruct(q.shape, BF16),
        scratch_shapes=[pltpu.VMEM((seq, D_HEAD), BF16), pltpu.VMEM((seq, D_HEAD), BF16)],
        compiler_params=_params("arbitrary", "arbitrary", "arbitrary"),
        name="sb_attention_prompt",
    )(q, k, v, _suffix_matrix(tk))


def _sb_sample_kernel(q_ref, kn_ref, vn_ref, kc_ref, vc_ref, ue_ref, o_ref, kb_ref, vb_ref, *, ts, tk, past):
    zeros = jnp.zeros((tk, D_HEAD), BF16)
    kb_ref[pl.ds(0, past), :] = kc_ref[...].astype(BF16)
    vb_ref[pl.ds(0, past), :] = vc_ref[...].astype(BF16)
    kb_ref[pl.ds(past, tk), :] = zeros
    vb_ref[pl.ds(past, tk), :] = zeros
    kb_ref[pl.ds(past, ts), :] = kn_ref[...].astype(BF16)
    vb_ref[pl.ds(past, ts), :] = vn_ref[...].astype(BF16)
    acc = _sb_core(q_ref[...].astype(BF16), kb_ref, vb_ref, ue_ref[...], past, past // tk, ts, tk)
    o_ref[...] = acc.astype(o_ref.dtype)


def sb_attention_sample(q, kn, vn, kc, vc, nb, ts, past, tk=SB_TK):
    nh = q.shape[1] // D_HEAD
    assert past % tk == 0 and ts <= tk
    new_spec = pl.BlockSpec((ts, D_HEAD), lambda b, h: (b, h))
    cache_spec = pl.BlockSpec((past, D_HEAD), lambda b, h: (b, h))
    return pl.pallas_call(
        functools.partial(_sb_sample_kernel, ts=ts, tk=tk, past=past),
        grid=(nb, nh),
        in_specs=[new_spec, new_spec, new_spec, cache_spec, cache_spec,
                  pl.BlockSpec((tk, tk + LANES), lambda b, h: (0, 0))],
        out_specs=new_spec,
        out_shape=jax.ShapeDtypeStruct(q.shape, BF16),
        scratch_shapes=[pltpu.VMEM((past + tk, D_HEAD), BF16), pltpu.VMEM((past + tk, D_HEAD), BF16)],
        compiler_params=_params("arbitrary", "arbitrary"),
        name="sb_attention_sample",
    )(q, kn, vn, kc, vc, _suffix_matrix(tk))


def _band_bias_kernel(tab_ref, o_ref, *, rows, cols, width):
    h = pl.program_id(0)
    band = BAND_CHUNKS * CHUNK
    x = _iota((8, width), 1)
    idx = jnp.clip(x - rows - band, -REL_CLIP, REL_CLIP) + REL_CLIP

    def body(r, g):
        return jnp.where(idx == r, tab_ref[h, r], g)

    g = lax.fori_loop(0, 2 * REL_CLIP + 1, body, jnp.zeros((8, width), F32))
    gt = jnp.broadcast_to(g[0:1, :], (rows, width))
    gt = pltpu.roll(gt, 0, 1, stride=1, stride_axis=0)
    bias = gt[:, rows:rows + cols]
    dc = _iota((rows, cols), 1) // CHUNK - _iota((rows, cols), 0) // CHUNK
    o_ref[0] = jnp.where((dc >= 0) & (dc <= BAND_CHUNKS), bias, NEG)


def band_bias(table, qchunks):
    nh = table.shape[0]
    rows = qchunks * CHUNK
    cols = (qchunks + BAND_CHUNKS) * CHUNK
    width = rows + cols
    assert width % LANES == 0 and rows % LANES == 0
    return pl.pallas_call(
        functools.partial(_band_bias_kernel, rows=rows, cols=cols, width=width),
        grid=(nh,),
        in_specs=[pl.BlockSpec(memory_space=pltpu.SMEM)],
        out_specs=pl.BlockSpec((1, rows, cols), lambda h: (h, 0, 0)),
        out_shape=jax.ShapeDtypeStruct((nh, rows, cols), F32),
        compiler_params=_params("arbitrary"),
        name="band_bias",
    )(table)


def _softmax_rows(s):
    m = jnp.max(s, axis=-1, keepdims=True)
    p = jnp.exp(s - m)
    return p / jnp.sum(p, axis=-1, keepdims=True)


def _band_prompt_kernel(q_ref, k_ref, v_ref, bias_ref, gq_ref, gk_ref, o_ref, kt_ref, vt_ref, kb_ref, vb_ref,
                        *, seq, rows, cols, keep):
    band = BAND_CHUNKS * CHUNK
    kn = _rms(k_ref[...], gk_ref[...])
    kt_ref[...] = kn[seq - keep:, :]
    vt_ref[...] = v_ref[pl.ds(seq - keep, keep), :]
    kb_ref[pl.ds(0, band), :] = jnp.zeros((band, D_HEAD), BF16)
    vb_ref[pl.ds(0, band), :] = jnp.zeros((band, D_HEAD), BF16)
    kb_ref[pl.ds(band, seq), :] = kn.astype(BF16)
    vb_ref[pl.ds(band, seq), :] = v_ref[...].astype(BF16)
    bias = bias_ref[0]
    gq = gq_ref[...]
    col = _iota((rows, cols), 1)

    def body(t, carry):
        r0 = pl.multiple_of(t * rows, rows)
        q = _rms(q_ref[pl.ds(r0, rows), :], gq).astype(BF16)
        k = kb_ref[pl.ds(r0, cols), :]
        v = vb_ref[pl.ds(r0, cols), :]
        s = lax.dot_general(q, k, (((1,), (1,)), ((), ())), preferred_element_type=F32) * SCALE + bias
        s = jnp.where(col >= band - r0, s, NEG)
        p = _softmax_rows(s)
        o_ref[pl.ds(r0, rows), :] = jnp.dot(p.astype(BF16), v, preferred_element_type=F32).astype(o_ref.dtype)
        return carry

    lax.fori_loop(0, seq // rows, body, 0)


def band_attention_prompt(q, k, v, bias, g_q, g_k, nb, seq, keep):
    nh = q.shape[1] // D_HEAD
    rows, cols = bias.shape[1], bias.shape[2]
    band = BAND_CHUNKS * CHUNK
    blk = pl.BlockSpec((seq, D_HEAD), lambda b, h: (b, h))
    tail = pl.BlockSpec((keep, D_HEAD), lambda b, h: (b, h))
    gspec = pl.BlockSpec((1, D_HEAD), lambda b, h: (0, 0))
    return pl.pallas_call(
        functools.partial(_band_prompt_kernel, seq=seq, rows=rows, cols=cols, keep=keep),
        grid=(nb, nh),
        in_specs=[blk, blk, blk, pl.BlockSpec((1, rows, cols), lambda b, h: (h, 0, 0)), gspec, gspec],
        out_specs=[blk, tail, tail],
        out_shape=[jax.ShapeDtypeStruct(q.shape, BF16),
                   jax.ShapeDtypeStruct((nb * keep, q.shape[1]), F32),
                   jax.ShapeDtypeStruct((nb * keep, q.shape[1]), F32)],
        scratch_shapes=[pltpu.VMEM((band + seq, D_HEAD), BF16), pltpu.VMEM((band + seq, D_HEAD), BF16)],
        compiler_params=_params("arbitrary", "arbitrary"),
        name="band_attention_prompt",
    )(q, k, v, bias, g_q.reshape(1, D_HEAD), g_k.reshape(1, D_HEAD))


def _band_sample_kernel(q_ref, kn_ref, vn_ref, kc_ref, vc_ref, bias_ref, gq_ref, gk_ref, o_ref, ko_ref,
                        kb_ref, vb_ref, *, ts, w, cols):
    kn = _rms(kn_ref[...], gk_ref[...])
    ko_ref[...] = kn
    kb_ref[pl.ds(0, w), :] = kc_ref[...].astype(BF16)
    vb_ref[pl.ds(0, w), :] = vc_ref[...].astype(BF16)
    kb_ref[pl.ds(w, cols - w), :] = jnp.zeros((cols - w, D_HEAD), BF16)
    vb_ref[pl.ds(w, cols - w), :] = jnp.zeros((cols - w, D_HEAD), BF16)
    kb_ref[pl.ds(w, ts), :] = kn.astype(BF16)
    vb_ref[pl.ds(w, ts), :] = vn_ref[...].astype(BF16)
    q = _rms(q_ref[...], gq_ref[...]).astype(BF16)
    s = lax.dot_general(q, kb_ref[...], (((1,), (1,)), ((), ())), preferred_element_type=F32) * SCALE + bias_ref[0]
    s = jnp.where(_iota((ts, cols), 1) < w + ts, s, NEG)
    p = _softmax_rows(s)
    o_ref[...] = jnp.dot(p.astype(BF16), vb_ref[...], preferred_element_type=F32).astype(o_ref.dtype)


def band_attention_sample(q, kn, vn, kc, vc, bias, g_q, g_k, nb, ts, w):
    nh = q.shape[1] // D_HEAD
    cols = -(-(w + ts) // LANES) * LANES
    assert w == BAND_CHUNKS * CHUNK and ts <= CHUNK and cols <= bias.shape[2] and ts % 8 == 0
    new_spec = pl.BlockSpec((ts, D_HEAD), lambda b, h: (b, h))
    cache_spec = pl.BlockSpec((w, D_HEAD), lambda b, h: (b, h))
    gspec = pl.BlockSpec((1, D_HEAD), lambda b, h: (0, 0))
    return pl.pallas_call(
        functools.partial(_band_sample_kernel, ts=ts, w=w, cols=cols),
        grid=(nb, nh),
        in_specs=[new_spec, new_spec, new_spec, cache_spec, cache_spec,
                  pl.BlockSpec((1, ts, cols), lambda b, h: (h, 0, 0)), gspec, gspec],
        out_specs=[new_spec, new_spec],
        out_shape=[jax.ShapeDtypeStruct(q.shape, BF16), jax.ShapeDtypeStruct(q.shape, F32)],
        scratch_shapes=[pltpu.VMEM((cols, D_HEAD), BF16), pltpu.VMEM((cols, D_HEAD), BF16)],
        compiler_params=_params("arbitrary", "arbitrary"),
        name="band_attention_sample",
    )(q, kn, vn, kc, vc, bias, g_q.reshape(1, D_HEAD), g_k.reshape(1, D_HEAD))


def _norm_rope(x, g, cosf, sina, sinb):
    half = ROT_DIM // 2
    y = _rms(x, g)
    return y * cosf + pltpu.roll(y, D_HEAD - half, 1) * sina + pltpu.roll(y, half, 1) * sinb


def rope_tables(pos):
    half = ROT_DIM // 2
    inv_freq = ROPE_THETA ** (-jnp.arange(half, dtype=F32) * (2.0 / ROT_DIM))
    ang = pos.astype(F32)[:, None] * inv_freq[None, :]
    cos, sin = jnp.cos(ang), jnp.sin(ang)
    n = pos.shape[0]
    rest = D_HEAD - ROT_DIM
    cosf = jnp.concatenate([cos, cos, jnp.ones((n, rest), F32)], axis=1)
    sina = jnp.concatenate([-sin, jnp.zeros((n, rest + half), F32)], axis=1)
    sinb = jnp.concatenate([jnp.zeros((n, half), F32), sin, jnp.zeros((n, rest), F32)], axis=1)
    return cosf, sina, sinb


def _online_block(q, k, v, mask, m, l, acc):
    s = lax.dot_general(q, k, (((1,), (1,)), ((), ())), preferred_element_type=F32)
    if mask is not None:
        s = jnp.where(mask, s, NEG)
    m_new = jnp.maximum(m, jnp.max(s, axis=-1, keepdims=True))
    alpha = jnp.exp(m - m_new)
    p = jnp.exp(s - m_new)
    l = alpha * l + jnp.sum(p, axis=-1, keepdims=True)
    acc = alpha * acc + jnp.dot(p.astype(BF16), v, preferred_element_type=F32)
    return m_new, l, acc


def _diff_lambda(lq1, lk1, lq2, lk2):
    return (jnp.exp(jnp.sum(lq1 * lk1, axis=-1, keepdims=True))
            - jnp.exp(jnp.sum(lq2 * lk2, axis=-1, keepdims=True)) + LAMBDA_INIT)


def _diff_finish(state, lam, gsub):
    (_, l1, a1), (_, l2, a2) = state
    o = a1 / l1 - lam * (a2 / l2)
    return _rms(o, gsub) * (1.0 - LAMBDA_INIT)


def _diff_init(tq):
    return (jnp.full((tq, 1), NEG, F32), jnp.zeros((tq, 1), F32), jnp.zeros((tq, 2 * D_HEAD), F32))


def _diff_prompt_kernel(q_ref, k_ref, v_ref, cos_ref, sa_ref, sb_ref, gq_ref, gk_ref, lam_ref, gsub_ref,
                        o_ref, ko_ref, k1_ref, k2_ref, vb_ref, *, t):
    qi = pl.program_id(2)
    d = D_HEAD

    @pl.when(qi == 0)
    def _():
        gk = gk_ref[...]
        cosf, sina, sinb = cos_ref[...], sa_ref[...], sb_ref[...]
        k1 = _norm_rope(k_ref[:, :d], gk, cosf, sina, sinb)
        k2 = _norm_rope(k_ref[:, d:], gk, cosf, sina, sinb)
        ko_ref[:, :d] = k1
        ko_ref[:, d:] = k2
        k1_ref[...] = k1.astype(BF16)
        k2_ref[...] = k2.astype(BF16)
        vb_ref[...] = v_ref[...].astype(BF16)

    r0 = pl.multiple_of(qi * t, t)
    gq = gq_ref[...]
    cosf, sina, sinb = cos_ref[pl.ds(r0, t), :], sa_ref[pl.ds(r0, t), :], sb_ref[pl.ds(r0, t), :]
    q1 = (_norm_rope(q_ref[:, :d], gq, cosf, sina, sinb) * SCALE).astype(BF16)
    q2 = (_norm_rope(q_ref[:, d:], gq, cosf, sina, sinb) * SCALE).astype(BF16)

    def step(k0, state, mask):
        v = vb_ref[pl.ds(k0, t), :]
        s1 = _online_block(q1, k1_ref[pl.ds(k0, t), :], v, mask, *state[0])
        s2 = _online_block(q2, k2_ref[pl.ds(k0, t), :], v, mask, *state[1])
        return s1, s2

    def body(kb, state):
        return step(pl.multiple_of(kb * t, t), state, None)

    state = lax.fori_loop(0, qi, body, (_diff_init(t), _diff_init(t)))
    mask = (_iota((t, t), 1) // CHUNK) <= (_iota((t, t), 0) // CHUNK)
    state = step(r0, state, mask)
    lam = _diff_lambda(lam_ref[0:1, :], lam_ref[1:2, :], lam_ref[2:3, :], lam_ref[3:4, :])
    o_ref[...] = _diff_finish(state, lam, gsub_ref[...]).astype(o_ref.dtype)


def diff_attention_prompt(q, k, v, tabs, g_q, g_k, lam4, g_sub, nb, seq, t=DIFF_T):
    d2 = 2 * D_HEAD
    nh = q.shape[1] // d2
    assert t % CHUNK == 0
    kv_spec = pl.BlockSpec((seq, d2), lambda b, h, qi: (b, h))
    q_spec = pl.BlockSpec((t, d2), lambda b, h, qi: (b * (seq // t) + qi, h))
    tab_spec = pl.BlockSpec((seq, D_HEAD), lambda b, h, qi: (0, 0))
    g_spec = pl.BlockSpec((1, D_HEAD), lambda b, h, qi: (0, 0))
    return pl.pallas_call(
        functools.partial(_diff_prompt_kernel, t=t),
        grid=(nb, nh, seq // t),
        in_specs=[q_spec, kv_spec, kv_spec, tab_spec, tab_spec, tab_spec, g_spec, g_spec,
                  pl.BlockSpec((4, D_HEAD), lambda b, h, qi: (0, 0)),
                  pl.BlockSpec((1, d2), lambda b, h, qi: (0, 0))],
        out_specs=[q_spec, kv_spec],
        out_shape=[jax.ShapeDtypeStruct(q.shape, BF16), jax.ShapeDtypeStruct(k.shape, F32)],
        scratch_shapes=[pltpu.VMEM((seq, D_HEAD), BF16), pltpu.VMEM((seq, D_HEAD), BF16),
                        pltpu.VMEM((seq, d2), BF16)],
        compiler_params=_params("arbitrary", "arbitrary", "arbitrary"),
        name="diff_attention_prompt",
    )(q, k, v, *tabs, g_q.reshape(1, D_HEAD), g_k.reshape(1, D_HEAD), lam4, g_sub.reshape(1, d2))


def _diff_sample_kernel(q_ref, kn_ref, vn_ref, kc_ref, vc_ref, cos_ref, sa_ref, sb_ref, gq_ref, gk_ref,
                        lam_ref, gsub_ref, o_ref, ko_ref, k1_ref, k2_ref, vb_ref, *, ts, past, t, tn):
    d = D_HEAD
    cosf, sina, sinb = cos_ref[...], sa_ref[...], sb_ref[...]
    gk = gk_ref[...]
    k1 = _norm_rope(kn_ref[:, :d], gk, cosf, sina, sinb)
    k2 = _norm_rope(kn_ref[:, d:], gk, cosf, sina, sinb)
    ko_ref[:, :d] = k1
    ko_ref[:, d:] = k2
    k1_ref[pl.ds(0, past), :] = kc_ref[:, :d].astype(BF16)
    k2_ref[pl.ds(0, past), :] = kc_ref[:, d:].astype(BF16)
    vb_ref[pl.ds(0, past), :] = vc_ref[...].astype(BF16)
    k1_ref[pl.ds(past, tn), :] = jnp.zeros((tn, d), BF16)
    k2_ref[pl.ds(past, tn), :] = jnp.zeros((tn, d), BF16)
    vb_ref[pl.ds(past, tn), :] = jnp.zeros((tn, 2 * d), BF16)
    k1_ref[pl.ds(past, ts), :] = k1.astype(BF16)
    k2_ref[pl.ds(past, ts), :] = k2.astype(BF16)
    vb_ref[pl.ds(past, ts), :] = vn_ref[...].astype(BF16)
    gq = gq_ref[...]
    q1 = (_norm_rope(q_ref[:, :d], gq, cosf, sina, sinb) * SCALE).astype(BF16)
    q2 = (_norm_rope(q_ref[:, d:], gq, cosf, sina, sinb) * SCALE).astype(BF16)

    def step(k0, n, state, mask):
        v = vb_ref[pl.ds(k0, n), :]
        s1 = _online_block(q1, k1_ref[pl.ds(k0, n), :], v, mask, *state[0])
        s2 = _online_block(q2, k2_ref[pl.ds(k0, n), :], v, mask, *state[1])
        return s1, s2

    def body(kb, state):
        return step(pl.multiple_of(kb * t, t), t, state, None)

    state = lax.fori_loop(0, past // t, body, (_diff_init(ts), _diff_init(ts)))
    kpos = past + _iota((ts, tn), 1)
    qpos = past + _iota((ts, tn), 0)
    mask = ((kpos // CHUNK) <= (qpos // CHUNK)) & (kpos < past + ts)
    state = step(past, tn, state, mask)
    lam = _diff_lambda(lam_ref[0:1, :], lam_ref[1:2, :], lam_ref[2:3, :], lam_ref[3:4, :])
    o_ref[...] = _diff_finish(state, lam, gsub_ref[...]).astype(o_ref.dtype)


def diff_attention_sample(q, kn, vn, kc, vc, tabs, g_q, g_k, lam4, g_sub, nb, ts, past, t=DIFF_T, tn=LANES):
    d2 = 2 * D_HEAD
    nh = q.shape[1] // d2
    assert past % t == 0 and ts <= tn
    new_spec = pl.BlockSpec((ts, d2), lambda b, h: (b, h))
    cache_spec = pl.BlockSpec((past, d2), lambda b, h: (b, h))
    tab_spec = pl.BlockSpec((ts, D_HEAD), lambda b, h: (0, 0))
    g_spec = pl.BlockSpec((1, D_HEAD), lambda b, h: (0, 0))
    return pl.pallas_call(
        functools.partial(_diff_sample_kernel, ts=ts, past=past, t=t, tn=tn),
        grid=(nb, nh),
        in_specs=[new_spec, new_spec, new_spec, cache_spec, cache_spec, tab_spec, tab_spec, tab_spec,
                  g_spec, g_spec, pl.BlockSpec((4, D_HEAD), lambda b, h: (0, 0)),
                  pl.BlockSpec((1, d2), lambda b, h: (0, 0))],
        out_specs=[new_spec, new_spec],
        out_shape=[jax.ShapeDtypeStruct(q.shape, BF16), jax.ShapeDtypeStruct(q.shape, F32)],
        scratch_shapes=[pltpu.VMEM((past + tn, D_HEAD), BF16), pltpu.VMEM((past + tn, D_HEAD), BF16),
                        pltpu.VMEM((past + tn, d2), BF16)],
        compiler_params=_params("arbitrary", "arbitrary"),
        name="diff_attention_sample",
    )(q, kn, vn, kc, vc, *tabs, g_q.reshape(1, D_HEAD), g_k.reshape(1, D_HEAD), lam4, g_sub.reshape(1, d2))


def _to_time_major(x, nb, ts):
    return x.reshape(nb, ts, -1).transpose(1, 0, 2).reshape(nb * ts, -1)


def _to_batch_major(x, nb, ts):
    return x.reshape(ts, nb, -1).transpose(1, 0, 2).reshape(nb * ts, -1)


def _conv_glu_layer(hp, hs, g_norm, w_up, w_conv, w_down, state, nb_s, ts, seq):
    mp = hp.shape[0]
    hs_t = _to_time_major(hs, nb_s, ts)
    xn = rmsnorm_rows(hp, hs_t, g_norm)
    hidden, conv_p, conv_s_t = up_convglu(xn, w_up, w_conv, state.transpose(1, 0, 2), mp, seq)
    hp, hs_t = down_project(hidden, w_down.astype(BF16), hp, hs_t)
    return hp, _to_batch_major(hs_t, nb_s, ts), conv_p, conv_s_t.transpose(1, 0, 2)


def kernel(x_prompt, x_sample, cache_k_sb, cache_v_sb, cache_k_band, cache_v_band, cache_k_diff, cache_v_diff, state_conv_ffn, norm_mix, norm_ffn, w_in_ab, w_out_ab, g_q_band, g_k_band, rel_bias_band, w_in_diff, w_out_diff, g_q_diff, g_k_diff, lambda_q1, lambda_k1, lambda_q2, lambda_k2, g_sub_diff, w_up, w_conv, w_down):
    bp, tp, dm = x_prompt.shape
    bs, ts, _ = x_sample.shape
    past = cache_k_sb.shape[1]
    h_sb, h_cb, h_df = cache_k_sb.shape[2], cache_k_band.shape[2], cache_k_diff.shape[2]
    d_sb, d_cb, d_df = h_sb * D_HEAD, h_cb * D_HEAD, h_df * 2 * D_HEAD
    w_band = cache_k_band.shape[1]
    keep = min(BAND_CHUNKS * CHUNK, tp)
    mp, ms = bp * tp, bs * ts

    hp = x_prompt.reshape(mp, dm)
    hs = x_sample.reshape(ms, dm)

    xn = rmsnorm_rows(hp, hs, norm_mix[0])
    sec = []
    col = 0
    for width in (d_sb, d_sb, d_sb, d_cb, d_cb, d_cb):
        sec.append(project(xn, w_in_ab, col, width, mp))
        col += width
    (qa_p, qa_s), (ka_p, ka_s), (va_p, va_s), (qb_p, qb_s), (kb_p, kb_s), (vb_p, vb_s) = sec

    oa_p = sb_attention_prompt(qa_p, ka_p, va_p, bp, tp)
    oa_s = sb_attention_sample(qa_s, ka_s, va_s, cache_k_sb.reshape(bs * past, d_sb),
                               cache_v_sb.reshape(bs * past, d_sb), bs, ts, past)
    bias = band_bias(rel_bias_band, BAND_QCHUNKS)
    ob_p, k_band_p, v_band_p = band_attention_prompt(qb_p, kb_p, vb_p, bias, g_q_band, g_k_band, bp, tp, keep)
    ob_s, k_band_s = band_attention_sample(qb_s, kb_s, vb_s, cache_k_band.reshape(bs * w_band, d_cb),
                                           cache_v_band.reshape(bs * w_band, d_cb), bias, g_q_band, g_k_band,
                                           bs, ts, w_band)
    hp, hs = out_project([oa_p, ob_p], [oa_s, ob_s], w_out_ab, hp, hs)
    hp, hs, conv_p0, conv_s0 = _conv_glu_layer(hp, hs, norm_ffn[0], w_up[0], w_conv[0], w_down[0],
                                               state_conv_ffn[0], bs, ts, tp)

    xn = rmsnorm_rows(hp, hs, norm_mix[1])
    (q_p, q_s), (k_p, k_s), (v_p, v_s) = [project(xn, w_in_diff, c * d_df, d_df, mp) for c in range(3)]
    lam4 = jnp.stack([lambda_q1, lambda_k1, lambda_q2, lambda_k2]).astype(F32)
    o_p, k_diff_p = diff_attention_prompt(q_p, k_p, v_p, rope_tables(jnp.arange(tp)), g_q_diff, g_k_diff,
                                          lam4, g_sub_diff, bp, tp)
    o_s, k_diff_s = diff_attention_sample(q_s, k_s, v_s, cache_k_diff.reshape(bs * past, d_df),
                                          cache_v_diff.reshape(bs * past, d_df),
                                          rope_tables(past + jnp.arange(ts)), g_q_diff, g_k_diff,
                                          lam4, g_sub_diff, bs, ts, past)
    hp, hs = out_project([o_p], [o_s], w_out_diff, hp, hs)
    hp, hs, conv_p1, conv_s1 = _conv_glu_layer(hp, hs, norm_ffn[1], w_up[1], w_conv[1], w_down[1],
                                               state_conv_ffn[1], bs, ts, tp)

    return (hp.reshape(bp, tp, dm), hs.reshape(bs, ts, dm),
            ka_p.reshape(bp, tp, h_sb, D_HEAD), va_p.reshape(bp, tp, h_sb, D_HEAD),
            k_band_p.reshape(bp, keep, h_cb, D_HEAD), v_band_p.reshape(bp, keep, h_cb, D_HEAD),
            k_diff_p.reshape(bp, tp, h_df, 2 * D_HEAD), v_p.reshape(bp, tp, h_df, 2 * D_HEAD),
            jnp.stack([conv_p0, conv_p1]),
            ka_s.reshape(bs, ts, h_sb, D_HEAD), va_s.reshape(bs, ts, h_sb, D_HEAD),
            k_band_s.reshape(bs, ts, h_cb, D_HEAD), vb_s.reshape(bs, ts, h_cb, D_HEAD),
            k_diff_s.reshape(bs, ts, h_df, 2 * D_HEAD), v_s.reshape(bs, ts, h_df, 2 * D_HEAD),
            jnp.stack([conv_s0, conv_s1]))
```

```python
import functools
import math

import jax
import jax.numpy as jnp
from jax import lax
from jax.experimental import pallas as pl
from jax.experimental.pallas import tpu as pltpu

F32 = jnp.float32
BF16 = jnp.bfloat16

CHUNK = 64
D_HEAD = 128
BAND_CHUNKS = 8
REL_CLIP = 128
ROT_DIM = D_HEAD // 4
ROPE_THETA = 500000.0
CONV_W = 3
EPS = 1e-6
NEG = -1e30
LAMBDA_INIT = 0.8 - 0.6 * math.exp(-0.3 * 1)
SCALE = 1.0 / math.sqrt(D_HEAD)
LOG2E = math.log2(math.e)

V7X_VMEM_LIMIT_BYTES = 56 * 1024 * 1024
LANES = 128

ROW_TILE = 512
DOWN_ROW_TILE = 256
SB_T = 512
SB_CB = 256
BAND_QCHUNKS = 4
DIFF_T = 512


def _params(*sem):
    return pltpu.CompilerParams(dimension_semantics=sem, vmem_limit_bytes=V7X_VMEM_LIMIT_BYTES)


def _iota(shape, dim):
    return lax.broadcasted_iota(jnp.int32, shape, dim)


def _rms(x, g):
    return x * lax.rsqrt(jnp.mean(x * x, axis=-1, keepdims=True) + EPS) * g


def _rmsnorm_kernel(xp_ref, xs_ref, g_ref, o_ref, *, np_tiles):
    i = pl.program_id(0)

    @pl.when(i < np_tiles)
    def _():
        o_ref[...] = _rms(xp_ref[...], g_ref[...]).astype(o_ref.dtype)

    @pl.when(i >= np_tiles)
    def _():
        o_ref[...] = _rms(xs_ref[...], g_ref[...]).astype(o_ref.dtype)


def rmsnorm_rows(xp, xs, g, tm=256):
    mp, d = xp.shape
    ms = xs.shape[0]
    np_tiles, ns_tiles = mp // tm, ms // tm
    return pl.pallas_call(
        functools.partial(_rmsnorm_kernel, np_tiles=np_tiles),
        grid=(np_tiles + ns_tiles,),
        in_specs=[
            pl.BlockSpec((tm, d), lambda i: (jnp.minimum(i, np_tiles - 1), 0)),
            pl.BlockSpec((tm, d), lambda i: (jnp.maximum(i - np_tiles, 0), 0)),
            pl.BlockSpec((1, d), lambda i: (0, 0)),
        ],
        out_specs=pl.BlockSpec((tm, d), lambda i: (i, 0)),
        out_shape=jax.ShapeDtypeStruct((mp + ms, d), BF16),
        compiler_params=_params("arbitrary"),
        name="rmsnorm_rows",
    )(xp, xs, g.reshape(1, d))


def _proj_kernel(x_ref, w_ref, op_ref, os_ref, wb_ref, *, np_tiles):
    i = pl.program_id(1)

    @pl.when(i == 0)
    def _():
        wb_ref[...] = w_ref[...].astype(BF16)

    @pl.when(i < np_tiles)
    def _():
        op_ref[...] = jnp.dot(x_ref[...], wb_ref[...], preferred_element_type=F32)

    @pl.when(i >= np_tiles)
    def _():
        os_ref[...] = jnp.dot(x_ref[...], wb_ref[...], preferred_element_type=F32)


def project(x, w, col0, n, mp, tm=ROW_TILE, tn=512):
    m, k = x.shape
    ms = m - mp
    np_tiles, ns_tiles = mp // tm, ms // tm
    jb = col0 // tn
    return pl.pallas_call(
        functools.partial(_proj_kernel, np_tiles=np_tiles),
        grid=(n // tn, np_tiles + ns_tiles),
        in_specs=[
            pl.BlockSpec((tm, k), lambda j, i: (i, 0)),
            pl.BlockSpec((k, tn), lambda j, i: (0, jb + j)),
        ],
        out_specs=[
            pl.BlockSpec((tm, tn), lambda j, i: (jnp.minimum(i, np_tiles - 1), j)),
            pl.BlockSpec((tm, tn), lambda j, i: (jnp.maximum(i - np_tiles, 0), j)),
        ],
        out_shape=[jax.ShapeDtypeStruct((mp, n), F32), jax.ShapeDtypeStruct((ms, n), F32)],
        scratch_shapes=[pltpu.VMEM((k, tn), BF16)],
        compiler_params=_params("arbitrary", "arbitrary"),
        name="project",
    )(x, w)


def _outproj_kernel(*refs, np_tiles, nparts):
    ap = refs[0:nparts]
    a_s = refs[nparts:2 * nparts]
    w = refs[2 * nparts:3 * nparts]
    rp_ref, rs_ref, op_ref, os_ref, wb_ref = refs[3 * nparts:]
    i = pl.program_id(1)

    @pl.when(i == 0)
    def _():
        for p in range(nparts):
            wb_ref[p] = w[p][...].astype(BF16)

    def run(a_refs, r_ref, o_ref):
        acc = r_ref[...]
        for p in range(nparts):
            acc = acc + jnp.dot(a_refs[p][...], wb_ref[p], preferred_element_type=F32)
        o_ref[...] = acc

    @pl.when(i < np_tiles)
    def _():
        run(ap, rp_ref, op_ref)

    @pl.when(i >= np_tiles)
    def _():
        run(a_s, rs_ref, os_ref)


def out_project(parts_p, parts_s, w, res_p, res_s, tm=ROW_TILE, tn=512):
    nparts = len(parts_p)
    mp, kp = parts_p[0].shape
    ms = parts_s[0].shape[0]
    n = w.shape[1]
    np_tiles, ns_tiles = mp // tm, ms // tm
    pmap = lambda j, i: (jnp.minimum(i, np_tiles - 1), 0)
    smap = lambda j, i: (jnp.maximum(i - np_tiles, 0), 0)
    pmap_o = lambda j, i: (jnp.minimum(i, np_tiles - 1), j)
    smap_o = lambda j, i: (jnp.maximum(i - np_tiles, 0), j)
    in_specs = ([pl.BlockSpec((tm, kp), pmap)] * nparts + [pl.BlockSpec((tm, kp), smap)] * nparts
                + [pl.BlockSpec((kp, tn), functools.partial(lambda j, i, p: (p, j), p=p)) for p in range(nparts)]
                + [pl.BlockSpec((tm, tn), pmap_o), pl.BlockSpec((tm, tn), smap_o)])
    return pl.pallas_call(
        functools.partial(_outproj_kernel, np_tiles=np_tiles, nparts=nparts),
        grid=(n // tn, np_tiles + ns_tiles),
        in_specs=in_specs,
        out_specs=[pl.BlockSpec((tm, tn), pmap_o), pl.BlockSpec((tm, tn), smap_o)],
        out_shape=[jax.ShapeDtypeStruct((mp, n), F32), jax.ShapeDtypeStruct((ms, n), F32)],
        scratch_shapes=[pltpu.VMEM((nparts, kp, tn), BF16)],
        compiler_params=_params("arbitrary", "arbitrary"),
        name="out_project",
    )(*parts_p, *parts_s, *([w] * nparts), res_p, res_s)


def _silu_mul(gc, u):
    return gc * (1.0 / (1.0 + jnp.exp(-gc))) * u


def _up_kernel(x_ref, wg_ref, wu_ref, wc_ref, hist_ref, h_ref, cp_ref, cs_ref, wb_ref, carry_ref,
               *, np_tiles, tiles_per_seq, nb_s, tn):
    i = pl.program_id(1)
    tm = x_ref.shape[0]

    @pl.when(i == 0)
    def _():
        wb_ref[:, :tn] = wg_ref[...].astype(BF16)
        wb_ref[:, tn:] = wu_ref[...].astype(BF16)

    w0 = wc_ref[0:1, :]
    w1 = wc_ref[1:2, :]
    w2 = wc_ref[2:3, :]

    @pl.when(lax.rem(i, tiles_per_seq) == 0)
    def _():
        carry_ref[...] = jnp.zeros_like(carry_ref)

    @pl.when(i < np_tiles)
    def _():
        gu = jnp.dot(x_ref[...], wb_ref[...], preferred_element_type=F32)
        g = gu[:, :tn]
        u = gu[:, tn:]
        c0 = carry_ref[0:1, :]
        c1 = carry_ref[1:2, :]
        row = _iota((tm, tn), 0)
        gm1 = jnp.where(row == 0, c1, pltpu.roll(g, 1, 0))
        gm2 = jnp.where(row == 0, c0, jnp.where(row == 1, c1, pltpu.roll(g, 2, 0)))
        gc = gm2 * w0 + gm1 * w1 + g * w2
        h_ref[...] = _silu_mul(gc, u).astype(h_ref.dtype)
        tail = g[tm - 2:tm, :]
        carry_ref[0:2, :] = tail
        cp_ref[0] = tail

    @pl.when(i >= np_tiles)
    def _():
        gu = jnp.dot(x_ref[...], wb_ref[...], preferred_element_type=F32)
        g = gu[:, :tn]
        u = gu[:, tn:]
        h0 = hist_ref[0]
        h1 = hist_ref[1]
        gm1 = jnp.concatenate([h1, g[:tm - nb_s]], axis=0)
        gm2 = jnp.concatenate([h0, h1, g[:tm - 2 * nb_s]], axis=0)
        gc = gm2 * w0 + gm1 * w1 + g * w2
        h_ref[...] = _silu_mul(gc, u).astype(h_ref.dtype)
        cs_ref[0] = g[tm - 2 * nb_s:tm - nb_s]
        cs_ref[1] = g[tm - nb_s:]


def up_convglu(x, w_up, w_conv, layer, hist_t, mp, seq_len, tm=ROW_TILE, tn=256):
    m, k = x.shape
    f = w_up.shape[2] // 2
    nb_s = hist_t.shape[1]
    nb_p = mp // seq_len
    np_tiles = mp // tm
    assert m - mp == tm and CONV_W == 3
    tiles_per_seq = seq_len // tm
    nj = f // tn
    return pl.pallas_call(
        functools.partial(_up_kernel, np_tiles=np_tiles, tiles_per_seq=tiles_per_seq, nb_s=nb_s, tn=tn),
        grid=(nj, np_tiles + 1),
        in_specs=[
            pl.BlockSpec((tm, k), lambda j, i: (i, 0)),
            pl.BlockSpec((None, k, tn), lambda j, i: (layer, 0, j)),
            pl.BlockSpec((None, k, tn), lambda j, i: (layer, 0, nj + j)),
            pl.BlockSpec((None, CONV_W, tn), lambda j, i: (layer, 0, j)),
            pl.BlockSpec((CONV_W - 1, nb_s, tn), lambda j, i: (0, 0, j)),
        ],
        out_specs=[
            pl.BlockSpec((tm, tn), lambda j, i: (i, j)),
            pl.BlockSpec((1, CONV_W - 1, tn), lambda j, i: (jnp.minimum(i // tiles_per_seq, nb_p - 1), 0, j)),
            pl.BlockSpec((CONV_W - 1, nb_s, tn), lambda j, i: (0, 0, j)),
        ],
        out_shape=[
            jax.ShapeDtypeStruct((m, f), BF16),
            jax.ShapeDtypeStruct((nb_p, CONV_W - 1, f), F32),
            jax.ShapeDtypeStruct((CONV_W - 1, nb_s, f), F32),
        ],
        scratch_shapes=[pltpu.VMEM((k, 2 * tn), BF16), pltpu.VMEM((8, tn), F32)],
        compiler_params=_params("arbitrary", "arbitrary"),
        name="up_convglu",
    )(x, w_up, w_up, w_conv, hist_t)


def _down_kernel(a_ref, w_ref, rp_ref, rs_ref, op_ref, os_ref, *, np_tiles):
    i = pl.program_id(1)

    @pl.when(i < np_tiles)
    def _():
        op_ref[...] = rp_ref[...] + jnp.dot(a_ref[...], w_ref[...], preferred_element_type=F32)

    @pl.when(i >= np_tiles)
    def _():
        os_ref[...] = rs_ref[...] + jnp.dot(a_ref[...], w_ref[...], preferred_element_type=F32)


def down_project(a, w, layer, res_p, res_s, tm=DOWN_ROW_TILE, tn=512):
    m, f = a.shape
    n = w.shape[2]
    mp = res_p.shape[0]
    ms = m - mp
    np_tiles, ns_tiles = mp // tm, ms // tm
    pmap_o = lambda j, i: (jnp.minimum(i, np_tiles - 1), j)
    smap_o = lambda j, i: (jnp.maximum(i - np_tiles, 0), j)
    return pl.pallas_call(
        functools.partial(_down_kernel, np_tiles=np_tiles),
        grid=(n // tn, np_tiles + ns_tiles),
        in_specs=[
            pl.BlockSpec((tm, f), lambda j, i: (i, 0)),
            pl.BlockSpec((None, f, tn), lambda j, i: (layer, 0, j)),
            pl.BlockSpec((tm, tn), pmap_o),
            pl.BlockSpec((tm, tn), smap_o),
        ],
        out_specs=[pl.BlockSpec((tm, tn), pmap_o), pl.BlockSpec((tm, tn), smap_o)],
        out_shape=[jax.ShapeDtypeStruct((mp, n), F32), jax.ShapeDtypeStruct((ms, n), F32)],
        compiler_params=_params("arbitrary", "arbitrary"),
        name="down_project",
    )(a, w, res_p, res_s)


def _sb_group(q, k, v, uu, run, acc, vis):
    nk = k.shape[0]
    cb = uu.shape[1]
    z2 = lax.dot_general(q, k, (((1,), (1,)), ((), ())), preferred_element_type=F32)
    s = jnp.maximum(z2, 0.0) + jnp.log2(1.0 + jnp.exp2(-jnp.abs(z2)))
    if vis is not None:
        s = jnp.where(vis, s, 0.0)
    hi = s.astype(BF16)
    lo = (s - hi.astype(F32)).astype(BF16)
    afters = []
    for blk in reversed(range(nk // cb)):
        sl = slice(blk * cb, (blk + 1) * cb)
        cs = jnp.dot(jnp.concatenate([hi[:, sl], lo[:, sl]], axis=1), uu, preferred_element_type=F32)
        afters.append(cs + run)
        run = run + jnp.sum(s[:, sl], axis=-1, keepdims=True)
    after = jnp.concatenate(afters[::-1], axis=1)
    w = jnp.exp2(z2 - s - after)
    if vis is not None:
        w = jnp.where(vis, w, 0.0)
    acc = acc + jnp.dot(w.astype(BF16), v, preferred_element_type=F32)
    return run, acc


def _sb_prompt_kernel(q_ref, k_ref, v_ref, uu_ref, o_ref, kb_ref, vb_ref, *, t):
    qi = pl.program_id(2)

    @pl.when(qi == 0)
    def _():
        kb_ref[...] = k_ref[...].astype(BF16)
        vb_ref[...] = v_ref[...].astype(BF16)

    q = (q_ref[...] * (SCALE * LOG2E)).astype(BF16)
    uu = uu_ref[...]
    r0 = pl.multiple_of(qi * t, t)
    run = jnp.zeros((t, 1), F32)
    acc = jnp.zeros((t, D_HEAD), F32)
    vis = _iota((t, t), 1) < _iota((t, t), 0)
    run, acc = _sb_group(q, kb_ref[pl.ds(r0, t), :], vb_ref[pl.ds(r0, t), :], uu, run, acc, vis)

    def body(g, carry):
        k0 = pl.multiple_of((qi - 1 - g) * t, t)
        return _sb_group(q, kb_ref[pl.ds(k0, t), :], vb_ref[pl.ds(k0, t), :], uu, carry[0], carry[1], None)

    run, acc = lax.fori_loop(0, qi, body, (run, acc))
    o_ref[...] = acc.astype(o_ref.dtype)


def _suffix_matrix(cb):
    j = jnp.arange(cb)[:, None]
    s = jnp.arange(cb)[None, :]
    u = (j > s).astype(BF16)
    return jnp.concatenate([u, u], axis=0)


def sb_attention_prompt(q, k, v, nb, seq, t=SB_T, cb=SB_CB):
    nh = q.shape[1] // D_HEAD
    kv_spec = pl.BlockSpec((seq, D_HEAD), lambda b, h, qi: (b, h))
    q_spec = pl.BlockSpec((t, D_HEAD), lambda b, h, qi: (b * (seq // t) + qi, h))
    return pl.pallas_call(
        functools.partial(_sb_prompt_kernel, t=t),
        grid=(nb, nh, seq // t),
        in_specs=[q_spec, kv_spec, kv_spec, pl.BlockSpec((2 * cb, cb), lambda b, h, qi: (0, 0))],
        out_specs=q_spec,
        out_shape=jax.ShapeDtypeStruct(q.shape, BF16),
        scratch_shapes=[pltpu.VMEM((seq, D_HEAD), BF16), pltpu.VMEM((seq, D_HEAD), BF16)],
        compiler_params=_params("arbitrary", "arbitrary", "arbitrary"),
        name="sb_attention_prompt",
    )(q, k, v, _suffix_matrix(cb))


def _sb_sample_kernel(q_ref, kn_ref, vn_ref, kc_ref, vc_ref, uu_ref, o_ref, kb_ref, vb_ref, *, ts, past, pad):
    kb_ref[pl.ds(0, past), :] = kc_ref[...].astype(BF16)
    vb_ref[pl.ds(0, past), :] = vc_ref[...].astype(BF16)
    kb_ref[pl.ds(past, pad), :] = jnp.zeros((pad, D_HEAD), BF16)
    vb_ref[pl.ds(past, pad), :] = jnp.zeros((pad, D_HEAD), BF16)
    kb_ref[pl.ds(past, ts), :] = kn_ref[...].astype(BF16)
    vb_ref[pl.ds(past, ts), :] = vn_ref[...].astype(BF16)
    nk = past + pad
    vis = _iota((ts, nk), 1) < past + _iota((ts, nk), 0)
    _, acc = _sb_group((q_ref[...] * (SCALE * LOG2E)).astype(BF16), kb_ref[...], vb_ref[...], uu_ref[...],
                       jnp.zeros((ts, 1), F32), jnp.zeros((ts, D_HEAD), F32), vis)
    o_ref[...] = acc.astype(o_ref.dtype)


def sb_attention_sample(q, kn, vn, kc, vc, nb, ts, past, cb=SB_CB):
    nh = q.shape[1] // D_HEAD
    assert past % cb == 0 and ts <= cb
    new_spec = pl.BlockSpec((ts, D_HEAD), lambda b, h: (b, h))
    cache_spec = pl.BlockSpec((past, D_HEAD), lambda b, h: (b, h))
    return pl.pallas_call(
        functools.partial(_sb_sample_kernel, ts=ts, past=past, pad=cb),
        grid=(nb, nh),
        in_specs=[new_spec, new_spec, new_spec, cache_spec, cache_spec,
                  pl.BlockSpec((2 * cb, cb), lambda b, h: (0, 0))],
        out_specs=new_spec,
        out_shape=jax.ShapeDtypeStruct(q.shape, BF16),
        scratch_shapes=[pltpu.VMEM((past + cb, D_HEAD), BF16), pltpu.VMEM((past + cb, D_HEAD), BF16)],
        compiler_params=_params("arbitrary", "arbitrary"),
        name="sb_attention_sample",
    )(q, kn, vn, kc, vc, _suffix_matrix(cb))


def _band_bias_kernel(tab_ref, o_ref, *, rows, cols, width):
    h = pl.program_id(0)
    band = BAND_CHUNKS * CHUNK
    x = _iota((8, width), 1)
    idx = jnp.clip(x - rows - band, -REL_CLIP, REL_CLIP) + REL_CLIP

    def body(r, g):
        return jnp.where(idx == r, tab_ref[h, r] * LOG2E, g)

    g = lax.fori_loop(0, 2 * REL_CLIP + 1, body, jnp.zeros((8, width), F32))
    gt = jnp.broadcast_to(g[0:1, :], (rows, width))
    gt = pltpu.roll(gt, 0, 1, stride=1, stride_axis=0)
    bias = gt[:, rows:rows + cols]
    dc = _iota((rows, cols), 1) // CHUNK - _iota((rows, cols), 0) // CHUNK
    o_ref[0] = jnp.where((dc >= 0) & (dc <= BAND_CHUNKS), bias, NEG)


def band_bias(table, qchunks):
    nh = table.shape[0]
    rows = qchunks * CHUNK
    cols = (qchunks + BAND_CHUNKS) * CHUNK
    width = rows + cols
    assert width % LANES == 0 and rows % LANES == 0
    return pl.pallas_call(
        functools.partial(_band_bias_kernel, rows=rows, cols=cols, width=width),
        grid=(nh,),
        in_specs=[pl.BlockSpec(memory_space=pltpu.SMEM)],
        out_specs=pl.BlockSpec((1, rows, cols), lambda h: (h, 0, 0)),
        out_shape=jax.ShapeDtypeStruct((nh, rows, cols), F32),
        compiler_params=_params("arbitrary"),
        name="band_bias",
    )(table)


def _softmax_rows_base2(s2):
    m = jnp.max(s2, axis=-1, keepdims=True)
    p = jnp.exp2(s2 - m)
    return p / jnp.sum(p, axis=-1, keepdims=True)


def _band_prompt_kernel(q_ref, k_ref, v_ref, bias_ref, gq_ref, gk_ref, o_ref, kt_ref, vt_ref, kb_ref, vb_ref,
                        *, seq, rows, cols, keep):
    band = BAND_CHUNKS * CHUNK
    kn = _rms(k_ref[...], gk_ref[...])
    kt_ref[...] = kn[seq - keep:, :]
    vt_ref[...] = v_ref[pl.ds(seq - keep, keep), :]
    kb_ref[pl.ds(0, band), :] = jnp.zeros((band, D_HEAD), BF16)
    vb_ref[pl.ds(0, band), :] = jnp.zeros((band, D_HEAD), BF16)
    kb_ref[pl.ds(band, seq), :] = kn.astype(BF16)
    vb_ref[pl.ds(band, seq), :] = v_ref[...].astype(BF16)
    bias = bias_ref[0]
    gq = gq_ref[...]
    col = _iota((rows, cols), 1)

    def body(t, carry):
        r0 = pl.multiple_of(t * rows, rows)
        q = (_rms(q_ref[pl.ds(r0, rows), :], gq) * (SCALE * LOG2E)).astype(BF16)
        k = kb_ref[pl.ds(r0, cols), :]
        v = vb_ref[pl.ds(r0, cols), :]
        s = lax.dot_general(q, k, (((1,), (1,)), ((), ())), preferred_element_type=F32) + bias
        s = jnp.where(col >= band - r0, s, NEG)
        p = _softmax_rows_base2(s)
        o_ref[pl.ds(r0, rows), :] = jnp.dot(p.astype(BF16), v, preferred_element_type=F32).astype(o_ref.dtype)
        return carry

    lax.fori_loop(0, seq // rows, body, 0)


def band_attention_prompt(q, k, v, bias, g_q, g_k, nb, seq, keep):
    nh = q.shape[1] // D_HEAD
    rows, cols = bias.shape[1], bias.shape[2]
    band = BAND_CHUNKS * CHUNK
    blk = pl.BlockSpec((seq, D_HEAD), lambda b, h: (b, h))
    tail = pl.BlockSpec((keep, D_HEAD), lambda b, h: (b, h))
    gspec = pl.BlockSpec((1, D_HEAD), lambda b, h: (0, 0))
    return pl.pallas_call(
        functools.partial(_band_prompt_kernel, seq=seq, rows=rows, cols=cols, keep=keep),
        grid=(nb, nh),
        in_specs=[blk, blk, blk, pl.BlockSpec((1, rows, cols), lambda b, h: (h, 0, 0)), gspec, gspec],
        out_specs=[blk, tail, tail],
        out_shape=[jax.ShapeDtypeStruct(q.shape, BF16),
                   jax.ShapeDtypeStruct((nb * keep, q.shape[1]), F32),
                   jax.ShapeDtypeStruct((nb * keep, q.shape[1]), F32)],
        scratch_shapes=[pltpu.VMEM((band + seq, D_HEAD), BF16), pltpu.VMEM((band + seq, D_HEAD), BF16)],
        compiler_params=_params("arbitrary", "arbitrary"),
        name="band_attention_prompt",
    )(q, k, v, bias, g_q.reshape(1, D_HEAD), g_k.reshape(1, D_HEAD))


def _band_sample_kernel(q_ref, kn_ref, vn_ref, kc_ref, vc_ref, bias_ref, gq_ref, gk_ref, o_ref, ko_ref,
                        kb_ref, vb_ref, *, ts, w, cols):
    kn = _rms(kn_ref[...], gk_ref[...])
    ko_ref[...] = kn
    kb_ref[pl.ds(0, w), :] = kc_ref[...].astype(BF16)
    vb_ref[pl.ds(0, w), :] = vc_ref[...].astype(BF16)
    kb_ref[pl.ds(w, cols - w), :] = jnp.zeros((cols - w, D_HEAD), BF16)
    vb_ref[pl.ds(w, cols - w), :] = jnp.zeros((cols - w, D_HEAD), BF16)
    kb_ref[pl.ds(w, ts), :] = kn.astype(BF16)
    vb_ref[pl.ds(w, ts), :] = vn_ref[...].astype(BF16)
    q = (_rms(q_ref[...], gq_ref[...]) * (SCALE * LOG2E)).astype(BF16)
    s = lax.dot_general(q, kb_ref[...], (((1,), (1,)), ((), ())), preferred_element_type=F32) + bias_ref[0]
    s = jnp.where(_iota((ts, cols), 1) < w + ts, s, NEG)
    p = _softmax_rows_base2(s)
    o_ref[...] = jnp.dot(p.astype(BF16), vb_ref[...], preferred_element_type=F32).astype(o_ref.dtype)


def band_attention_sample(q, kn, vn, kc, vc, bias, g_q, g_k, nb, ts, w):
    nh = q.shape[1] // D_HEAD
    cols = -(-(w + ts) // LANES) * LANES
    assert w == BAND_CHUNKS * CHUNK and ts <= CHUNK and cols <= bias.shape[2] and ts % 8 == 0
    new_spec = pl.BlockSpec((ts, D_HEAD), lambda b, h: (b, h))
    cache_spec = pl.BlockSpec((w, D_HEAD), lambda b, h: (b, h))
    gspec = pl.BlockSpec((1, D_HEAD), lambda b, h: (0, 0))
    return pl.pallas_call(
        functools.partial(_band_sample_kernel, ts=ts, w=w, cols=cols),
        grid=(nb, nh),
        in_specs=[new_spec, new_spec, new_spec, cache_spec, cache_spec,
                  pl.BlockSpec((1, ts, cols), lambda b, h: (h, 0, 0)), gspec, gspec],
        out_specs=[new_spec, new_spec],
        out_shape=[jax.ShapeDtypeStruct(q.shape, BF16), jax.ShapeDtypeStruct(q.shape, F32)],
        scratch_shapes=[pltpu.VMEM((cols, D_HEAD), BF16), pltpu.VMEM((cols, D_HEAD), BF16)],
        compiler_params=_params("arbitrary", "arbitrary"),
        name="band_attention_sample",
    )(q, kn, vn, kc, vc, bias, g_q.reshape(1, D_HEAD), g_k.reshape(1, D_HEAD))


def _norm_rope(x, g, cosf, sina, sinb):
    half = ROT_DIM // 2
    y = _rms(x, g)
    return y * cosf + pltpu.roll(y, D_HEAD - half, 1) * sina + pltpu.roll(y, half, 1) * sinb


def rope_tables(pos):
    half = ROT_DIM // 2
    inv_freq = ROPE_THETA ** (-jnp.arange(half, dtype=F32) * (2.0 / ROT_DIM))
    ang = pos.astype(F32)[:, None] * inv_freq[None, :]
    cos, sin = jnp.cos(ang), jnp.sin(ang)
    n = pos.shape[0]
    rest = D_HEAD - ROT_DIM
    cosf = jnp.concatenate([cos, cos, jnp.ones((n, rest), F32)], axis=1)
    sina = jnp.concatenate([-sin, jnp.zeros((n, rest + half), F32)], axis=1)
    sinb = jnp.concatenate([jnp.zeros((n, half), F32), sin, jnp.zeros((n, rest), F32)], axis=1)
    return cosf, sina, sinb


def _online_block(q, k, v, mask, m, l, acc):
    s = lax.dot_general(q, k, (((1,), (1,)), ((), ())), preferred_element_type=F32)
    if mask is not None:
        s = jnp.where(mask, s, NEG)
    m_new = jnp.maximum(m, jnp.max(s, axis=-1, keepdims=True))
    alpha = jnp.exp2(m - m_new)
    p = jnp.exp2(s - m_new)
    l = alpha * l + jnp.sum(p, axis=-1, keepdims=True)
    acc = alpha * acc + jnp.dot(p.astype(BF16), v, preferred_element_type=F32)
    return m_new, l, acc


def _diff_lambda(lq1, lk1, lq2, lk2):
    return (jnp.exp(jnp.sum(lq1 * lk1, axis=-1, keepdims=True))
            - jnp.exp(jnp.sum(lq2 * lk2, axis=-1, keepdims=True)) + LAMBDA_INIT)


def _diff_finish(state, lam, gsub):
    (_, l1, a1), (_, l2, a2) = state
    o = a1 / l1 - lam * (a2 / l2)
    return _rms(o, gsub) * (1.0 - LAMBDA_INIT)


def _diff_init(tq):
    return (jnp.full((tq, 1), NEG, F32), jnp.zeros((tq, 1), F32), jnp.zeros((tq, 2 * D_HEAD), F32))


def _diff_prompt_kernel(q_ref, k_ref, v_ref, cos_ref, sa_ref, sb_ref, gq_ref, gk_ref, lam_ref, gsub_ref,
                        o_ref, ko_ref, k1_ref, k2_ref, vb_ref, *, t):
    qi = pl.program_id(2)
    d = D_HEAD

    @pl.when(qi == 0)
    def _():
        gk = gk_ref[...]
        cosf, sina, sinb = cos_ref[...], sa_ref[...], sb_ref[...]
        k1 = _norm_rope(k_ref[:, :d], gk, cosf, sina, sinb)
        k2 = _norm_rope(k_ref[:, d:], gk, cosf, sina, sinb)
        ko_ref[:, :d] = k1
        ko_ref[:, d:] = k2
        k1_ref[...] = k1.astype(BF16)
        k2_ref[...] = k2.astype(BF16)
        vb_ref[...] = v_ref[...].astype(BF16)

    r0 = pl.multiple_of(qi * t, t)
    gq = gq_ref[...]
    cosf, sina, sinb = cos_ref[pl.ds(r0, t), :], sa_ref[pl.ds(r0, t), :], sb_ref[pl.ds(r0, t), :]
    q1 = (_norm_rope(q_ref[:, :d], gq, cosf, sina, sinb) * (SCALE * LOG2E)).astype(BF16)
    q2 = (_norm_rope(q_ref[:, d:], gq, cosf, sina, sinb) * (SCALE * LOG2E)).astype(BF16)

    def step(k0, state, mask):
        v = vb_ref[pl.ds(k0, t), :]
        s1 = _online_block(q1, k1_ref[pl.ds(k0, t), :], v, mask, *state[0])
        s2 = _online_block(q2, k2_ref[pl.ds(k0, t), :], v, mask, *state[1])
        return s1, s2

    def body(kb, state):
        return step(pl.multiple_of(kb * t, t), state, None)

    state = lax.fori_loop(0, qi, body, (_diff_init(t), _diff_init(t)))
    mask = (_iota((t, t), 1) // CHUNK) <= (_iota((t, t), 0) // CHUNK)
    state = step(r0, state, mask)
    lam = _diff_lambda(lam_ref[0:1, :], lam_ref[1:2, :], lam_ref[2:3, :], lam_ref[3:4, :])
    o_ref[...] = _diff_finish(state, lam, gsub_ref[...]).astype(o_ref.dtype)


def diff_attention_prompt(q, k, v, tabs, g_q, g_k, lam4, g_sub, nb, seq, t=DIFF_T):
    d2 = 2 * D_HEAD
    nh = q.shape[1] // d2
    assert t % CHUNK == 0
    kv_spec = pl.BlockSpec((seq, d2), lambda b, h, qi: (b, h))
    q_spec = pl.BlockSpec((t, d2), lambda b, h, qi: (b * (seq // t) + qi, h))
    tab_spec = pl.BlockSpec((seq, D_HEAD), lambda b, h, qi: (0, 0))
    g_spec = pl.BlockSpec((1, D_HEAD), lambda b, h, qi: (0, 0))
    return pl.pallas_call(
        functools.partial(_diff_prompt_kernel, t=t),
        grid=(nb, nh, seq // t),
        in_specs=[q_spec, kv_spec, kv_spec, tab_spec, tab_spec, tab_spec, g_spec, g_spec,
                  pl.BlockSpec((4, D_HEAD), lambda b, h, qi: (0, 0)),
                  pl.BlockSpec((1, d2), lambda b, h, qi: (0, 0))],
        out_specs=[q_spec, kv_spec],
        out_shape=[jax.ShapeDtypeStruct(q.shape, BF16), jax.ShapeDtypeStruct(k.shape, F32)],
        scratch_shapes=[pltpu.VMEM((seq, D_HEAD), BF16), pltpu.VMEM((seq, D_HEAD), BF16),
                        pltpu.VMEM((seq, d2), BF16)],
        compiler_params=_params("arbitrary", "arbitrary", "arbitrary"),
        name="diff_attention_prompt",
    )(q, k, v, *tabs, g_q.reshape(1, D_HEAD), g_k.reshape(1, D_HEAD), lam4, g_sub.reshape(1, d2))


def _diff_sample_kernel(q_ref, kn_ref, vn_ref, kc_ref, vc_ref, cos_ref, sa_ref, sb_ref, gq_ref, gk_ref,
                        lam_ref, gsub_ref, o_ref, ko_ref, k1_ref, k2_ref, vb_ref, *, ts, past, tn):
    d = D_HEAD
    cosf, sina, sinb = cos_ref[...], sa_ref[...], sb_ref[...]
    gk = gk_ref[...]
    k1 = _norm_rope(kn_ref[:, :d], gk, cosf, sina, sinb)
    k2 = _norm_rope(kn_ref[:, d:], gk, cosf, sina, sinb)
    ko_ref[:, :d] = k1
    ko_ref[:, d:] = k2
    k1_ref[pl.ds(0, past), :] = kc_ref[:, :d].astype(BF16)
    k2_ref[pl.ds(0, past), :] = kc_ref[:, d:].astype(BF16)
    vb_ref[pl.ds(0, past), :] = vc_ref[...].astype(BF16)
    k1_ref[pl.ds(past, tn), :] = jnp.zeros((tn, d), BF16)
    k2_ref[pl.ds(past, tn), :] = jnp.zeros((tn, d), BF16)
    vb_ref[pl.ds(past, tn), :] = jnp.zeros((tn, 2 * d), BF16)
    k1_ref[pl.ds(past, ts), :] = k1.astype(BF16)
    k2_ref[pl.ds(past, ts), :] = k2.astype(BF16)
    vb_ref[pl.ds(past, ts), :] = vn_ref[...].astype(BF16)
    gq = gq_ref[...]
    q1 = (_norm_rope(q_ref[:, :d], gq, cosf, sina, sinb) * (SCALE * LOG2E)).astype(BF16)
    q2 = (_norm_rope(q_ref[:, d:], gq, cosf, sina, sinb) * (SCALE * LOG2E)).astype(BF16)

    nk = past + tn
    kpos = _iota((ts, nk), 1)
    qpos = past + _iota((ts, nk), 0)
    mask = ((kpos // CHUNK) <= (qpos // CHUNK)) & (kpos < past + ts)
    v = vb_ref[...]
    state = (_online_block(q1, k1_ref[...], v, mask, *_diff_init(ts)),
             _online_block(q2, k2_ref[...], v, mask, *_diff_init(ts)))
    lam = _diff_lambda(lam_ref[0:1, :], lam_ref[1:2, :], lam_ref[2:3, :], lam_ref[3:4, :])
    o_ref[...] = _diff_finish(state, lam, gsub_ref[...]).astype(o_ref.dtype)


def diff_attention_sample(q, kn, vn, kc, vc, tabs, g_q, g_k, lam4, g_sub, nb, ts, past, tn=LANES):
    d2 = 2 * D_HEAD
    nh = q.shape[1] // d2
    assert past % LANES == 0 and ts <= tn
    new_spec = pl.BlockSpec((ts, d2), lambda b, h: (b, h))
    cache_spec = pl.BlockSpec((past, d2), lambda b, h: (b, h))
    tab_spec = pl.BlockSpec((ts, D_HEAD), lambda b, h: (0, 0))
    g_spec = pl.BlockSpec((1, D_HEAD), lambda b, h: (0, 0))
    return pl.pallas_call(
        functools.partial(_diff_sample_kernel, ts=ts, past=past, tn=tn),
        grid=(nb, nh),
        in_specs=[new_spec, new_spec, new_spec, cache_spec, cache_spec, tab_spec, tab_spec, tab_spec,
                  g_spec, g_spec, pl.BlockSpec((4, D_HEAD), lambda b, h: (0, 0)),
                  pl.BlockSpec((1, d2), lambda b, h: (0, 0))],
        out_specs=[new_spec, new_spec],
        out_shape=[jax.ShapeDtypeStruct(q.shape, BF16), jax.ShapeDtypeStruct(q.shape, F32)],
        scratch_shapes=[pltpu.VMEM((past + tn, D_HEAD), BF16), pltpu.VMEM((past + tn, D_HEAD), BF16),
                        pltpu.VMEM((past + tn, d2), BF16)],
        compiler_params=_params("arbitrary", "arbitrary"),
        name="diff_attention_sample",
    )(q, kn, vn, kc, vc, *tabs, g_q.reshape(1, D_HEAD), g_k.reshape(1, D_HEAD), lam4, g_sub.reshape(1, d2))


def _to_time_major(x, nb, ts):
    return x.reshape(nb, ts, -1).transpose(1, 0, 2).reshape(nb * ts, -1)


def _to_batch_major(x, nb, ts):
    return x.reshape(ts, nb, -1).transpose(1, 0, 2).reshape(nb * ts, -1)


def _conv_glu_layer(hp, hs, g_norm, w_up, w_conv, w_down_bf16, layer, state, nb_s, ts, seq):
    mp = hp.shape[0]
    hs_t = _to_time_major(hs, nb_s, ts)
    xn = rmsnorm_rows(hp, hs_t, g_norm)
    hidden, conv_p, conv_s_t = up_convglu(xn, w_up, w_conv, layer, state.transpose(1, 0, 2), mp, seq)
    hp, hs_t = down_project(hidden, w_down_bf16, layer, hp, hs_t)
    return hp, _to_batch_major(hs_t, nb_s, ts), conv_p, conv_s_t.transpose(1, 0, 2)


def kernel(x_prompt, x_sample, cache_k_sb, cache_v_sb, cache_k_band, cache_v_band, cache_k_diff, cache_v_diff, state_conv_ffn, norm_mix, norm_ffn, w_in_ab, w_out_ab, g_q_band, g_k_band, rel_bias_band, w_in_diff, w_out_diff, g_q_diff, g_k_diff, lambda_q1, lambda_k1, lambda_q2, lambda_k2, g_sub_diff, w_up, w_conv, w_down):
    bp, tp, dm = x_prompt.shape
    bs, ts, _ = x_sample.shape
    past = cache_k_sb.shape[1]
    h_sb, h_cb, h_df = cache_k_sb.shape[2], cache_k_band.shape[2], cache_k_diff.shape[2]
    d_sb, d_cb, d_df = h_sb * D_HEAD, h_cb * D_HEAD, h_df * 2 * D_HEAD
    w_band = cache_k_band.shape[1]
    keep = min(BAND_CHUNKS * CHUNK, tp)
    mp, ms = bp * tp, bs * ts

    hp = x_prompt.reshape(mp, dm)
    hs = x_sample.reshape(ms, dm)

    xn = rmsnorm_rows(hp, hs, norm_mix[0])
    sec = []
    col = 0
    for width in (d_sb, d_sb, d_sb, d_cb, d_cb, d_cb):
        sec.append(project(xn, w_in_ab, col, width, mp))
        col += width
    (qa_p, qa_s), (ka_p, ka_s), (va_p, va_s), (qb_p, qb_s), (kb_p, kb_s), (vb_p, vb_s) = sec

    oa_p = sb_attention_prompt(qa_p, ka_p, va_p, bp, tp)
    oa_s = sb_attention_sample(qa_s, ka_s, va_s, cache_k_sb.reshape(bs * past, d_sb),
                               cache_v_sb.reshape(bs * past, d_sb), bs, ts, past)
    bias = band_bias(rel_bias_band, BAND_QCHUNKS)
    ob_p, k_band_p, v_band_p = band_attention_prompt(qb_p, kb_p, vb_p, bias, g_q_band, g_k_band, bp, tp, keep)
    ob_s, k_band_s = band_attention_sample(qb_s, kb_s, vb_s, cache_k_band.reshape(bs * w_band, d_cb),
                                           cache_v_band.reshape(bs * w_band, d_cb), bias, g_q_band, g_k_band,
                                           bs, ts, w_band)
    hp, hs = out_project([oa_p, ob_p], [oa_s, ob_s], w_out_ab, hp, hs)
    w_down_bf16 = w_down.astype(BF16)
    hp, hs, conv_p0, conv_s0 = _conv_glu_layer(hp, hs, norm_ffn[0], w_up, w_conv, w_down_bf16, 0,
                                               state_conv_ffn[0], bs, ts, tp)

    xn = rmsnorm_rows(hp, hs, norm_mix[1])
    (q_p, q_s), (k_p, k_s), (v_p, v_s) = [project(xn, w_in_diff, c * d_df, d_df, mp) for c in range(3)]
    lam4 = jnp.stack([lambda_q1, lambda_k1, lambda_q2, lambda_k2]).astype(F32)
    o_p, k_diff_p = diff_attention_prompt(q_p, k_p, v_p, rope_tables(jnp.arange(tp)), g_q_diff, g_k_diff,
                                          lam4, g_sub_diff, bp, tp)
    o_s, k_diff_s = diff_attention_sample(q_s, k_s, v_s, cache_k_diff.reshape(bs * past, d_df),
                                          cache_v_diff.reshape(bs * past, d_df),
                                          rope_tables(past + jnp.arange(ts)), g_q_diff, g_k_diff,
                                          lam4, g_sub_diff, bs, ts, past)
    hp, hs = out_project([o_p], [o_s], w_out_diff, hp, hs)
    hp, hs, conv_p1, conv_s1 = _conv_glu_layer(hp, hs, norm_ffn[1], w_up, w_conv, w_down_bf16, 1,
                                               state_conv_ffn[1], bs, ts, tp)

    return (hp.reshape(bp, tp, dm), hs.reshape(bs, ts, dm),
            ka_p.reshape(bp, tp, h_sb, D_HEAD), va_p.reshape(bp, tp, h_sb, D_HEAD),
            k_band_p.reshape(bp, keep, h_cb, D_HEAD), v_band_p.reshape(bp, keep, h_cb, D_HEAD),
            k_diff_p.reshape(bp, tp, h_df, 2 * D_HEAD), v_p.reshape(bp, tp, h_df, 2 * D_HEAD),
            jnp.stack([conv_p0, conv_p1]),
            ka_s.reshape(bs, ts, h_sb, D_HEAD), va_s.reshape(bs, ts, h_sb, D_HEAD),
            k_band_s.reshape(bs, ts, h_cb, D_HEAD), vb_s.reshape(bs, ts, h_cb, D_HEAD),
            k_diff_s.reshape(bs, ts, h_df, 2 * D_HEAD), v_s.reshape(bs, ts, h_df, 2 * D_HEAD),
            jnp.stack([conv_s0, conv_s1]))
```

```python
import functools
import math

import jax
import jax.numpy as jnp
from jax import lax
from jax.experimental import pallas as pl
from jax.experimental.pallas import tpu as pltpu

F32 = jnp.float32
BF16 = jnp.bfloat16

CHUNK = 64
D_HEAD = 128
BAND_CHUNKS = 8
REL_CLIP = 128
ROT_DIM = D_HEAD // 4
ROPE_THETA = 500000.0
CONV_W = 3
EPS = 1e-6
NEG = -1e30
LAMBDA_INIT = 0.8 - 0.6 * math.exp(-0.3 * 1)
SCALE = 1.0 / math.sqrt(D_HEAD)
LOG2E = math.log2(math.e)

V7X_VMEM_LIMIT_BYTES = 56 * 1024 * 1024
LANES = 128

ROW_TILE = 1024
OUT_ROW_TILE = 512
DOWN_ROW_TILE = 256
SB_T = 512
SB_CB = 256
BAND_QCHUNKS = 4
DIFF_T = 512


def _params(*sem):
    return pltpu.CompilerParams(dimension_semantics=sem, vmem_limit_bytes=V7X_VMEM_LIMIT_BYTES)


def _iota(shape, dim):
    return lax.broadcasted_iota(jnp.int32, shape, dim)


def _rms(x, g):
    return x * lax.rsqrt(jnp.mean(x * x, axis=-1, keepdims=True) + EPS) * g


def _rmsnorm_kernel(xp_ref, xs_ref, g_ref, o_ref, *, np_tiles):
    i = pl.program_id(0)

    @pl.when(i < np_tiles)
    def _():
        o_ref[...] = _rms(xp_ref[...], g_ref[...]).astype(o_ref.dtype)

    @pl.when(i >= np_tiles)
    def _():
        o_ref[...] = _rms(xs_ref[...], g_ref[...]).astype(o_ref.dtype)


def rmsnorm_rows(xp, xs, g, tm=256):
    mp, d = xp.shape
    ms = xs.shape[0]
    np_tiles, ns_tiles = mp // tm, ms // tm
    return pl.pallas_call(
        functools.partial(_rmsnorm_kernel, np_tiles=np_tiles),
        grid=(np_tiles + ns_tiles,),
        in_specs=[
            pl.BlockSpec((tm, d), lambda i: (jnp.minimum(i, np_tiles - 1), 0)),
            pl.BlockSpec((tm, d), lambda i: (jnp.maximum(i - np_tiles, 0), 0)),
            pl.BlockSpec((1, d), lambda i: (0, 0)),
        ],
        out_specs=pl.BlockSpec((tm, d), lambda i: (i, 0)),
        out_shape=jax.ShapeDtypeStruct((mp + ms, d), BF16),
        compiler_params=_params("arbitrary"),
        name="rmsnorm_rows",
    )(xp, xs, g.reshape(1, d))


def _proj_kernel(x_ref, w_ref, op_ref, os_ref, wb_ref, *, np_tiles, ms):
    i = pl.program_id(1)

    @pl.when(i == 0)
    def _():
        wb_ref[...] = w_ref[...].astype(BF16)

    @pl.when(i < np_tiles)
    def _():
        op_ref[...] = jnp.dot(x_ref[...], wb_ref[...], preferred_element_type=F32)

    @pl.when(i >= np_tiles)
    def _():
        os_ref[...] = jnp.dot(x_ref[0:ms, :], wb_ref[...], preferred_element_type=F32)


def project(x, w, col0, n, mp, tm=ROW_TILE, tn=512):
    m, k = x.shape
    ms = m - mp
    assert mp % tm == 0 and 0 < ms <= tm
    np_tiles = mp // tm
    jb = col0 // tn
    return pl.pallas_call(
        functools.partial(_proj_kernel, np_tiles=np_tiles, ms=ms),
        grid=(n // tn, np_tiles + 1),
        in_specs=[
            pl.BlockSpec((tm, k), lambda j, i: (i, 0)),
            pl.BlockSpec((k, tn), lambda j, i: (0, jb + j)),
        ],
        out_specs=[
            pl.BlockSpec((tm, tn), lambda j, i: (jnp.minimum(i, np_tiles - 1), j)),
            pl.BlockSpec((ms, tn), lambda j, i: (0, j)),
        ],
        out_shape=[jax.ShapeDtypeStruct((mp, n), F32), jax.ShapeDtypeStruct((ms, n), F32)],
        scratch_shapes=[pltpu.VMEM((k, tn), BF16)],
        compiler_params=_params("arbitrary", "arbitrary"),
        name="project",
    )(x, w)


def _outproj_kernel(*refs, np_tiles, nparts):
    ap = refs[0:nparts]
    a_s = refs[nparts:2 * nparts]
    w = refs[2 * nparts:3 * nparts]
    rp_ref, rs_ref, op_ref, os_ref, wb_ref = refs[3 * nparts:]
    i = pl.program_id(1)

    @pl.when(i == 0)
    def _():
        for p in range(nparts):
            wb_ref[p] = w[p][...].astype(BF16)

    def run(a_refs, r_ref, o_ref):
        acc = r_ref[...]
        for p in range(nparts):
            acc = acc + jnp.dot(a_refs[p][...], wb_ref[p], preferred_element_type=F32)
        o_ref[...] = acc

    @pl.when(i < np_tiles)
    def _():
        run(ap, rp_ref, op_ref)

    @pl.when(i >= np_tiles)
    def _():
        run(a_s, rs_ref, os_ref)


def out_project(parts_p, parts_s, w, res_p, res_s, tm=OUT_ROW_TILE, tn=512):
    nparts = len(parts_p)
    mp, kp = parts_p[0].shape
    ms = parts_s[0].shape[0]
    n = w.shape[1]
    np_tiles, ns_tiles = mp // tm, ms // tm
    pmap = lambda j, i: (jnp.minimum(i, np_tiles - 1), 0)
    smap = lambda j, i: (jnp.maximum(i - np_tiles, 0), 0)
    pmap_o = lambda j, i: (jnp.minimum(i, np_tiles - 1), j)
    smap_o = lambda j, i: (jnp.maximum(i - np_tiles, 0), j)
    in_specs = ([pl.BlockSpec((tm, kp), pmap)] * nparts + [pl.BlockSpec((tm, kp), smap)] * nparts
                + [pl.BlockSpec((kp, tn), functools.partial(lambda j, i, p: (p, j), p=p)) for p in range(nparts)]
                + [pl.BlockSpec((tm, tn), pmap_o), pl.BlockSpec((tm, tn), smap_o)])
    return pl.pallas_call(
        functools.partial(_outproj_kernel, np_tiles=np_tiles, nparts=nparts),
        grid=(n // tn, np_tiles + ns_tiles),
        in_specs=in_specs,
        out_specs=[pl.BlockSpec((tm, tn), pmap_o), pl.BlockSpec((tm, tn), smap_o)],
        out_shape=[jax.ShapeDtypeStruct((mp, n), F32), jax.ShapeDtypeStruct((ms, n), F32)],
        scratch_shapes=[pltpu.VMEM((nparts, kp, tn), BF16)],
        compiler_params=_params("arbitrary", "arbitrary"),
        name="out_project",
    )(*parts_p, *parts_s, *([w] * nparts), res_p, res_s)


def _silu_mul(gc, u):
    return gc * (1.0 / (1.0 + jnp.exp(-gc))) * u


def _up_kernel(x_ref, wg_ref, wu_ref, wc_ref, hist_ref, h_ref, cp_ref, cs_ref, wb_ref, carry_ref,
               *, np_tiles, tiles_per_seq, nb_s, ms, tn):
    i = pl.program_id(1)
    tm = x_ref.shape[0]

    @pl.when(i == 0)
    def _():
        wb_ref[:, :tn] = wg_ref[...].astype(BF16)
        wb_ref[:, tn:] = wu_ref[...].astype(BF16)

    w0 = wc_ref[0:1, :]
    w1 = wc_ref[1:2, :]
    w2 = wc_ref[2:3, :]

    @pl.when(lax.rem(i, tiles_per_seq) == 0)
    def _():
        carry_ref[...] = jnp.zeros_like(carry_ref)

    @pl.when(i < np_tiles)
    def _():
        gu = jnp.dot(x_ref[...], wb_ref[...], preferred_element_type=F32)
        g = gu[:, :tn]
        u = gu[:, tn:]
        c0 = carry_ref[0:1, :]
        c1 = carry_ref[1:2, :]
        row = _iota((tm, tn), 0)
        gm1 = jnp.where(row == 0, c1, pltpu.roll(g, 1, 0))
        gm2 = jnp.where(row == 0, c0, jnp.where(row == 1, c1, pltpu.roll(g, 2, 0)))
        gc = gm2 * w0 + gm1 * w1 + g * w2
        h_ref[...] = _silu_mul(gc, u).astype(h_ref.dtype)
        tail = g[tm - 2:tm, :]
        carry_ref[0:2, :] = tail
        cp_ref[0] = tail

    @pl.when(i >= np_tiles)
    def _():
        gu = jnp.dot(x_ref[0:ms, :], wb_ref[...], preferred_element_type=F32)
        g = gu[:, :tn]
        u = gu[:, tn:]
        h0 = hist_ref[0]
        h1 = hist_ref[1]
        gm1 = jnp.concatenate([h1, g[:ms - nb_s]], axis=0)
        gm2 = jnp.concatenate([h0, h1, g[:ms - 2 * nb_s]], axis=0)
        gc = gm2 * w0 + gm1 * w1 + g * w2
        h_ref[0:ms, :] = _silu_mul(gc, u).astype(h_ref.dtype)
        cs_ref[0] = g[ms - 2 * nb_s:ms - nb_s]
        cs_ref[1] = g[ms - nb_s:]


def up_convglu(x, w_up, w_conv, layer, hist_t, mp, seq_len, tm=ROW_TILE, tn=256):
    m, k = x.shape
    f = w_up.shape[2] // 2
    nb_s = hist_t.shape[1]
    nb_p = mp // seq_len
    np_tiles = mp // tm
    ms = m - mp
    assert 0 < ms <= tm and seq_len % tm == 0 and CONV_W == 3
    tiles_per_seq = seq_len // tm
    nj = f // tn
    return pl.pallas_call(
        functools.partial(_up_kernel, np_tiles=np_tiles, tiles_per_seq=tiles_per_seq, nb_s=nb_s, ms=ms, tn=tn),
        grid=(nj, np_tiles + 1),
        in_specs=[
            pl.BlockSpec((tm, k), lambda j, i: (i, 0)),
            pl.BlockSpec((None, k, tn), lambda j, i: (layer, 0, j)),
            pl.BlockSpec((None, k, tn), lambda j, i: (layer, 0, nj + j)),
            pl.BlockSpec((None, CONV_W, tn), lambda j, i: (layer, 0, j)),
            pl.BlockSpec((CONV_W - 1, nb_s, tn), lambda j, i: (0, 0, j)),
        ],
        out_specs=[
            pl.BlockSpec((tm, tn), lambda j, i: (i, j)),
            pl.BlockSpec((1, CONV_W - 1, tn), lambda j, i: (jnp.minimum(i // tiles_per_seq, nb_p - 1), 0, j)),
            pl.BlockSpec((CONV_W - 1, nb_s, tn), lambda j, i: (0, 0, j)),
        ],
        out_shape=[
            jax.ShapeDtypeStruct((m, f), BF16),
            jax.ShapeDtypeStruct((nb_p, CONV_W - 1, f), F32),
            jax.ShapeDtypeStruct((CONV_W - 1, nb_s, f), F32),
        ],
        scratch_shapes=[pltpu.VMEM((k, 2 * tn), BF16), pltpu.VMEM((8, tn), F32)],
        compiler_params=_params("arbitrary", "arbitrary"),
        name="up_convglu",
    )(x, w_up, w_up, w_conv, hist_t)


def _down_kernel(a_ref, w_ref, rp_ref, rs_ref, op_ref, os_ref, *, np_tiles):
    i = pl.program_id(1)

    @pl.when(i < np_tiles)
    def _():
        op_ref[...] = rp_ref[...] + jnp.dot(a_ref[...], w_ref[...], preferred_element_type=F32)

    @pl.when(i >= np_tiles)
    def _():
        os_ref[...] = rs_ref[...] + jnp.dot(a_ref[...], w_ref[...], preferred_element_type=F32)


def down_project(a, w, layer, res_p, res_s, tm=DOWN_ROW_TILE, tn=512):
    m, f = a.shape
    n = w.shape[2]
    mp = res_p.shape[0]
    ms = m - mp
    np_tiles, ns_tiles = mp // tm, ms // tm
    pmap_o = lambda j, i: (jnp.minimum(i, np_tiles - 1), j)
    smap_o = lambda j, i: (jnp.maximum(i - np_tiles, 0), j)
    return pl.pallas_call(
        functools.partial(_down_kernel, np_tiles=np_tiles),
        grid=(n // tn, np_tiles + ns_tiles),
        in_specs=[
            pl.BlockSpec((tm, f), lambda j, i: (i, 0)),
            pl.BlockSpec((None, f, tn), lambda j, i: (layer, 0, j)),
            pl.BlockSpec((tm, tn), pmap_o),
            pl.BlockSpec((tm, tn), smap_o),
        ],
        out_specs=[pl.BlockSpec((tm, tn), pmap_o), pl.BlockSpec((tm, tn), smap_o)],
        out_shape=[jax.ShapeDtypeStruct((mp, n), F32), jax.ShapeDtypeStruct((ms, n), F32)],
        compiler_params=_params("arbitrary", "arbitrary"),
        name="down_project",
    )(a, w, res_p, res_s)


def _sb_group(q, k, v, uu, run, acc, vis):
    nk = k.shape[0]
    cb = uu.shape[1]
    z2 = lax.dot_general(q, k, (((1,), (1,)), ((), ())), preferred_element_type=F32)
    s = jnp.maximum(z2, 0.0) + jnp.log2(1.0 + jnp.exp2(-jnp.abs(z2)))
    if vis is not None:
        s = jnp.where(vis, s, 0.0)
    hi = s.astype(BF16)
    lo = (s - hi.astype(F32)).astype(BF16)
    afters = []
    for blk in reversed(range(nk // cb)):
        sl = slice(blk * cb, (blk + 1) * cb)
        cs = jnp.dot(jnp.concatenate([hi[:, sl], lo[:, sl]], axis=1), uu, preferred_element_type=F32)
        afters.append(cs + run)
        run = run + jnp.sum(s[:, sl], axis=-1, keepdims=True)
    after = jnp.concatenate(afters[::-1], axis=1)
    w = jnp.exp2(z2 - s - after)
    if vis is not None:
        w = jnp.where(vis, w, 0.0)
    acc = acc + jnp.dot(w.astype(BF16), v, preferred_element_type=F32)
    return run, acc


def _sb_prompt_kernel(q_ref, k_ref, v_ref, uu_ref, o_ref, kb_ref, vb_ref, *, t):
    qi = pl.program_id(2)

    @pl.when(qi == 0)
    def _():
        kb_ref[...] = k_ref[...].astype(BF16)
        vb_ref[...] = v_ref[...].astype(BF16)

    q = (q_ref[...] * (SCALE * LOG2E)).astype(BF16)
    uu = uu_ref[...]
    r0 = pl.multiple_of(qi * t, t)
    run = jnp.zeros((t, 1), F32)
    acc = jnp.zeros((t, D_HEAD), F32)
    vis = _iota((t, t), 1) < _iota((t, t), 0)
    run, acc = _sb_group(q, kb_ref[pl.ds(r0, t), :], vb_ref[pl.ds(r0, t), :], uu, run, acc, vis)

    def body(g, carry):
        k0 = pl.multiple_of((qi - 1 - g) * t, t)
        return _sb_group(q, kb_ref[pl.ds(k0, t), :], vb_ref[pl.ds(k0, t), :], uu, carry[0], carry[1], None)

    run, acc = lax.fori_loop(0, qi, body, (run, acc))
    o_ref[...] = acc.astype(o_ref.dtype)


def _suffix_matrix(cb):
    j = jnp.arange(cb)[:, None]
    s = jnp.arange(cb)[None, :]
    u = (j > s).astype(BF16)
    return jnp.concatenate([u, u], axis=0)


def sb_attention_prompt(q, k, v, nb, seq, t=SB_T, cb=SB_CB):
    nh = q.shape[1] // D_HEAD
    kv_spec = pl.BlockSpec((seq, D_HEAD), lambda b, h, qi: (b, h))
    q_spec = pl.BlockSpec((t, D_HEAD), lambda b, h, qi: (b * (seq // t) + qi, h))
    return pl.pallas_call(
        functools.partial(_sb_prompt_kernel, t=t),
        grid=(nb, nh, seq // t),
        in_specs=[q_spec, kv_spec, kv_spec, pl.BlockSpec((2 * cb, cb), lambda b, h, qi: (0, 0))],
        out_specs=q_spec,
        out_shape=jax.ShapeDtypeStruct(q.shape, BF16),
        scratch_shapes=[pltpu.VMEM((seq, D_HEAD), BF16), pltpu.VMEM((seq, D_HEAD), BF16)],
        compiler_params=_params("arbitrary", "arbitrary", "arbitrary"),
        name="sb_attention_prompt",
    )(q, k, v, _suffix_matrix(cb))


def _head_copy(cache_hbm, buf_ref, sem_ref, b, h, slot):
    return pltpu.make_async_copy(cache_hbm.at[b, :, h, :], buf_ref.at[slot], sem_ref.at[slot])


def _prefetch_heads(caches, bufs, sems):
    b, h = pl.program_id(0), pl.program_id(1)
    nb, nh = pl.num_programs(0), pl.num_programs(1)
    step = b * nh + h
    slot = lax.rem(step, 2)
    wrap = h + 1 == nh
    b1 = jnp.where(wrap, b + 1, b)
    h1 = jnp.where(wrap, 0, h + 1)

    @pl.when(step == 0)
    def _():
        for c, buf, sem in zip(caches, bufs, sems):
            _head_copy(c, buf, sem, b, h, slot).start()

    @pl.when(step + 1 < nb * nh)
    def _():
        for c, buf, sem in zip(caches, bufs, sems):
            _head_copy(c, buf, sem, b1, h1, 1 - slot).start()

    for c, buf, sem in zip(caches, bufs, sems):
        _head_copy(c, buf, sem, b, h, slot).wait()
    return slot


def _sb_sample_kernel(q_ref, kn_ref, vn_ref, kc_hbm, vc_hbm, uu_ref, o_ref, kbuf, vbuf, ksem, vsem, kb_ref, vb_ref,
                      *, ts, past, pad):
    slot = _prefetch_heads((kc_hbm, vc_hbm), (kbuf, vbuf), (ksem, vsem))
    kb_ref[pl.ds(0, past), :] = kbuf[slot].astype(BF16)
    vb_ref[pl.ds(0, past), :] = vbuf[slot].astype(BF16)
    kb_ref[pl.ds(past, pad), :] = jnp.zeros((pad, D_HEAD), BF16)
    vb_ref[pl.ds(past, pad), :] = jnp.zeros((pad, D_HEAD), BF16)
    kb_ref[pl.ds(past, ts), :] = kn_ref[...].astype(BF16)
    vb_ref[pl.ds(past, ts), :] = vn_ref[...].astype(BF16)
    nk = past + pad
    vis = _iota((ts, nk), 1) < past + _iota((ts, nk), 0)
    _, acc = _sb_group((q_ref[...] * (SCALE * LOG2E)).astype(BF16), kb_ref[...], vb_ref[...], uu_ref[...],
                       jnp.zeros((ts, 1), F32), jnp.zeros((ts, D_HEAD), F32), vis)
    o_ref[...] = acc.astype(o_ref.dtype)


def sb_attention_sample(q, kn, vn, kc, vc, nb, ts, past, nh, cb=SB_CB):
    assert past % cb == 0 and ts <= cb
    new_spec = pl.BlockSpec((ts, D_HEAD), lambda b, h: (b, h))
    hbm = pl.BlockSpec(memory_space=pl.ANY)
    return pl.pallas_call(
        functools.partial(_sb_sample_kernel, ts=ts, past=past, pad=cb),
        grid=(nb, nh),
        in_specs=[new_spec, new_spec, new_spec, hbm, hbm, pl.BlockSpec((2 * cb, cb), lambda b, h: (0, 0))],
        out_specs=new_spec,
        out_shape=jax.ShapeDtypeStruct(q.shape, BF16),
        scratch_shapes=[pltpu.VMEM((2, past, D_HEAD), F32), pltpu.VMEM((2, past, D_HEAD), F32),
                        pltpu.SemaphoreType.DMA((2,)), pltpu.SemaphoreType.DMA((2,)),
                        pltpu.VMEM((past + cb, D_HEAD), BF16), pltpu.VMEM((past + cb, D_HEAD), BF16)],
        compiler_params=_params("arbitrary", "arbitrary"),
        name="sb_attention_sample",
    )(q, kn, vn, kc, vc, _suffix_matrix(cb))


def _band_bias_kernel(tab_ref, o_ref, *, rows, cols, width):
    h = pl.program_id(0)
    band = BAND_CHUNKS * CHUNK
    x = _iota((8, width), 1)
    idx = jnp.clip(x - rows - band, -REL_CLIP, REL_CLIP) + REL_CLIP

    def body(r, g):
        return jnp.where(idx == r, tab_ref[h, r] * LOG2E, g)

    g = lax.fori_loop(0, 2 * REL_CLIP + 1, body, jnp.zeros((8, width), F32))
    gt = jnp.broadcast_to(g[0:1, :], (rows, width))
    gt = pltpu.roll(gt, 0, 1, stride=1, stride_axis=0)
    bias = gt[:, rows:rows + cols]
    dc = _iota((rows, cols), 1) // CHUNK - _iota((rows, cols), 0) // CHUNK
    o_ref[0] = jnp.where((dc >= 0) & (dc <= BAND_CHUNKS), bias, NEG)


def band_bias(table, qchunks):
    nh = table.shape[0]
    rows = qchunks * CHUNK
    cols = (qchunks + BAND_CHUNKS) * CHUNK
    width = rows + cols
    assert width % LANES == 0 and rows % LANES == 0
    return pl.pallas_call(
        functools.partial(_band_bias_kernel, rows=rows, cols=cols, width=width),
        grid=(nh,),
        in_specs=[pl.BlockSpec(memory_space=pltpu.SMEM)],
        out_specs=pl.BlockSpec((1, rows, cols), lambda h: (h, 0, 0)),
        out_shape=jax.ShapeDtypeStruct((nh, rows, cols), F32),
        compiler_params=_params("arbitrary"),
        name="band_bias",
    )(table)


def _softmax_rows_base2(s2):
    m = jnp.max(s2, axis=-1, keepdims=True)
    p = jnp.exp2(s2 - m)
    return p / jnp.sum(p, axis=-1, keepdims=True)


def _band_prompt_kernel(q_ref, k_ref, v_ref, bias_ref, gq_ref, gk_ref, o_ref, kt_ref, vt_ref, kb_ref, vb_ref,
                        *, seq, rows, cols, keep):
    band = BAND_CHUNKS * CHUNK
    kn = _rms(k_ref[...], gk_ref[...])
    kt_ref[...] = kn[seq - keep:, :]
    vt_ref[...] = v_ref[pl.ds(seq - keep, keep), :]
    kb_ref[pl.ds(0, band), :] = jnp.zeros((band, D_HEAD), BF16)
    vb_ref[pl.ds(0, band), :] = jnp.zeros((band, D_HEAD), BF16)
    kb_ref[pl.ds(band, seq), :] = kn.astype(BF16)
    vb_ref[pl.ds(band, seq), :] = v_ref[...].astype(BF16)
    bias = bias_ref[0]
    gq = gq_ref[...]
    col = _iota((rows, cols), 1)

    def body(t, carry):
        r0 = pl.multiple_of(t * rows, rows)
        q = (_rms(q_ref[pl.ds(r0, rows), :], gq) * (SCALE * LOG2E)).astype(BF16)
        k = kb_ref[pl.ds(r0, cols), :]
        v = vb_ref[pl.ds(r0, cols), :]
        s = lax.dot_general(q, k, (((1,), (1,)), ((), ())), preferred_element_type=F32) + bias
        s = jnp.where(col >= band - r0, s, NEG)
        p = _softmax_rows_base2(s)
        o_ref[pl.ds(r0, rows), :] = jnp.dot(p.astype(BF16), v, preferred_element_type=F32).astype(o_ref.dtype)
        return carry

    lax.fori_loop(0, seq // rows, body, 0)


def band_attention_prompt(q, k, v, bias, g_q, g_k, nb, seq, keep):
    nh = q.shape[1] // D_HEAD
    rows, cols = bias.shape[1], bias.shape[2]
    band = BAND_CHUNKS * CHUNK
    blk = pl.BlockSpec((seq, D_HEAD), lambda b, h: (b, h))
    tail = pl.BlockSpec((keep, D_HEAD), lambda b, h: (b, h))
    gspec = pl.BlockSpec((1, D_HEAD), lambda b, h: (0, 0))
    return pl.pallas_call(
        functools.partial(_band_prompt_kernel, seq=seq, rows=rows, cols=cols, keep=keep),
        grid=(nb, nh),
        in_specs=[blk, blk, blk, pl.BlockSpec((1, rows, cols), lambda b, h: (h, 0, 0)), gspec, gspec],
        out_specs=[blk, tail, tail],
        out_shape=[jax.ShapeDtypeStruct(q.shape, BF16),
                   jax.ShapeDtypeStruct((nb * keep, q.shape[1]), F32),
                   jax.ShapeDtypeStruct((nb * keep, q.shape[1]), F32)],
        scratch_shapes=[pltpu.VMEM((band + seq, D_HEAD), BF16), pltpu.VMEM((band + seq, D_HEAD), BF16)],
        compiler_params=_params("arbitrary", "arbitrary"),
        name="band_attention_prompt",
    )(q, k, v, bias, g_q.reshape(1, D_HEAD), g_k.reshape(1, D_HEAD))


def _head_rows(ref, h):
    return ref[:, h, :]


def _band_sample_kernel(q_ref, kn_ref, vn_ref, kc_ref, vc_ref, bias_ref, gq_ref, gk_ref, o_ref, ko_ref,
                        *, ts, w, nh, cols):
    d = D_HEAD
    gq, gk = gq_ref[...], gk_ref[...]
    zeros = jnp.zeros((cols - w - ts, d), BF16)
    valid = _iota((ts, cols), 1) < w + ts
    for h in range(nh):
        hs = slice(h * d, (h + 1) * d)
        kn = _rms(kn_ref[:, hs], gk)
        ko_ref[:, hs] = kn
        k = jnp.concatenate([_head_rows(kc_ref, h).astype(BF16), kn.astype(BF16), zeros], axis=0)
        v = jnp.concatenate([_head_rows(vc_ref, h).astype(BF16), vn_ref[:, hs].astype(BF16), zeros], axis=0)
        q = (_rms(q_ref[:, hs], gq) * (SCALE * LOG2E)).astype(BF16)
        s = lax.dot_general(q, k, (((1,), (1,)), ((), ())), preferred_element_type=F32) + bias_ref[h]
        p = _softmax_rows_base2(jnp.where(valid, s, NEG))
        o_ref[:, hs] = jnp.dot(p.astype(BF16), v, preferred_element_type=F32).astype(o_ref.dtype)


def band_attention_sample(q, kn, vn, kc, vc, bias, g_q, g_k, nb, ts, w, nh):
    cols = -(-(w + ts) // LANES) * LANES
    assert w == BAND_CHUNKS * CHUNK and ts <= CHUNK and cols <= bias.shape[2] and ts % 16 == 0
    new_spec = pl.BlockSpec((ts, nh * D_HEAD), lambda b: (b, 0))
    cache_spec = pl.BlockSpec((None, w, nh, D_HEAD), lambda b: (b, 0, 0, 0))
    gspec = pl.BlockSpec((1, D_HEAD), lambda b: (0, 0))
    return pl.pallas_call(
        functools.partial(_band_sample_kernel, ts=ts, w=w, nh=nh, cols=cols),
        grid=(nb,),
        in_specs=[new_spec, new_spec, new_spec, cache_spec, cache_spec,
                  pl.BlockSpec((nh, ts, cols), lambda b: (0, 0, 0)), gspec, gspec],
        out_specs=[new_spec, new_spec],
        out_shape=[jax.ShapeDtypeStruct(q.shape, BF16), jax.ShapeDtypeStruct(q.shape, F32)],
        compiler_params=_params("arbitrary"),
        name="band_attention_sample",
    )(q, kn, vn, kc, vc, bias, g_q.reshape(1, D_HEAD), g_k.reshape(1, D_HEAD))


def _norm_rope(x, g, cosf, sina, sinb):
    half = ROT_DIM // 2
    y = _rms(x, g)
    return y * cosf + pltpu.roll(y, D_HEAD - half, 1) * sina + pltpu.roll(y, half, 1) * sinb


def rope_tables(pos):
    half = ROT_DIM // 2
    inv_freq = ROPE_THETA ** (-jnp.arange(half, dtype=F32) * (2.0 / ROT_DIM))
    ang = pos.astype(F32)[:, None] * inv_freq[None, :]
    cos, sin = jnp.cos(ang), jnp.sin(ang)
    n = pos.shape[0]
    rest = D_HEAD - ROT_DIM
    cosf = jnp.concatenate([cos, cos, jnp.ones((n, rest), F32)], axis=1)
    sina = jnp.concatenate([-sin, jnp.zeros((n, rest + half), F32)], axis=1)
    sinb = jnp.concatenate([jnp.zeros((n, half), F32), sin, jnp.zeros((n, rest), F32)], axis=1)
    return cosf, sina, sinb


def _online_block(q, k, v, mask, m, l, acc):
    s = lax.dot_general(q, k, (((1,), (1,)), ((), ())), preferred_element_type=F32)
    if mask is not None:
        s = jnp.where(mask, s, NEG)
    m_new = jnp.maximum(m, jnp.max(s, axis=-1, keepdims=True))
    alpha = jnp.exp2(m - m_new)
    p = jnp.exp2(s - m_new)
    l = alpha * l + jnp.sum(p, axis=-1, keepdims=True)
    acc = alpha * acc + jnp.dot(p.astype(BF16), v, preferred_element_type=F32)
    return m_new, l, acc


def _diff_lambda(lq1, lk1, lq2, lk2):
    return (jnp.exp(jnp.sum(lq1 * lk1, axis=-1, keepdims=True))
            - jnp.exp(jnp.sum(lq2 * lk2, axis=-1, keepdims=True)) + LAMBDA_INIT)


def _diff_finish(state, lam, gsub):
    (_, l1, a1), (_, l2, a2) = state
    o = a1 / l1 - lam * (a2 / l2)
    return _rms(o, gsub) * (1.0 - LAMBDA_INIT)


def _diff_init(tq):
    return (jnp.full((tq, 1), NEG, F32), jnp.zeros((tq, 1), F32), jnp.zeros((tq, 2 * D_HEAD), F32))


def _diff_prompt_kernel(q_ref, k_ref, v_ref, cos_ref, sa_ref, sb_ref, gq_ref, gk_ref, lam_ref, gsub_ref,
                        o_ref, ko_ref, k1_ref, k2_ref, vb_ref, *, t):
    qi = pl.program_id(2)
    d = D_HEAD

    @pl.when(qi == 0)
    def _():
        gk = gk_ref[...]
        cosf, sina, sinb = cos_ref[...], sa_ref[...], sb_ref[...]
        k1 = _norm_rope(k_ref[:, :d], gk, cosf, sina, sinb)
        k2 = _norm_rope(k_ref[:, d:], gk, cosf, sina, sinb)
        ko_ref[:, :d] = k1
        ko_ref[:, d:] = k2
        k1_ref[...] = k1.astype(BF16)
        k2_ref[...] = k2.astype(BF16)
        vb_ref[...] = v_ref[...].astype(BF16)

    r0 = pl.multiple_of(qi * t, t)
    gq = gq_ref[...]
    cosf, sina, sinb = cos_ref[pl.ds(r0, t), :], sa_ref[pl.ds(r0, t), :], sb_ref[pl.ds(r0, t), :]
    q1 = (_norm_rope(q_ref[:, :d], gq, cosf, sina, sinb) * (SCALE * LOG2E)).astype(BF16)
    q2 = (_norm_rope(q_ref[:, d:], gq, cosf, sina, sinb) * (SCALE * LOG2E)).astype(BF16)

    def step(k0, state, mask):
        v = vb_ref[pl.ds(k0, t), :]
        s1 = _online_block(q1, k1_ref[pl.ds(k0, t), :], v, mask, *state[0])
        s2 = _online_block(q2, k2_ref[pl.ds(k0, t), :], v, mask, *state[1])
        return s1, s2

    def body(kb, state):
        return step(pl.multiple_of(kb * t, t), state, None)

    state = lax.fori_loop(0, qi, body, (_diff_init(t), _diff_init(t)))
    mask = (_iota((t, t), 1) // CHUNK) <= (_iota((t, t), 0) // CHUNK)
    state = step(r0, state, mask)
    lam = _diff_lambda(lam_ref[0:1, :], lam_ref[1:2, :], lam_ref[2:3, :], lam_ref[3:4, :])
    o_ref[...] = _diff_finish(state, lam, gsub_ref[...]).astype(o_ref.dtype)


def diff_attention_prompt(q, k, v, tabs, g_q, g_k, lam4, g_sub, nb, seq, t=DIFF_T):
    d2 = 2 * D_HEAD
    nh = q.shape[1] // d2
    assert t % CHUNK == 0
    kv_spec = pl.BlockSpec((seq, d2), lambda b, h, qi: (b, h))
    q_spec = pl.BlockSpec((t, d2), lambda b, h, qi: (b * (seq // t) + qi, h))
    tab_spec = pl.BlockSpec((seq, D_HEAD), lambda b, h, qi: (0, 0))
    g_spec = pl.BlockSpec((1, D_HEAD), lambda b, h, qi: (0, 0))
    return pl.pallas_call(
        functools.partial(_diff_prompt_kernel, t=t),
        grid=(nb, nh, seq // t),
        in_specs=[q_spec, kv_spec, kv_spec, tab_spec, tab_spec, tab_spec, g_spec, g_spec,
                  pl.BlockSpec((4, D_HEAD), lambda b, h, qi: (0, 0)),
                  pl.BlockSpec((1, d2), lambda b, h, qi: (0, 0))],
        out_specs=[q_spec, kv_spec],
        out_shape=[jax.ShapeDtypeStruct(q.shape, BF16), jax.ShapeDtypeStruct(k.shape, F32)],
        scratch_shapes=[pltpu.VMEM((seq, D_HEAD), BF16), pltpu.VMEM((seq, D_HEAD), BF16),
                        pltpu.VMEM((seq, d2), BF16)],
        compiler_params=_params("arbitrary", "arbitrary", "arbitrary"),
        name="diff_attention_prompt",
    )(q, k, v, *tabs, g_q.reshape(1, D_HEAD), g_k.reshape(1, D_HEAD), lam4, g_sub.reshape(1, d2))


def _diff_sample_kernel(q_ref, kn_ref, vn_ref, kc_hbm, vc_hbm, cos_ref, sa_ref, sb_ref, gq_ref, gk_ref,
                        lam_ref, gsub_ref, o_ref, ko_ref, kbuf, vbuf, ksem, vsem, k1_ref, k2_ref, vb_ref,
                        *, ts, past, tn):
    slot = _prefetch_heads((kc_hbm, vc_hbm), (kbuf, vbuf), (ksem, vsem))
    d = D_HEAD
    cosf, sina, sinb = cos_ref[...], sa_ref[...], sb_ref[...]
    gk = gk_ref[...]
    k1 = _norm_rope(kn_ref[:, :d], gk, cosf, sina, sinb)
    k2 = _norm_rope(kn_ref[:, d:], gk, cosf, sina, sinb)
    ko_ref[:, :d] = k1
    ko_ref[:, d:] = k2
    k1_ref[pl.ds(0, past), :] = kbuf[slot, :, :d].astype(BF16)
    k2_ref[pl.ds(0, past), :] = kbuf[slot, :, d:].astype(BF16)
    vb_ref[pl.ds(0, past), :] = vbuf[slot].astype(BF16)
    k1_ref[pl.ds(past, tn), :] = jnp.zeros((tn, d), BF16)
    k2_ref[pl.ds(past, tn), :] = jnp.zeros((tn, d), BF16)
    vb_ref[pl.ds(past, tn), :] = jnp.zeros((tn, 2 * d), BF16)
    k1_ref[pl.ds(past, ts), :] = k1.astype(BF16)
    k2_ref[pl.ds(past, ts), :] = k2.astype(BF16)
    vb_ref[pl.ds(past, ts), :] = vn_ref[...].astype(BF16)
    gq = gq_ref[...]
    q1 = (_norm_rope(q_ref[:, :d], gq, cosf, sina, sinb) * (SCALE * LOG2E)).astype(BF16)
    q2 = (_norm_rope(q_ref[:, d:], gq, cosf, sina, sinb) * (SCALE * LOG2E)).astype(BF16)

    nk = past + tn
    kpos = _iota((ts, nk), 1)
    qpos = past + _iota((ts, nk), 0)
    mask = ((kpos // CHUNK) <= (qpos // CHUNK)) & (kpos < past + ts)
    v = vb_ref[...]
    state = (_online_block(q1, k1_ref[...], v, mask, *_diff_init(ts)),
             _online_block(q2, k2_ref[...], v, mask, *_diff_init(ts)))
    lam = _diff_lambda(lam_ref[0:1, :], lam_ref[1:2, :], lam_ref[2:3, :], lam_ref[3:4, :])
    o_ref[...] = _diff_finish(state, lam, gsub_ref[...]).astype(o_ref.dtype)


def diff_attention_sample(q, kn, vn, kc, vc, tabs, g_q, g_k, lam4, g_sub, nb, ts, past, nh, tn=LANES):
    d2 = 2 * D_HEAD
    assert past % LANES == 0 and ts <= tn
    new_spec = pl.BlockSpec((ts, d2), lambda b, h: (b, h))
    hbm = pl.BlockSpec(memory_space=pl.ANY)
    tab_spec = pl.BlockSpec((ts, D_HEAD), lambda b, h: (0, 0))
    g_spec = pl.BlockSpec((1, D_HEAD), lambda b, h: (0, 0))
    return pl.pallas_call(
        functools.partial(_diff_sample_kernel, ts=ts, past=past, tn=tn),
        grid=(nb, nh),
        in_specs=[new_spec, new_spec, new_spec, hbm, hbm, tab_spec, tab_spec, tab_spec,
                  g_spec, g_spec, pl.BlockSpec((4, D_HEAD), lambda b, h: (0, 0)),
                  pl.BlockSpec((1, d2), lambda b, h: (0, 0))],
        out_specs=[new_spec, new_spec],
        out_shape=[jax.ShapeDtypeStruct(q.shape, BF16), jax.ShapeDtypeStruct(q.shape, F32)],
        scratch_shapes=[pltpu.VMEM((2, past, d2), F32), pltpu.VMEM((2, past, d2), F32),
                        pltpu.SemaphoreType.DMA((2,)), pltpu.SemaphoreType.DMA((2,)),
                        pltpu.VMEM((past + tn, D_HEAD), BF16), pltpu.VMEM((past + tn, D_HEAD), BF16),
                        pltpu.VMEM((past + tn, d2), BF16)],
        compiler_params=_params("arbitrary", "arbitrary"),
        name="diff_attention_sample",
    )(q, kn, vn, kc, vc, *tabs, g_q.reshape(1, D_HEAD), g_k.reshape(1, D_HEAD), lam4, g_sub.reshape(1, d2))


def _to_time_major(x, nb, ts):
    return x.reshape(nb, ts, -1).transpose(1, 0, 2).reshape(nb * ts, -1)


def _to_batch_major(x, nb, ts):
    return x.reshape(ts, nb, -1).transpose(1, 0, 2).reshape(nb * ts, -1)


def _conv_glu_layer(hp, hs, g_norm, w_up, w_conv, w_down_bf16, layer, state, nb_s, ts, seq):
    mp = hp.shape[0]
    hs_t = _to_time_major(hs, nb_s, ts)
    xn = rmsnorm_rows(hp, hs_t, g_norm)
    hidden, conv_p, conv_s_t = up_convglu(xn, w_up, w_conv, layer, state.transpose(1, 0, 2), mp, seq)
    hp, hs_t = down_project(hidden, w_down_bf16, layer, hp, hs_t)
    return hp, _to_batch_major(hs_t, nb_s, ts), conv_p, conv_s_t.transpose(1, 0, 2)


def kernel(x_prompt, x_sample, cache_k_sb, cache_v_sb, cache_k_band, cache_v_band, cache_k_diff, cache_v_diff, state_conv_ffn, norm_mix, norm_ffn, w_in_ab, w_out_ab, g_q_band, g_k_band, rel_bias_band, w_in_diff, w_out_diff, g_q_diff, g_k_diff, lambda_q1, lambda_k1, lambda_q2, lambda_k2, g_sub_diff, w_up, w_conv, w_down):
    bp, tp, dm = x_prompt.shape
    bs, ts, _ = x_sample.shape
    past = cache_k_sb.shape[1]
    h_sb, h_cb, h_df = cache_k_sb.shape[2], cache_k_band.shape[2], cache_k_diff.shape[2]
    d_sb, d_cb, d_df = h_sb * D_HEAD, h_cb * D_HEAD, h_df * 2 * D_HEAD
    w_band = cache_k_band.shape[1]
    keep = min(BAND_CHUNKS * CHUNK, tp)
    mp, ms = bp * tp, bs * ts

    hp = x_prompt.reshape(mp, dm)
    hs = x_sample.reshape(ms, dm)

    xn = rmsnorm_rows(hp, hs, norm_mix[0])
    sec = []
    col = 0
    for width in (d_sb, d_sb, d_sb, d_cb, d_cb, d_cb):
        sec.append(project(xn, w_in_ab, col, width, mp))
        col += width
    (qa_p, qa_s), (ka_p, ka_s), (va_p, va_s), (qb_p, qb_s), (kb_p, kb_s), (vb_p, vb_s) = sec

    oa_p = sb_attention_prompt(qa_p, ka_p, va_p, bp, tp)
    oa_s = sb_attention_sample(qa_s, ka_s, va_s, cache_k_sb, cache_v_sb, bs, ts, past, h_sb)
    bias = band_bias(rel_bias_band, BAND_QCHUNKS)
    ob_p, k_band_p, v_band_p = band_attention_prompt(qb_p, kb_p, vb_p, bias, g_q_band, g_k_band, bp, tp, keep)
    ob_s, k_band_s = band_attention_sample(qb_s, kb_s, vb_s, cache_k_band, cache_v_band, bias,
                                           g_q_band, g_k_band, bs, ts, w_band, h_cb)
    hp, hs = out_project([oa_p, ob_p], [oa_s, ob_s], w_out_ab, hp, hs)
    w_down_bf16 = w_down.astype(BF16)
    hp, hs, conv_p0, conv_s0 = _conv_glu_layer(hp, hs, norm_ffn[0], w_up, w_conv, w_down_bf16, 0,
                                               state_conv_ffn[0], bs, ts, tp)

    xn = rmsnorm_rows(hp, hs, norm_mix[1])
    (q_p, q_s), (k_p, k_s), (v_p, v_s) = [project(xn, w_in_diff, c * d_df, d_df, mp) for c in range(3)]
    lam4 = jnp.stack([lambda_q1, lambda_k1, lambda_q2, lambda_k2]).astype(F32)
    o_p, k_diff_p = diff_attention_prompt(q_p, k_p, v_p, rope_tables(jnp.arange(tp)), g_q_diff, g_k_diff,
                                          lam4, g_sub_diff, bp, tp)
    o_s, k_diff_s = diff_attention_sample(q_s, k_s, v_s, cache_k_diff, cache_v_diff,
                                          rope_tables(past + jnp.arange(ts)), g_q_diff, g_k_diff,
                                          lam4, g_sub_diff, bs, ts, past, h_df)
    hp, hs = out_project([o_p], [o_s], w_out_diff, hp, hs)
    hp, hs, conv_p1, conv_s1 = _conv_glu_layer(hp, hs, norm_ffn[1], w_up, w_conv, w_down_bf16, 1,
                                               state_conv_ffn[1], bs, ts, tp)

    return (hp.reshape(bp, tp, dm), hs.reshape(bs, ts, dm),
            ka_p.reshape(bp, tp, h_sb, D_HEAD), va_p.reshape(bp, tp, h_sb, D_HEAD),
            k_band_p.reshape(bp, keep, h_cb, D_HEAD), v_band_p.reshape(bp, keep, h_cb, D_HEAD),
            k_diff_p.reshape(bp, tp, h_df, 2 * D_HEAD), v_p.reshape(bp, tp, h_df, 2 * D_HEAD),
            jnp.stack([conv_p0, conv_p1]),
            ka_s.reshape(bs, ts, h_sb, D_HEAD), va_s.reshape(bs, ts, h_sb, D_HEAD),
            k_band_s.reshape(bs, ts, h_cb, D_HEAD), vb_s.reshape(bs, ts, h_cb, D_HEAD),
            k_diff_s.reshape(bs, ts, h_df, 2 * D_HEAD), v_s.reshape(bs, ts, h_df, 2 * D_HEAD),
            jnp.stack([conv_s0, conv_s1]))
```

```python
import functools
import math

import jax
import jax.numpy as jnp
from jax import lax
from jax.experimental import pallas as pl
from jax.experimental.pallas import tpu as pltpu

F32 = jnp.float32
BF16 = jnp.bfloat16

CHUNK = 64
D_HEAD = 128
BAND_CHUNKS = 8
REL_CLIP = 128
ROT_DIM = D_HEAD // 4
ROPE_THETA = 500000.0
CONV_W = 3
EPS = 1e-6
NEG = -1e30
LAMBDA_INIT = 0.8 - 0.6 * math.exp(-0.3 * 1)
SCALE = 1.0 / math.sqrt(D_HEAD)
LOG2E = math.log2(math.e)

V7X_VMEM_LIMIT_BYTES = 56 * 1024 * 1024
LANES = 128

ROW_TILE = 1024
OUT_ROW_TILE = 512
UP_COL_BLOCK = 256
UP_ROW_CHUNK = 512
DOWN_ROW_TILE = 256
SB_T = 512
SB_CB = 256
BAND_QCHUNKS = 4
BAND_UNROLL = 4
SAMPLE_HEAD_GROUP = 4
DIFF_T = 512


def _params(*sem):
    return pltpu.CompilerParams(dimension_semantics=sem, vmem_limit_bytes=V7X_VMEM_LIMIT_BYTES)


def _iota(shape, dim):
    return lax.broadcasted_iota(jnp.int32, shape, dim)


def _rms(x, g):
    return x * lax.rsqrt(jnp.mean(x * x, axis=-1, keepdims=True) + EPS) * g


def _rmsnorm_kernel(xp_ref, xs_ref, g_ref, o_ref, *, np_tiles):
    i = pl.program_id(0)

    @pl.when(i < np_tiles)
    def _():
        o_ref[...] = _rms(xp_ref[...], g_ref[...]).astype(o_ref.dtype)

    @pl.when(i >= np_tiles)
    def _():
        o_ref[...] = _rms(xs_ref[...], g_ref[...]).astype(o_ref.dtype)


def rmsnorm_rows(xp, xs, g, tm=256):
    mp, d = xp.shape
    ms = xs.shape[0]
    np_tiles, ns_tiles = mp // tm, ms // tm
    return pl.pallas_call(
        functools.partial(_rmsnorm_kernel, np_tiles=np_tiles),
        grid=(np_tiles + ns_tiles,),
        in_specs=[
            pl.BlockSpec((tm, d), lambda i: (jnp.minimum(i, np_tiles - 1), 0)),
            pl.BlockSpec((tm, d), lambda i: (jnp.maximum(i - np_tiles, 0), 0)),
            pl.BlockSpec((1, d), lambda i: (0, 0)),
        ],
        out_specs=pl.BlockSpec((tm, d), lambda i: (i, 0)),
        out_shape=jax.ShapeDtypeStruct((mp + ms, d), BF16),
        compiler_params=_params("arbitrary"),
        name="rmsnorm_rows",
    )(xp, xs, g.reshape(1, d))


def _proj_kernel(x_ref, w_ref, op_ref, os_ref, wb_ref, *, np_tiles, ms):
    i = pl.program_id(1)

    @pl.when(i == 0)
    def _():
        wb_ref[...] = w_ref[...].astype(BF16)

    @pl.when(i < np_tiles)
    def _():
        op_ref[...] = jnp.dot(x_ref[...], wb_ref[...], preferred_element_type=F32)

    @pl.when(i >= np_tiles)
    def _():
        os_ref[...] = jnp.dot(x_ref[0:ms, :], wb_ref[...], preferred_element_type=F32)


def project(x, w, col0, n, mp, tm=ROW_TILE, tn=512):
    m, k = x.shape
    ms = m - mp
    assert mp % tm == 0 and 0 < ms <= tm
    np_tiles = mp // tm
    jb = col0 // tn
    return pl.pallas_call(
        functools.partial(_proj_kernel, np_tiles=np_tiles, ms=ms),
        grid=(n // tn, np_tiles + 1),
        in_specs=[
            pl.BlockSpec((tm, k), lambda j, i: (i, 0)),
            pl.BlockSpec((k, tn), lambda j, i: (0, jb + j)),
        ],
        out_specs=[
            pl.BlockSpec((tm, tn), lambda j, i: (jnp.minimum(i, np_tiles - 1), j)),
            pl.BlockSpec((ms, tn), lambda j, i: (0, j)),
        ],
        out_shape=[jax.ShapeDtypeStruct((mp, n), F32), jax.ShapeDtypeStruct((ms, n), F32)],
        scratch_shapes=[pltpu.VMEM((k, tn), BF16)],
        compiler_params=_params("arbitrary", "arbitrary"),
        name="project",
    )(x, w)


def _outproj_kernel(*refs, np_tiles, nparts):
    ap = refs[0:nparts]
    a_s = refs[nparts:2 * nparts]
    w = refs[2 * nparts:3 * nparts]
    rp_ref, rs_ref, op_ref, os_ref, wb_ref = refs[3 * nparts:]
    i = pl.program_id(1)

    @pl.when(i == 0)
    def _():
        for p in range(nparts):
            wb_ref[p] = w[p][...].astype(BF16)

    def run(a_refs, r_ref, o_ref):
        acc = r_ref[...]
        for p in range(nparts):
            acc = acc + jnp.dot(a_refs[p][...], wb_ref[p], preferred_element_type=F32)
        o_ref[...] = acc

    @pl.when(i < np_tiles)
    def _():
        run(ap, rp_ref, op_ref)

    @pl.when(i >= np_tiles)
    def _():
        run(a_s, rs_ref, os_ref)


def out_project(parts_p, parts_s, w, res_p, res_s, tm=OUT_ROW_TILE, tn=512):
    nparts = len(parts_p)
    mp, kp = parts_p[0].shape
    ms = parts_s[0].shape[0]
    n = w.shape[1]
    np_tiles, ns_tiles = mp // tm, ms // tm
    pmap = lambda j, i: (jnp.minimum(i, np_tiles - 1), 0)
    smap = lambda j, i: (jnp.maximum(i - np_tiles, 0), 0)
    pmap_o = lambda j, i: (jnp.minimum(i, np_tiles - 1), j)
    smap_o = lambda j, i: (jnp.maximum(i - np_tiles, 0), j)
    in_specs = ([pl.BlockSpec((tm, kp), pmap)] * nparts + [pl.BlockSpec((tm, kp), smap)] * nparts
                + [pl.BlockSpec((kp, tn), functools.partial(lambda j, i, p: (p, j), p=p)) for p in range(nparts)]
                + [pl.BlockSpec((tm, tn), pmap_o), pl.BlockSpec((tm, tn), smap_o)])
    return pl.pallas_call(
        functools.partial(_outproj_kernel, np_tiles=np_tiles, nparts=nparts),
        grid=(n // tn, np_tiles + ns_tiles),
        in_specs=in_specs,
        out_specs=[pl.BlockSpec((tm, tn), pmap_o), pl.BlockSpec((tm, tn), smap_o)],
        out_shape=[jax.ShapeDtypeStruct((mp, n), F32), jax.ShapeDtypeStruct((ms, n), F32)],
        scratch_shapes=[pltpu.VMEM((nparts, kp, tn), BF16)],
        compiler_params=_params("arbitrary", "arbitrary"),
        name="out_project",
    )(*parts_p, *parts_s, *([w] * nparts), res_p, res_s)


def _silu_mul(gc, u):
    return gc * (1.0 / (1.0 + jnp.exp(-gc))) * u


def _up_weight_copies(w_hbm, wf_ref, wsem, jj, *, layer, nj, tb):
    j0 = 2 * jj
    j1 = jnp.minimum(j0 + 1, nj - 1)
    cols = (j0, j1, nj + j0, nj + j1)
    return [pltpu.make_async_copy(w_hbm.at[layer, :, pl.ds(pl.multiple_of(c * tb, tb), tb)], wf_ref.at[n], wsem.at[n])
            for n, c in enumerate(cols)]


def _up_kernel(x_ref, w_hbm, wc_ref, hist_ref, h_ref, cp_ref, cs_ref, wf_ref, wsem, wb_ref, carry_ref,
               *, layer, nj, tb, np_tiles, tiles_per_seq, nb_s, ms, rc):
    jj, i = pl.program_id(0), pl.program_id(1)
    tm = x_ref.shape[0]
    tn = 2 * tb
    copies = functools.partial(_up_weight_copies, w_hbm, wf_ref, wsem, layer=layer, nj=nj, tb=tb)

    @pl.when(i == 0)
    def _():
        @pl.when(jj == 0)
        def _():
            for cp in copies(jj):
                cp.start()

        for cp in copies(jj):
            cp.wait()
        for n in range(4):
            wb_ref[:, n * tb:(n + 1) * tb] = wf_ref[n].astype(BF16)

        @pl.when(jj + 1 < pl.num_programs(0))
        def _():
            for cp in copies(jj + 1):
                cp.start()

    w0 = wc_ref[0:1, :]
    w1 = wc_ref[1:2, :]
    w2 = wc_ref[2:3, :]

    @pl.when(lax.rem(i, tiles_per_seq) == 0)
    def _():
        carry_ref[...] = jnp.zeros_like(carry_ref)

    @pl.when(i < np_tiles)
    def _():
        c0 = carry_ref[0:1, :]
        c1 = carry_ref[1:2, :]
        row = _iota((rc, tn), 0)
        for r in range(tm // rc):
            rows = slice(r * rc, (r + 1) * rc)
            gu = jnp.dot(x_ref[rows, :], wb_ref[...], preferred_element_type=F32)
            g = gu[:, :tn]
            u = gu[:, tn:]
            gm1 = jnp.where(row == 0, c1, pltpu.roll(g, 1, 0))
            gm2 = jnp.where(row == 0, c0, jnp.where(row == 1, c1, pltpu.roll(g, 2, 0)))
            gc = gm2 * w0 + gm1 * w1 + g * w2
            h_ref[rows, :] = _silu_mul(gc, u).astype(h_ref.dtype)
            c0 = g[rc - 2:rc - 1, :]
            c1 = g[rc - 1:rc, :]
        tail = jnp.concatenate([c0, c1], axis=0)
        carry_ref[0:2, :] = tail
        cp_ref[0] = tail

    @pl.when(i >= np_tiles)
    def _():
        gu = jnp.dot(x_ref[0:ms, :], wb_ref[...], preferred_element_type=F32)
        g = gu[:, :tn]
        u = gu[:, tn:]
        h0 = hist_ref[0]
        h1 = hist_ref[1]
        gm1 = jnp.concatenate([h1, g[:ms - nb_s]], axis=0)
        gm2 = jnp.concatenate([h0, h1, g[:ms - 2 * nb_s]], axis=0)
        gc = gm2 * w0 + gm1 * w1 + g * w2
        h_ref[0:ms, :] = _silu_mul(gc, u).astype(h_ref.dtype)
        cs_ref[0] = g[ms - 2 * nb_s:ms - nb_s]
        cs_ref[1] = g[ms - nb_s:]


def up_convglu(x, w_up, w_conv, layer, hist_t, mp, seq_len, tm=ROW_TILE, tb=UP_COL_BLOCK, rc=UP_ROW_CHUNK):
    m, k = x.shape
    f = w_up.shape[2] // 2
    nb_s = hist_t.shape[1]
    nb_p = mp // seq_len
    np_tiles = mp // tm
    ms = m - mp
    assert 0 < ms <= tm and seq_len % tm == 0 and tm % rc == 0 and f % tb == 0 and CONV_W == 3
    tiles_per_seq = seq_len // tm
    nj = f // tb
    tn = 2 * tb
    njj = -(-nj // 2)
    return pl.pallas_call(
        functools.partial(_up_kernel, layer=layer, nj=nj, tb=tb, np_tiles=np_tiles, tiles_per_seq=tiles_per_seq,
                          nb_s=nb_s, ms=ms, rc=rc),
        grid=(njj, np_tiles + 1),
        in_specs=[
            pl.BlockSpec((tm, k), lambda j, i: (i, 0)),
            pl.BlockSpec(memory_space=pl.ANY),
            pl.BlockSpec((None, CONV_W, tn), lambda j, i: (layer, 0, j)),
            pl.BlockSpec((CONV_W - 1, nb_s, tn), lambda j, i: (0, 0, j)),
        ],
        out_specs=[
            pl.BlockSpec((tm, tn), lambda j, i: (i, j)),
            pl.BlockSpec((1, CONV_W - 1, tn), lambda j, i: (jnp.minimum(i // tiles_per_seq, nb_p - 1), 0, j)),
            pl.BlockSpec((CONV_W - 1, nb_s, tn), lambda j, i: (0, 0, j)),
        ],
        out_shape=[
            jax.ShapeDtypeStruct((m, f), BF16),
            jax.ShapeDtypeStruct((nb_p, CONV_W - 1, f), F32),
            jax.ShapeDtypeStruct((CONV_W - 1, nb_s, f), F32),
        ],
        scratch_shapes=[pltpu.VMEM((4, k, tb), F32), pltpu.SemaphoreType.DMA((4,)),
                        pltpu.VMEM((k, 2 * tn), BF16), pltpu.VMEM((8, tn), F32)],
        compiler_params=_params("arbitrary", "arbitrary"),
        name="up_convglu",
    )(x, w_up, w_conv, hist_t)


def _down_kernel(a_ref, w_ref, rp_ref, rs_ref, op_ref, os_ref, *, np_tiles):
    i = pl.program_id(1)

    @pl.when(i < np_tiles)
    def _():
        op_ref[...] = rp_ref[...] + jnp.dot(a_ref[...], w_ref[...], preferred_element_type=F32)

    @pl.when(i >= np_tiles)
    def _():
        os_ref[...] = rs_ref[...] + jnp.dot(a_ref[...], w_ref[...], preferred_element_type=F32)


def down_project(a, w, layer, res_p, res_s, tm=DOWN_ROW_TILE, tn=512):
    m, f = a.shape
    n = w.shape[2]
    mp = res_p.shape[0]
    ms = m - mp
    np_tiles, ns_tiles = mp // tm, ms // tm
    pmap_o = lambda j, i: (jnp.minimum(i, np_tiles - 1), j)
    smap_o = lambda j, i: (jnp.maximum(i - np_tiles, 0), j)
    return pl.pallas_call(
        functools.partial(_down_kernel, np_tiles=np_tiles),
        grid=(n // tn, np_tiles + ns_tiles),
        in_specs=[
            pl.BlockSpec((tm, f), lambda j, i: (i, 0)),
            pl.BlockSpec((None, f, tn), lambda j, i: (layer, 0, j)),
            pl.BlockSpec((tm, tn), pmap_o),
            pl.BlockSpec((tm, tn), smap_o),
        ],
        out_specs=[pl.BlockSpec((tm, tn), pmap_o), pl.BlockSpec((tm, tn), smap_o)],
        out_shape=[jax.ShapeDtypeStruct((mp, n), F32), jax.ShapeDtypeStruct((ms, n), F32)],
        compiler_params=_params("arbitrary", "arbitrary"),
        name="down_project",
    )(a, w, res_p, res_s)


def _sb_group(q, k, v, uu, run, acc, vis):
    nk = k.shape[0]
    cb = uu.shape[1]
    z2 = lax.dot_general(q, k, (((1,), (1,)), ((), ())), preferred_element_type=F32)
    s = jnp.maximum(z2, 0.0) + jnp.log2(1.0 + jnp.exp2(-jnp.abs(z2)))
    if vis is not None:
        s = jnp.where(vis, s, 0.0)
    hi = s.astype(BF16)
    lo = (s - hi.astype(F32)).astype(BF16)
    afters = []
    for blk in reversed(range(nk // cb)):
        sl = slice(blk * cb, (blk + 1) * cb)
        cs = jnp.dot(jnp.concatenate([hi[:, sl], lo[:, sl]], axis=1), uu, preferred_element_type=F32)
        afters.append(cs + run)
        run = run + jnp.sum(s[:, sl], axis=-1, keepdims=True)
    after = jnp.concatenate(afters[::-1], axis=1)
    w = jnp.exp2(z2 - s - after)
    if vis is not None:
        w = jnp.where(vis, w, 0.0)
    acc = acc + jnp.dot(w.astype(BF16), v, preferred_element_type=F32)
    return run, acc


def _sb_prompt_kernel(q_ref, k_ref, v_ref, uu_ref, o_ref, kb_ref, vb_ref, *, t):
    qi = pl.program_id(2)

    @pl.when(qi == 0)
    def _():
        kb_ref[...] = k_ref[...].astype(BF16)
        vb_ref[...] = v_ref[...].astype(BF16)

    q = (q_ref[...] * (SCALE * LOG2E)).astype(BF16)
    uu = uu_ref[...]
    r0 = pl.multiple_of(qi * t, t)
    run = jnp.zeros((t, 1), F32)
    acc = jnp.zeros((t, D_HEAD), F32)
    vis = _iota((t, t), 1) < _iota((t, t), 0)
    run, acc = _sb_group(q, kb_ref[pl.ds(r0, t), :], vb_ref[pl.ds(r0, t), :], uu, run, acc, vis)

    def body(g, carry):
        k0 = pl.multiple_of((qi - 1 - g) * t, t)
        return _sb_group(q, kb_ref[pl.ds(k0, t), :], vb_ref[pl.ds(k0, t), :], uu, carry[0], carry[1], None)

    run, acc = lax.fori_loop(0, qi, body, (run, acc))
    o_ref[...] = acc.astype(o_ref.dtype)


def _suffix_matrix(cb):
    j = jnp.arange(cb)[:, None]
    s = jnp.arange(cb)[None, :]
    u = (j > s).astype(BF16)
    return jnp.concatenate([u, u], axis=0)


def sb_attention_prompt(q, k, v, nb, seq, t=SB_T, cb=SB_CB):
    nh = q.shape[1] // D_HEAD
    kv_spec = pl.BlockSpec((seq, D_HEAD), lambda b, h, qi: (b, h))
    q_spec = pl.BlockSpec((t, D_HEAD), lambda b, h, qi: (b * (seq // t) + qi, h))
    return pl.pallas_call(
        functools.partial(_sb_prompt_kernel, t=t),
        grid=(nb, nh, seq // t),
        in_specs=[q_spec, kv_spec, kv_spec, pl.BlockSpec((2 * cb, cb), lambda b, h, qi: (0, 0))],
        out_specs=q_spec,
        out_shape=jax.ShapeDtypeStruct(q.shape, BF16),
        scratch_shapes=[pltpu.VMEM((seq, D_HEAD), BF16), pltpu.VMEM((seq, D_HEAD), BF16)],
        compiler_params=_params("arbitrary", "arbitrary", "arbitrary"),
        name="sb_attention_prompt",
    )(q, k, v, _suffix_matrix(cb))


def _head_copy(cache_hbm, buf_ref, sem_ref, b, h, slot, g):
    return pltpu.make_async_copy(cache_hbm.at[b, :, h, :], buf_ref.at[slot, g], sem_ref.at[slot, g])


def _prefetch_heads(caches, bufs, sems, hg):
    b, hh = pl.program_id(0), pl.program_id(1)
    nb, ng = pl.num_programs(0), pl.num_programs(1)
    step = b * ng + hh
    slot = lax.rem(step, 2)
    wrap = hh + 1 == ng
    b1 = jnp.where(wrap, b + 1, b)
    hh1 = jnp.where(wrap, 0, hh + 1)

    def copies(bb, hgrp, sl):
        return [_head_copy(c, buf, sem, bb, hgrp * hg + g, sl, g)
                for c, buf, sem in zip(caches, bufs, sems) for g in range(hg)]

    @pl.when(step == 0)
    def _():
        for cp in copies(b, hh, slot):
            cp.start()

    @pl.when(step + 1 < nb * ng)
    def _():
        for cp in copies(b1, hh1, 1 - slot):
            cp.start()

    for cp in copies(b, hh, slot):
        cp.wait()
    return slot


def _sb_sample_kernel(q_ref, kn_ref, vn_ref, kc_hbm, vc_hbm, uu_ref, o_ref, kbuf, vbuf, ksem, vsem, kb_ref, vb_ref,
                      *, ts, past, pad, hg):
    slot = _prefetch_heads((kc_hbm, vc_hbm), (kbuf, vbuf), (ksem, vsem), hg)
    d = D_HEAD
    nk = past + pad
    vis = _iota((ts, nk), 1) < past + _iota((ts, nk), 0)
    uu = uu_ref[...]
    for g in range(hg):
        cols = slice(g * d, (g + 1) * d)
        kb_ref[g, pl.ds(0, past), :] = kbuf[slot, g].astype(BF16)
        vb_ref[g, pl.ds(0, past), :] = vbuf[slot, g].astype(BF16)
        kb_ref[g, pl.ds(past, pad), :] = jnp.zeros((pad, d), BF16)
        vb_ref[g, pl.ds(past, pad), :] = jnp.zeros((pad, d), BF16)
        kb_ref[g, pl.ds(past, ts), :] = kn_ref[:, cols].astype(BF16)
        vb_ref[g, pl.ds(past, ts), :] = vn_ref[:, cols].astype(BF16)
        _, acc = _sb_group((q_ref[:, cols] * (SCALE * LOG2E)).astype(BF16), kb_ref[g], vb_ref[g], uu,
                           jnp.zeros((ts, 1), F32), jnp.zeros((ts, d), F32), vis)
        o_ref[:, cols] = acc.astype(o_ref.dtype)


def sb_attention_sample(q, kn, vn, kc, vc, nb, ts, past, nh, cb=SB_CB, hg=SAMPLE_HEAD_GROUP):
    assert past % cb == 0 and ts <= cb and nh % hg == 0
    new_spec = pl.BlockSpec((ts, hg * D_HEAD), lambda b, h: (b, h))
    hbm = pl.BlockSpec(memory_space=pl.ANY)
    return pl.pallas_call(
        functools.partial(_sb_sample_kernel, ts=ts, past=past, pad=cb, hg=hg),
        grid=(nb, nh // hg),
        in_specs=[new_spec, new_spec, new_spec, hbm, hbm, pl.BlockSpec((2 * cb, cb), lambda b, h: (0, 0))],
        out_specs=new_spec,
        out_shape=jax.ShapeDtypeStruct(q.shape, BF16),
        scratch_shapes=[pltpu.VMEM((2, hg, past, D_HEAD), F32), pltpu.VMEM((2, hg, past, D_HEAD), F32),
                        pltpu.SemaphoreType.DMA((2, hg)), pltpu.SemaphoreType.DMA((2, hg)),
                        pltpu.VMEM((hg, past + cb, D_HEAD), BF16), pltpu.VMEM((hg, past + cb, D_HEAD), BF16)],
        compiler_params=_params("arbitrary", "arbitrary"),
        name="sb_attention_sample",
    )(q, kn, vn, kc, vc, _suffix_matrix(cb))


def _band_bias_kernel(tab_ref, o_ref, *, rows, cols, width):
    h = pl.program_id(0)
    band = BAND_CHUNKS * CHUNK
    x = _iota((8, width), 1)
    idx = jnp.clip(x - rows - band, -REL_CLIP, REL_CLIP) + REL_CLIP

    def body(r, g):
        return jnp.where(idx == r, tab_ref[h, r] * LOG2E, g)

    g = lax.fori_loop(0, 2 * REL_CLIP + 1, body, jnp.zeros((8, width), F32))
    gt = jnp.broadcast_to(g[0:1, :], (rows, width))
    gt = pltpu.roll(gt, 0, 1, stride=1, stride_axis=0)
    bias = gt[:, rows:rows + cols]
    dc = _iota((rows, cols), 1) // CHUNK - _iota((rows, cols), 0) // CHUNK
    o_ref[0] = jnp.where((dc >= 0) & (dc <= BAND_CHUNKS), bias, NEG)


def band_bias(table, qchunks):
    nh = table.shape[0]
    rows = qchunks * CHUNK
    cols = (qchunks + BAND_CHUNKS) * CHUNK
    width = rows + cols
    assert width % LANES == 0 and rows % LANES == 0
    return pl.pallas_call(
        functools.partial(_band_bias_kernel, rows=rows, cols=cols, width=width),
        grid=(nh,),
        in_specs=[pl.BlockSpec(memory_space=pltpu.SMEM)],
        out_specs=pl.BlockSpec((1, rows, cols), lambda h: (h, 0, 0)),
        out_shape=jax.ShapeDtypeStruct((nh, rows, cols), F32),
        compiler_params=_params("arbitrary"),
        name="band_bias",
    )(table)


def _softmax_rows_base2(s2):
    m = jnp.max(s2, axis=-1, keepdims=True)
    p = jnp.exp2(s2 - m)
    return p / jnp.sum(p, axis=-1, keepdims=True)


def _band_prompt_kernel(q_ref, k_ref, v_ref, bias_ref, gq_ref, gk_ref, o_ref, kt_ref, vt_ref, kb_ref, vb_ref,
                        *, seq, rows, cols, keep):
    band = BAND_CHUNKS * CHUNK
    kn = _rms(k_ref[...], gk_ref[...])
    kt_ref[...] = kn[seq - keep:, :]
    vt_ref[...] = v_ref[pl.ds(seq - keep, keep), :]
    kb_ref[pl.ds(0, band), :] = jnp.zeros((band, D_HEAD), BF16)
    vb_ref[pl.ds(0, band), :] = jnp.zeros((band, D_HEAD), BF16)
    kb_ref[pl.ds(band, seq), :] = kn.astype(BF16)
    vb_ref[pl.ds(band, seq), :] = v_ref[...].astype(BF16)
    bias = bias_ref[0]
    gq = gq_ref[...]
    col = _iota((rows, cols), 1)

    def body(t, carry):
        r0 = pl.multiple_of(t * rows, rows)
        q = (_rms(q_ref[pl.ds(r0, rows), :], gq) * (SCALE * LOG2E)).astype(BF16)
        k = kb_ref[pl.ds(r0, cols), :]
        v = vb_ref[pl.ds(r0, cols), :]
        s = lax.dot_general(q, k, (((1,), (1,)), ((), ())), preferred_element_type=F32) + bias
        s = jnp.where(col >= band - r0, s, NEG)
        p = _softmax_rows_base2(s)
        o_ref[pl.ds(r0, rows), :] = jnp.dot(p.astype(BF16), v, preferred_element_type=F32).astype(o_ref.dtype)
        return carry

    lax.fori_loop(0, seq // rows, body, 0, unroll=BAND_UNROLL)


def band_attention_prompt(q, k, v, bias, g_q, g_k, nb, seq, keep):
    nh = q.shape[1] // D_HEAD
    rows, cols = bias.shape[1], bias.shape[2]
    band = BAND_CHUNKS * CHUNK
    blk = pl.BlockSpec((seq, D_HEAD), lambda b, h: (b, h))
    tail = pl.BlockSpec((keep, D_HEAD), lambda b, h: (b, h))
    gspec = pl.BlockSpec((1, D_HEAD), lambda b, h: (0, 0))
    return pl.pallas_call(
        functools.partial(_band_prompt_kernel, seq=seq, rows=rows, cols=cols, keep=keep),
        grid=(nb, nh),
        in_specs=[blk, blk, blk, pl.BlockSpec((1, rows, cols), lambda b, h: (h, 0, 0)), gspec, gspec],
        out_specs=[blk, tail, tail],
        out_shape=[jax.ShapeDtypeStruct(q.shape, BF16),
                   jax.ShapeDtypeStruct((nb * keep, q.shape[1]), F32),
                   jax.ShapeDtypeStruct((nb * keep, q.shape[1]), F32)],
        scratch_shapes=[pltpu.VMEM((band + seq, D_HEAD), BF16), pltpu.VMEM((band + seq, D_HEAD), BF16)],
        compiler_params=_params("arbitrary", "arbitrary"),
        name="band_attention_prompt",
    )(q, k, v, bias, g_q.reshape(1, D_HEAD), g_k.reshape(1, D_HEAD))


def _head_rows(ref, h):
    return ref[:, h, :]


def _band_sample_kernel(q_ref, kn_ref, vn_ref, kc_ref, vc_ref, bias_ref, gq_ref, gk_ref, o_ref, ko_ref,
                        *, ts, w, nh, cols):
    d = D_HEAD
    gq, gk = gq_ref[...], gk_ref[...]
    zeros = jnp.zeros((cols - w - ts, d), BF16)
    valid = _iota((ts, cols), 1) < w + ts
    for h in range(nh):
        hs = slice(h * d, (h + 1) * d)
        kn = _rms(kn_ref[:, hs], gk)
        ko_ref[:, hs] = kn
        k = jnp.concatenate([_head_rows(kc_ref, h).astype(BF16), kn.astype(BF16), zeros], axis=0)
        v = jnp.concatenate([_head_rows(vc_ref, h).astype(BF16), vn_ref[:, hs].astype(BF16), zeros], axis=0)
        q = (_rms(q_ref[:, hs], gq) * (SCALE * LOG2E)).astype(BF16)
        s = lax.dot_general(q, k, (((1,), (1,)), ((), ())), preferred_element_type=F32) + bias_ref[h]
        p = _softmax_rows_base2(jnp.where(valid, s, NEG))
        o_ref[:, hs] = jnp.dot(p.astype(BF16), v, preferred_element_type=F32).astype(o_ref.dtype)


def band_attention_sample(q, kn, vn, kc, vc, bias, g_q, g_k, nb, ts, w, nh):
    cols = -(-(w + ts) // LANES) * LANES
    assert w == BAND_CHUNKS * CHUNK and ts <= CHUNK and cols <= bias.shape[2] and ts % 16 == 0
    new_spec = pl.BlockSpec((ts, nh * D_HEAD), lambda b: (b, 0))
    cache_spec = pl.BlockSpec((None, w, nh, D_HEAD), lambda b: (b, 0, 0, 0))
    gspec = pl.BlockSpec((1, D_HEAD), lambda b: (0, 0))
    return pl.pallas_call(
        functools.partial(_band_sample_kernel, ts=ts, w=w, nh=nh, cols=cols),
        grid=(nb,),
        in_specs=[new_spec, new_spec, new_spec, cache_spec, cache_spec,
                  pl.BlockSpec((nh, ts, cols), lambda b: (0, 0, 0)), gspec, gspec],
        out_specs=[new_spec, new_spec],
        out_shape=[jax.ShapeDtypeStruct(q.shape, BF16), jax.ShapeDtypeStruct(q.shape, F32)],
        compiler_params=_params("arbitrary"),
        name="band_attention_sample",
    )(q, kn, vn, kc, vc, bias, g_q.reshape(1, D_HEAD), g_k.reshape(1, D_HEAD))


def _norm_rope(x, g, cosf, sina, sinb):
    half = ROT_DIM // 2
    y = _rms(x, g)
    return y * cosf + pltpu.roll(y, D_HEAD - half, 1) * sina + pltpu.roll(y, half, 1) * sinb


def rope_tables(pos):
    half = ROT_DIM // 2
    inv_freq = ROPE_THETA ** (-jnp.arange(half, dtype=F32) * (2.0 / ROT_DIM))
    ang = pos.astype(F32)[:, None] * inv_freq[None, :]
    cos, sin = jnp.cos(ang), jnp.sin(ang)
    n = pos.shape[0]
    rest = D_HEAD - ROT_DIM
    cosf = jnp.concatenate([cos, cos, jnp.ones((n, rest), F32)], axis=1)
    sina = jnp.concatenate([-sin, jnp.zeros((n, rest + half), F32)], axis=1)
    sinb = jnp.concatenate([jnp.zeros((n, half), F32), sin, jnp.zeros((n, rest), F32)], axis=1)
    return cosf, sina, sinb


def _online_block(q, k, v, mask, m, l, acc):
    s = lax.dot_general(q, k, (((1,), (1,)), ((), ())), preferred_element_type=F32)
    if mask is not None:
        s = jnp.where(mask, s, NEG)
    m_new = jnp.maximum(m, jnp.max(s, axis=-1, keepdims=True))
    alpha = jnp.exp2(m - m_new)
    p = jnp.exp2(s - m_new)
    l = alpha * l + jnp.sum(p, axis=-1, keepdims=True)
    acc = alpha * acc + jnp.dot(p.astype(BF16), v, preferred_element_type=F32)
    return m_new, l, acc


def _diff_lambda(lq1, lk1, lq2, lk2):
    return (jnp.exp(jnp.sum(lq1 * lk1, axis=-1, keepdims=True))
            - jnp.exp(jnp.sum(lq2 * lk2, axis=-1, keepdims=True)) + LAMBDA_INIT)


def _diff_finish(state, lam, gsub):
    (_, l1, a1), (_, l2, a2) = state
    o = a1 / l1 - lam * (a2 / l2)
    return _rms(o, gsub) * (1.0 - LAMBDA_INIT)


def _diff_init(tq):
    return (jnp.full((tq, 1), NEG, F32), jnp.zeros((tq, 1), F32), jnp.zeros((tq, 2 * D_HEAD), F32))


def _diff_prompt_kernel(q_ref, k_ref, v_ref, cos_ref, sa_ref, sb_ref, gq_ref, gk_ref, lam_ref, gsub_ref,
                        o_ref, ko_ref, k1_ref, k2_ref, vb_ref, *, t):
    qi = pl.program_id(2)
    d = D_HEAD

    @pl.when(qi == 0)
    def _():
        gk = gk_ref[...]
        cosf, sina, sinb = cos_ref[...], sa_ref[...], sb_ref[...]
        k1 = _norm_rope(k_ref[:, :d], gk, cosf, sina, sinb)
        k2 = _norm_rope(k_ref[:, d:], gk, cosf, sina, sinb)
        ko_ref[:, :d] = k1
        ko_ref[:, d:] = k2
        k1_ref[...] = k1.astype(BF16)
        k2_ref[...] = k2.astype(BF16)
        vb_ref[...] = v_ref[...].astype(BF16)

    r0 = pl.multiple_of(qi * t, t)
    gq = gq_ref[...]
    cosf, sina, sinb = cos_ref[pl.ds(r0, t), :], sa_ref[pl.ds(r0, t), :], sb_ref[pl.ds(r0, t), :]
    q1 = (_norm_rope(q_ref[:, :d], gq, cosf, sina, sinb) * (SCALE * LOG2E)).astype(BF16)
    q2 = (_norm_rope(q_ref[:, d:], gq, cosf, sina, sinb) * (SCALE * LOG2E)).astype(BF16)

    def step(k0, state, mask):
        v = vb_ref[pl.ds(k0, t), :]
        s1 = _online_block(q1, k1_ref[pl.ds(k0, t), :], v, mask, *state[0])
        s2 = _online_block(q2, k2_ref[pl.ds(k0, t), :], v, mask, *state[1])
        return s1, s2

    def body(kb, state):
        return step(pl.multiple_of(kb * t, t), state, None)

    state = lax.fori_loop(0, qi, body, (_diff_init(t), _diff_init(t)))
    mask = (_iota((t, t), 1) // CHUNK) <= (_iota((t, t), 0) // CHUNK)
    state = step(r0, state, mask)
    lam = _diff_lambda(lam_ref[0:1, :], lam_ref[1:2, :], lam_ref[2:3, :], lam_ref[3:4, :])
    o_ref[...] = _diff_finish(state, lam, gsub_ref[...]).astype(o_ref.dtype)


def diff_attention_prompt(q, k, v, tabs, g_q, g_k, lam4, g_sub, nb, seq, t=DIFF_T):
    d2 = 2 * D_HEAD
    nh = q.shape[1] // d2
    assert t % CHUNK == 0
    kv_spec = pl.BlockSpec((seq, d2), lambda b, h, qi: (b, h))
    q_spec = pl.BlockSpec((t, d2), lambda b, h, qi: (b * (seq // t) + qi, h))
    tab_spec = pl.BlockSpec((seq, D_HEAD), lambda b, h, qi: (0, 0))
    g_spec = pl.BlockSpec((1, D_HEAD), lambda b, h, qi: (0, 0))
    return pl.pallas_call(
        functools.partial(_diff_prompt_kernel, t=t),
        grid=(nb, nh, seq // t),
        in_specs=[q_spec, kv_spec, kv_spec, tab_spec, tab_spec, tab_spec, g_spec, g_spec,
                  pl.BlockSpec((4, D_HEAD), lambda b, h, qi: (0, 0)),
                  pl.BlockSpec((1, d2), lambda b, h, qi: (0, 0))],
        out_specs=[q_spec, kv_spec],
        out_shape=[jax.ShapeDtypeStruct(q.shape, BF16), jax.ShapeDtypeStruct(k.shape, F32)],
        scratch_shapes=[pltpu.VMEM((seq, D_HEAD), BF16), pltpu.VMEM((seq, D_HEAD), BF16),
                        pltpu.VMEM((seq, d2), BF16)],
        compiler_params=_params("arbitrary", "arbitrary", "arbitrary"),
        name="diff_attention_prompt",
    )(q, k, v, *tabs, g_q.reshape(1, D_HEAD), g_k.reshape(1, D_HEAD), lam4, g_sub.reshape(1, d2))


def _diff_sample_kernel(q_ref, kn_ref, vn_ref, kc_hbm, vc_hbm, cos_ref, sa_ref, sb_ref, gq_ref, gk_ref,
                        lam_ref, gsub_ref, o_ref, ko_ref, kbuf, vbuf, ksem, vsem, k1_ref, k2_ref, vb_ref,
                        *, ts, past, tn, hg):
    slot = _prefetch_heads((kc_hbm, vc_hbm), (kbuf, vbuf), (ksem, vsem), hg)
    d = D_HEAD
    cosf, sina, sinb = cos_ref[...], sa_ref[...], sb_ref[...]
    gq, gk = gq_ref[...], gk_ref[...]
    lam = _diff_lambda(lam_ref[0:1, :], lam_ref[1:2, :], lam_ref[2:3, :], lam_ref[3:4, :])
    gsub = gsub_ref[...]
    nk = past + tn
    kpos = _iota((ts, nk), 1)
    qpos = past + _iota((ts, nk), 0)
    mask = ((kpos // CHUNK) <= (qpos // CHUNK)) & (kpos < past + ts)
    for g in range(hg):
        c1, c2 = slice(2 * g * d, (2 * g + 1) * d), slice((2 * g + 1) * d, (2 * g + 2) * d)
        cv = slice(2 * g * d, (2 * g + 2) * d)
        k1 = _norm_rope(kn_ref[:, c1], gk, cosf, sina, sinb)
        k2 = _norm_rope(kn_ref[:, c2], gk, cosf, sina, sinb)
        ko_ref[:, c1] = k1
        ko_ref[:, c2] = k2
        k1_ref[g, pl.ds(0, past), :] = kbuf[slot, g, :, :d].astype(BF16)
        k2_ref[g, pl.ds(0, past), :] = kbuf[slot, g, :, d:].astype(BF16)
        vb_ref[g, pl.ds(0, past), :] = vbuf[slot, g].astype(BF16)
        k1_ref[g, pl.ds(past, tn), :] = jnp.zeros((tn, d), BF16)
        k2_ref[g, pl.ds(past, tn), :] = jnp.zeros((tn, d), BF16)
        vb_ref[g, pl.ds(past, tn), :] = jnp.zeros((tn, 2 * d), BF16)
        k1_ref[g, pl.ds(past, ts), :] = k1.astype(BF16)
        k2_ref[g, pl.ds(past, ts), :] = k2.astype(BF16)
        vb_ref[g, pl.ds(past, ts), :] = vn_ref[:, cv].astype(BF16)
        q1 = (_norm_rope(q_ref[:, c1], gq, cosf, sina, sinb) * (SCALE * LOG2E)).astype(BF16)
        q2 = (_norm_rope(q_ref[:, c2], gq, cosf, sina, sinb) * (SCALE * LOG2E)).astype(BF16)
        v = vb_ref[g]
        state = (_online_block(q1, k1_ref[g], v, mask, *_diff_init(ts)),
                 _online_block(q2, k2_ref[g], v, mask, *_diff_init(ts)))
        o_ref[:, cv] = _diff_finish(state, lam, gsub).astype(o_ref.dtype)


def diff_attention_sample(q, kn, vn, kc, vc, tabs, g_q, g_k, lam4, g_sub, nb, ts, past, nh, tn=LANES,
                          hg=SAMPLE_HEAD_GROUP):
    d2 = 2 * D_HEAD
    assert past % LANES == 0 and ts <= tn and nh % hg == 0
    new_spec = pl.BlockSpec((ts, hg * d2), lambda b, h: (b, h))
    hbm = pl.BlockSpec(memory_space=pl.ANY)
    tab_spec = pl.BlockSpec((ts, D_HEAD), lambda b, h: (0, 0))
    g_spec = pl.BlockSpec((1, D_HEAD), lambda b, h: (0, 0))
    return pl.pallas_call(
        functools.partial(_diff_sample_kernel, ts=ts, past=past, tn=tn, hg=hg),
        grid=(nb, nh // hg),
        in_specs=[new_spec, new_spec, new_spec, hbm, hbm, tab_spec, tab_spec, tab_spec,
                  g_spec, g_spec, pl.BlockSpec((4, D_HEAD), lambda b, h: (0, 0)),
                  pl.BlockSpec((1, d2), lambda b, h: (0, 0))],
        out_specs=[new_spec, new_spec],
        out_shape=[jax.ShapeDtypeStruct(q.shape, BF16), jax.ShapeDtypeStruct(q.shape, F32)],
        scratch_shapes=[pltpu.VMEM((2, hg, past, d2), F32), pltpu.VMEM((2, hg, past, d2), F32),
                        pltpu.SemaphoreType.DMA((2, hg)), pltpu.SemaphoreType.DMA((2, hg)),
                        pltpu.VMEM((hg, past + tn, D_HEAD), BF16), pltpu.VMEM((hg, past + tn, D_HEAD), BF16),
                        pltpu.VMEM((hg, past + tn, d2), BF16)],
        compiler_params=_params("arbitrary", "arbitrary"),
        name="diff_attention_sample",
    )(q, kn, vn, kc, vc, *tabs, g_q.reshape(1, D_HEAD), g_k.reshape(1, D_HEAD), lam4, g_sub.reshape(1, d2))


def _to_time_major(x, nb, ts):
    return x.reshape(nb, ts, -1).transpose(1, 0, 2).reshape(nb * ts, -1)


def _to_batch_major(x, nb, ts):
    return x.reshape(ts, nb, -1).transpose(1, 0, 2).reshape(nb * ts, -1)


def _conv_glu_layer(hp, hs, g_norm, w_up, w_conv, w_down_bf16, layer, state, nb_s, ts, seq):
    mp = hp.shape[0]
    hs_t = _to_time_major(hs, nb_s, ts)
    xn = rmsnorm_rows(hp, hs_t, g_norm)
    hidden, conv_p, conv_s_t = up_convglu(xn, w_up, w_conv, layer, state.transpose(1, 0, 2), mp, seq)
    hp, hs_t = down_project(hidden, w_down_bf16, layer, hp, hs_t)
    return hp, _to_batch_major(hs_t, nb_s, ts), conv_p, conv_s_t.transpose(1, 0, 2)


def kernel(x_prompt, x_sample, cache_k_sb, cache_v_sb, cache_k_band, cache_v_band, cache_k_diff, cache_v_diff, state_conv_ffn, norm_mix, norm_ffn, w_in_ab, w_out_ab, g_q_band, g_k_band, rel_bias_band, w_in_diff, w_out_diff, g_q_diff, g_k_diff, lambda_q1, lambda_k1, lambda_q2, lambda_k2, g_sub_diff, w_up, w_conv, w_down):
    bp, tp, dm = x_prompt.shape
    bs, ts, _ = x_sample.shape
    past = cache_k_sb.shape[1]
    h_sb, h_cb, h_df = cache_k_sb.shape[2], cache_k_band.shape[2], cache_k_diff.shape[2]
    d_sb, d_cb, d_df = h_sb * D_HEAD, h_cb * D_HEAD, h_df * 2 * D_HEAD
    w_band = cache_k_band.shape[1]
    keep = min(BAND_CHUNKS * CHUNK, tp)
    mp, ms = bp * tp, bs * ts

    hp = x_prompt.reshape(mp, dm)
    hs = x_sample.reshape(ms, dm)

    xn = rmsnorm_rows(hp, hs, norm_mix[0])
    sec = []
    col = 0
    for width in (d_sb, d_sb, d_sb, d_cb, d_cb, d_cb):
        sec.append(project(xn, w_in_ab, col, width, mp))
        col += width
    (qa_p, qa_s), (ka_p, ka_s), (va_p, va_s), (qb_p, qb_s), (kb_p, kb_s), (vb_p, vb_s) = sec

    oa_p = sb_attention_prompt(qa_p, ka_p, va_p, bp, tp)
    oa_s = sb_attention_sample(qa_s, ka_s, va_s, cache_k_sb, cache_v_sb, bs, ts, past, h_sb)
    bias = band_bias(rel_bias_band, BAND_QCHUNKS)
    ob_p, k_band_p, v_band_p = band_attention_prompt(qb_p, kb_p, vb_p, bias, g_q_band, g_k_band, bp, tp, keep)
    ob_s, k_band_s = band_attention_sample(qb_s, kb_s, vb_s, cache_k_band, cache_v_band, bias,
                                           g_q_band, g_k_band, bs, ts, w_band, h_cb)
    hp, hs = out_project([oa_p, ob_p], [oa_s, ob_s], w_out_ab, hp, hs)
    w_down_bf16 = w_down.astype(BF16)
    hp, hs, conv_p0, conv_s0 = _conv_glu_layer(hp, hs, norm_ffn[0], w_up, w_conv, w_down_bf16, 0,
                                               state_conv_ffn[0], bs, ts, tp)

    xn = rmsnorm_rows(hp, hs, norm_mix[1])
    (q_p, q_s), (k_p, k_s), (v_p, v_s) = [project(xn, w_in_diff, c * d_df, d_df, mp) for c in range(3)]
    lam4 = jnp.stack([lambda_q1, lambda_k1, lambda_q2, lambda_k2]).astype(F32)
    o_p, k_diff_p = diff_attention_prompt(q_p, k_p, v_p, rope_tables(jnp.arange(tp)), g_q_diff, g_k_diff,
                                          lam4, g_sub_diff, bp, tp)
    o_s, k_diff_s = diff_attention_sample(q_s, k_s, v_s, cache_k_diff, cache_v_diff,
                                          rope_tables(past + jnp.arange(ts)), g_q_diff, g_k_diff,
                                          lam4, g_sub_diff, bs, ts, past, h_df)
    hp, hs = out_project([o_p], [o_s], w_out_diff, hp, hs)
    hp, hs, conv_p1, conv_s1 = _conv_glu_layer(hp, hs, norm_ffn[1], w_up, w_conv, w_down_bf16, 1,
                                               state_conv_ffn[1], bs, ts, tp)

    return (hp.reshape(bp, tp, dm), hs.reshape(bs, ts, dm),
            ka_p.reshape(bp, tp, h_sb, D_HEAD), va_p.reshape(bp, tp, h_sb, D_HEAD),
            k_band_p.reshape(bp, keep, h_cb, D_HEAD), v_band_p.reshape(bp, keep, h_cb, D_HEAD),
            k_diff_p.reshape(bp, tp, h_df, 2 * D_HEAD), v_p.reshape(bp, tp, h_df, 2 * D_HEAD),
            jnp.stack([conv_p0, conv_p1]),
            ka_s.reshape(bs, ts, h_sb, D_HEAD), va_s.reshape(bs, ts, h_sb, D_HEAD),
            k_band_s.reshape(bs, ts, h_cb, D_HEAD), vb_s.reshape(bs, ts, h_cb, D_HEAD),
            k_diff_s.reshape(bs, ts, h_df, 2 * D_HEAD), v_s.reshape(bs, ts, h_df, 2 * D_HEAD),
            jnp.stack([conv_s0, conv_s1]))
```

```python
import functools
import math

import jax
import jax.numpy as jnp
from jax import lax
from jax.experimental import pallas as pl
from jax.experimental.pallas import tpu as pltpu

F32 = jnp.float32
BF16 = jnp.bfloat16

CHUNK = 64
D_HEAD = 128
BAND_CHUNKS = 8
REL_CLIP = 128
ROT_DIM = D_HEAD // 4
ROPE_THETA = 500000.0
CONV_W = 3
EPS = 1e-6
NEG = -1e30
LAMBDA_INIT = 0.8 - 0.6 * math.exp(-0.3 * 1)
SCALE = 1.0 / math.sqrt(D_HEAD)
LOG2E = math.log2(math.e)

V7X_VMEM_LIMIT_BYTES = 56 * 1024 * 1024
LANES = 128

ROW_TILE = 1024
PROJ_ROW_TILE = 512
PROJ_COL_TILE = 1024
OUT_ROW_TILE = 512
UP_COL_BLOCK = 256
UP_ROW_CHUNK = 512
DOWN_ROW_TILE = 256
SB_T = 512
SB_CB = 256
BAND_QCHUNKS = 4
BAND_UNROLL = 4
SAMPLE_HEAD_GROUP = 4
DIFF_T = 512


def _params(*sem):
    return pltpu.CompilerParams(dimension_semantics=sem, vmem_limit_bytes=V7X_VMEM_LIMIT_BYTES)


def _iota(shape, dim):
    return lax.broadcasted_iota(jnp.int32, shape, dim)


def _rms(x, g):
    return x * lax.rsqrt(jnp.mean(x * x, axis=-1, keepdims=True) + EPS) * g


def _rmsnorm_kernel(xp_ref, xs_ref, g_ref, o_ref, *, np_tiles):
    i = pl.program_id(0)

    @pl.when(i < np_tiles)
    def _():
        o_ref[...] = _rms(xp_ref[...], g_ref[...]).astype(o_ref.dtype)

    @pl.when(i >= np_tiles)
    def _():
        o_ref[...] = _rms(xs_ref[...], g_ref[...]).astype(o_ref.dtype)


def rmsnorm_rows(xp, xs, g, tm=256):
    mp, d = xp.shape
    ms = xs.shape[0]
    np_tiles, ns_tiles = mp // tm, ms // tm
    return pl.pallas_call(
        functools.partial(_rmsnorm_kernel, np_tiles=np_tiles),
        grid=(np_tiles + ns_tiles,),
        in_specs=[
            pl.BlockSpec((tm, d), lambda i: (jnp.minimum(i, np_tiles - 1), 0)),
            pl.BlockSpec((tm, d), lambda i: (jnp.maximum(i - np_tiles, 0), 0)),
            pl.BlockSpec((1, d), lambda i: (0, 0)),
        ],
        out_specs=pl.BlockSpec((tm, d), lambda i: (i, 0)),
        out_shape=jax.ShapeDtypeStruct((mp + ms, d), BF16),
        compiler_params=_params("arbitrary"),
        name="rmsnorm_rows",
    )(xp, xs, g.reshape(1, d))


def _refresh_weights(copies, wf_ref, wb_ref):
    jj, i = pl.program_id(0), pl.program_id(1)
    tb = wf_ref.shape[2]

    @pl.when(i == 0)
    def _():
        @pl.when(jj == 0)
        def _():
            for cp in copies(jj):
                cp.start()

        for cp in copies(jj):
            cp.wait()
        for n in range(wf_ref.shape[0]):
            wb_ref[:, n * tb:(n + 1) * tb] = wf_ref[n].astype(BF16)

        @pl.when(jj + 1 < pl.num_programs(0))
        def _():
            for cp in copies(jj + 1):
                cp.start()


def _proj_kernel(x_ref, w_hbm, op_ref, os_ref, wf_ref, wsem, wb_ref, *, col0, np_tiles, ms):
    i = pl.program_id(1)
    tn = wf_ref.shape[2]

    def copies(jj):
        c0 = pl.multiple_of(col0 + jj * tn, tn)
        return [pltpu.make_async_copy(w_hbm.at[:, pl.ds(c0, tn)], wf_ref.at[0], wsem.at[0])]

    _refresh_weights(copies, wf_ref, wb_ref)

    @pl.when(i < np_tiles)
    def _():
        op_ref[...] = jnp.dot(x_ref[...], wb_ref[...], preferred_element_type=F32)

    @pl.when(i >= np_tiles)
    def _():
        os_ref[...] = jnp.dot(x_ref[0:ms, :], wb_ref[...], preferred_element_type=F32)


def project(x, w, col0, n, mp, tm=PROJ_ROW_TILE, tn=PROJ_COL_TILE):
    m, k = x.shape
    ms = m - mp
    assert mp % tm == 0 and 0 < ms <= tm and n % tn == 0 and col0 % tn == 0
    np_tiles = mp // tm
    return pl.pallas_call(
        functools.partial(_proj_kernel, col0=col0, np_tiles=np_tiles, ms=ms),
        grid=(n // tn, np_tiles + 1),
        in_specs=[
            pl.BlockSpec((tm, k), lambda j, i: (i, 0)),
            pl.BlockSpec(memory_space=pl.ANY),
        ],
        out_specs=[
            pl.BlockSpec((tm, tn), lambda j, i: (jnp.minimum(i, np_tiles - 1), j)),
            pl.BlockSpec((ms, tn), lambda j, i: (0, j)),
        ],
        out_shape=[jax.ShapeDtypeStruct((mp, n), F32), jax.ShapeDtypeStruct((ms, n), F32)],
        scratch_shapes=[pltpu.VMEM((1, k, tn), F32), pltpu.SemaphoreType.DMA((1,)), pltpu.VMEM((k, tn), BF16)],
        compiler_params=_params("arbitrary", "arbitrary"),
        name="project",
    )(x, w)


def _outproj_kernel(*refs, np_tiles, nparts):
    ap = refs[0:nparts]
    a_s = refs[nparts:2 * nparts]
    w = refs[2 * nparts:3 * nparts]
    rp_ref, rs_ref, op_ref, os_ref, wb_ref = refs[3 * nparts:]
    i = pl.program_id(1)

    @pl.when(i == 0)
    def _():
        for p in range(nparts):
            wb_ref[p] = w[p][...].astype(BF16)

    def run(a_refs, r_ref, o_ref):
        acc = r_ref[...]
        for p in range(nparts):
            acc = acc + jnp.dot(a_refs[p][...], wb_ref[p], preferred_element_type=F32)
        o_ref[...] = acc

    @pl.when(i < np_tiles)
    def _():
        run(ap, rp_ref, op_ref)

    @pl.when(i >= np_tiles)
    def _():
        run(a_s, rs_ref, os_ref)


def out_project(parts_p, parts_s, w, res_p, res_s, tm=OUT_ROW_TILE, tn=512):
    nparts = len(parts_p)
    mp, kp = parts_p[0].shape
    ms = parts_s[0].shape[0]
    n = w.shape[1]
    np_tiles, ns_tiles = mp // tm, ms // tm
    pmap = lambda j, i: (jnp.minimum(i, np_tiles - 1), 0)
    smap = lambda j, i: (jnp.maximum(i - np_tiles, 0), 0)
    pmap_o = lambda j, i: (jnp.minimum(i, np_tiles - 1), j)
    smap_o = lambda j, i: (jnp.maximum(i - np_tiles, 0), j)
    in_specs = ([pl.BlockSpec((tm, kp), pmap)] * nparts + [pl.BlockSpec((tm, kp), smap)] * nparts
                + [pl.BlockSpec((kp, tn), functools.partial(lambda j, i, p: (p, j), p=p)) for p in range(nparts)]
                + [pl.BlockSpec((tm, tn), pmap_o), pl.BlockSpec((tm, tn), smap_o)])
    return pl.pallas_call(
        functools.partial(_outproj_kernel, np_tiles=np_tiles, nparts=nparts),
        grid=(n // tn, np_tiles + ns_tiles),
        in_specs=in_specs,
        out_specs=[pl.BlockSpec((tm, tn), pmap_o), pl.BlockSpec((tm, tn), smap_o)],
        out_shape=[jax.ShapeDtypeStruct((mp, n), F32), jax.ShapeDtypeStruct((ms, n), F32)],
        scratch_shapes=[pltpu.VMEM((nparts, kp, tn), BF16)],
        compiler_params=_params("arbitrary", "arbitrary"),
        name="out_project",
    )(*parts_p, *parts_s, *([w] * nparts), res_p, res_s)


def _silu_mul(gc, u):
    return gc * (1.0 / (1.0 + jnp.exp(-gc))) * u


def _up_weight_copies(w_hbm, wf_ref, wsem, jj, *, layer, nj, tb):
    j0 = 2 * jj
    j1 = jnp.minimum(j0 + 1, nj - 1)
    cols = (j0, j1, nj + j0, nj + j1)
    return [pltpu.make_async_copy(w_hbm.at[layer, :, pl.ds(pl.multiple_of(c * tb, tb), tb)], wf_ref.at[n], wsem.at[n])
            for n, c in enumerate(cols)]


def _up_kernel(x_ref, w_hbm, wc_ref, hist_ref, h_ref, cp_ref, cs_ref, wf_ref, wsem, wb_ref, carry_ref,
               *, layer, nj, tb, np_tiles, tiles_per_seq, nb_s, ms, rc):
    i = pl.program_id(1)
    tm = x_ref.shape[0]
    tn = 2 * tb
    _refresh_weights(functools.partial(_up_weight_copies, w_hbm, wf_ref, wsem, layer=layer, nj=nj, tb=tb),
                     wf_ref, wb_ref)

    w0 = wc_ref[0:1, :]
    w1 = wc_ref[1:2, :]
    w2 = wc_ref[2:3, :]

    @pl.when(lax.rem(i, tiles_per_seq) == 0)
    def _():
        carry_ref[...] = jnp.zeros_like(carry_ref)

    @pl.when(i < np_tiles)
    def _():
        c0 = carry_ref[0:1, :]
        c1 = carry_ref[1:2, :]
        row = _iota((rc, tn), 0)
        for r in range(tm // rc):
            rows = slice(r * rc, (r + 1) * rc)
            gu = jnp.dot(x_ref[rows, :], wb_ref[...], preferred_element_type=F32)
            g = gu[:, :tn]
            u = gu[:, tn:]
            gm1 = jnp.where(row == 0, c1, pltpu.roll(g, 1, 0))
            gm2 = jnp.where(row == 0, c0, jnp.where(row == 1, c1, pltpu.roll(g, 2, 0)))
            gc = gm2 * w0 + gm1 * w1 + g * w2
            h_ref[rows, :] = _silu_mul(gc, u).astype(h_ref.dtype)
            c0 = g[rc - 2:rc - 1, :]
            c1 = g[rc - 1:rc, :]
        tail = jnp.concatenate([c0, c1], axis=0)
        carry_ref[0:2, :] = tail
        cp_ref[0] = tail

    @pl.when(i >= np_tiles)
    def _():
        gu = jnp.dot(x_ref[0:ms, :], wb_ref[...], preferred_element_type=F32)
        g = gu[:, :tn]
        u = gu[:, tn:]
        h0 = hist_ref[0]
        h1 = hist_ref[1]
        gm1 = jnp.concatenate([h1, g[:ms - nb_s]], axis=0)
        gm2 = jnp.concatenate([h0, h1, g[:ms - 2 * nb_s]], axis=0)
        gc = gm2 * w0 + gm1 * w1 + g * w2
        h_ref[0:ms, :] = _silu_mul(gc, u).astype(h_ref.dtype)
        cs_ref[0] = g[ms - 2 * nb_s:ms - nb_s]
        cs_ref[1] = g[ms - nb_s:]


def up_convglu(x, w_up, w_conv, layer, hist_t, mp, seq_len, tm=ROW_TILE, tb=UP_COL_BLOCK, rc=UP_ROW_CHUNK):
    m, k = x.shape
    f = w_up.shape[2] // 2
    nb_s = hist_t.shape[1]
    nb_p = mp // seq_len
    np_tiles = mp // tm
    ms = m - mp
    assert 0 < ms <= tm and seq_len % tm == 0 and tm % rc == 0 and f % tb == 0 and CONV_W == 3
    tiles_per_seq = seq_len // tm
    nj = f // tb
    tn = 2 * tb
    njj = -(-nj // 2)
    return pl.pallas_call(
        functools.partial(_up_kernel, layer=layer, nj=nj, tb=tb, np_tiles=np_tiles, tiles_per_seq=tiles_per_seq,
                          nb_s=nb_s, ms=ms, rc=rc),
        grid=(njj, np_tiles + 1),
        in_specs=[
            pl.BlockSpec((tm, k), lambda j, i: (i, 0)),
            pl.BlockSpec(memory_space=pl.ANY),
            pl.BlockSpec((None, CONV_W, tn), lambda j, i: (layer, 0, j)),
            pl.BlockSpec((CONV_W - 1, nb_s, tn), lambda j, i: (0, 0, j)),
        ],
        out_specs=[
            pl.BlockSpec((tm, tn), lambda j, i: (i, j)),
            pl.BlockSpec((1, CONV_W - 1, tn), lambda j, i: (jnp.minimum(i // tiles_per_seq, nb_p - 1), 0, j)),
            pl.BlockSpec((CONV_W - 1, nb_s, tn), lambda j, i: (0, 0, j)),
        ],
        out_shape=[
            jax.ShapeDtypeStruct((m, f), BF16),
            jax.ShapeDtypeStruct((nb_p, CONV_W - 1, f), F32),
            jax.ShapeDtypeStruct((CONV_W - 1, nb_s, f), F32),
        ],
        scratch_shapes=[pltpu.VMEM((4, k, tb), F32), pltpu.SemaphoreType.DMA((4,)),
                        pltpu.VMEM((k, 2 * tn), BF16), pltpu.VMEM((8, tn), F32)],
        compiler_params=_params("arbitrary", "arbitrary"),
        name="up_convglu",
    )(x, w_up, w_conv, hist_t)


def _down_kernel(a_ref, w_ref, rp_ref, rs_ref, op_ref, os_ref, *, np_tiles):
    i = pl.program_id(1)

    @pl.when(i < np_tiles)
    def _():
        op_ref[...] = rp_ref[...] + jnp.dot(a_ref[...], w_ref[...], preferred_element_type=F32)

    @pl.when(i >= np_tiles)
    def _():
        os_ref[...] = rs_ref[...] + jnp.dot(a_ref[...], w_ref[...], preferred_element_type=F32)


def down_project(a, w, layer, res_p, res_s, tm=DOWN_ROW_TILE, tn=512):
    m, f = a.shape
    n = w.shape[2]
    mp = res_p.shape[0]
    ms = m - mp
    np_tiles, ns_tiles = mp // tm, ms // tm
    pmap_o = lambda j, i: (jnp.minimum(i, np_tiles - 1), j)
    smap_o = lambda j, i: (jnp.maximum(i - np_tiles, 0), j)
    return pl.pallas_call(
        functools.partial(_down_kernel, np_tiles=np_tiles),
        grid=(n // tn, np_tiles + ns_tiles),
        in_specs=[
            pl.BlockSpec((tm, f), lambda j, i: (i, 0)),
            pl.BlockSpec((None, f, tn), lambda j, i: (layer, 0, j)),
            pl.BlockSpec((tm, tn), pmap_o),
            pl.BlockSpec((tm, tn), smap_o),
        ],
        out_specs=[pl.BlockSpec((tm, tn), pmap_o), pl.BlockSpec((tm, tn), smap_o)],
        out_shape=[jax.ShapeDtypeStruct((mp, n), F32), jax.ShapeDtypeStruct((ms, n), F32)],
        compiler_params=_params("arbitrary", "arbitrary"),
        name="down_project",
    )(a, w, res_p, res_s)


def _sb_group(q, k, v, uu, run, acc, vis):
    nk = k.shape[0]
    cb = uu.shape[1]
    z2 = lax.dot_general(q, k, (((1,), (1,)), ((), ())), preferred_element_type=F32)
    s = jnp.maximum(z2, 0.0) + jnp.log2(1.0 + jnp.exp2(-jnp.abs(z2)))
    if vis is not None:
        s = jnp.where(vis, s, 0.0)
    hi = s.astype(BF16)
    lo = (s - hi.astype(F32)).astype(BF16)
    afters = []
    for blk in reversed(range(nk // cb)):
        sl = slice(blk * cb, (blk + 1) * cb)
        cs = jnp.dot(jnp.concatenate([hi[:, sl], lo[:, sl]], axis=1), uu, preferred_element_type=F32)
        afters.append(cs + run)
        run = run + jnp.sum(s[:, sl], axis=-1, keepdims=True)
    after = jnp.concatenate(afters[::-1], axis=1)
    w = jnp.exp2(z2 - s - after)
    if vis is not None:
        w = jnp.where(vis, w, 0.0)
    acc = acc + jnp.dot(w.astype(BF16), v, preferred_element_type=F32)
    return run, acc


def _sb_prompt_kernel(q_ref, k_ref, v_ref, uu_ref, o_ref, kb_ref, vb_ref, *, t):
    qi = pl.program_id(2)

    @pl.when(qi == 0)
    def _():
        kb_ref[...] = k_ref[...].astype(BF16)
        vb_ref[...] = v_ref[...].astype(BF16)

    q = (q_ref[...] * (SCALE * LOG2E)).astype(BF16)
    uu = uu_ref[...]
    r0 = pl.multiple_of(qi * t, t)
    run = jnp.zeros((t, 1), F32)
    acc = jnp.zeros((t, D_HEAD), F32)
    vis = _iota((t, t), 1) < _iota((t, t), 0)
    run, acc = _sb_group(q, kb_ref[pl.ds(r0, t), :], vb_ref[pl.ds(r0, t), :], uu, run, acc, vis)

    def body(g, carry):
        k0 = pl.multiple_of((qi - 1 - g) * t, t)
        return _sb_group(q, kb_ref[pl.ds(k0, t), :], vb_ref[pl.ds(k0, t), :], uu, carry[0], carry[1], None)

    run, acc = lax.fori_loop(0, qi, body, (run, acc))
    o_ref[...] = acc.astype(o_ref.dtype)


def _suffix_matrix(cb):
    j = jnp.arange(cb)[:, None]
    s = jnp.arange(cb)[None, :]
    u = (j > s).astype(BF16)
    return jnp.concatenate([u, u], axis=0)


def sb_attention_prompt(q, k, v, nb, seq, t=SB_T, cb=SB_CB):
    nh = q.shape[1] // D_HEAD
    kv_spec = pl.BlockSpec((seq, D_HEAD), lambda b, h, qi: (b, h))
    q_spec = pl.BlockSpec((t, D_HEAD), lambda b, h, qi: (b * (seq // t) + qi, h))
    return pl.pallas_call(
        functools.partial(_sb_prompt_kernel, t=t),
        grid=(nb, nh, seq // t),
        in_specs=[q_spec, kv_spec, kv_spec, pl.BlockSpec((2 * cb, cb), lambda b, h, qi: (0, 0))],
        out_specs=q_spec,
        out_shape=jax.ShapeDtypeStruct(q.shape, BF16),
        scratch_shapes=[pltpu.VMEM((seq, D_HEAD), BF16), pltpu.VMEM((seq, D_HEAD), BF16)],
        compiler_params=_params("arbitrary", "arbitrary", "arbitrary"),
        name="sb_attention_prompt",
    )(q, k, v, _suffix_matrix(cb))


def _head_copy(cache_hbm, buf_ref, sem_ref, b, h, slot, g):
    return pltpu.make_async_copy(cache_hbm.at[b, :, h, :], buf_ref.at[slot, g], sem_ref.at[slot, g])


def _prefetch_heads(caches, bufs, sems, hg):
    b, hh = pl.program_id(0), pl.program_id(1)
    nb, ng = pl.num_programs(0), pl.num_programs(1)
    step = b * ng + hh
    slot = lax.rem(step, 2)
    wrap = hh + 1 == ng
    b1 = jnp.where(wrap, b + 1, b)
    hh1 = jnp.where(wrap, 0, hh + 1)

    def copies(bb, hgrp, sl):
        return [_head_copy(c, buf, sem, bb, hgrp * hg + g, sl, g)
                for c, buf, sem in zip(caches, bufs, sems) for g in range(hg)]

    @pl.when(step == 0)
    def _():
        for cp in copies(b, hh, slot):
            cp.start()

    @pl.when(step + 1 < nb * ng)
    def _():
        for cp in copies(b1, hh1, 1 - slot):
            cp.start()

    for cp in copies(b, hh, slot):
        cp.wait()
    return slot


def _sb_sample_kernel(q_ref, kn_ref, vn_ref, kc_hbm, vc_hbm, uu_ref, o_ref, kbuf, vbuf, ksem, vsem, kb_ref, vb_ref,
                      *, ts, past, pad, hg):
    slot = _prefetch_heads((kc_hbm, vc_hbm), (kbuf, vbuf), (ksem, vsem), hg)
    d = D_HEAD
    nk = past + pad
    vis = _iota((ts, nk), 1) < past + _iota((ts, nk), 0)
    uu = uu_ref[...]
    for g in range(hg):
        cols = slice(g * d, (g + 1) * d)
        kb_ref[g, pl.ds(0, past), :] = kbuf[slot, g].astype(BF16)
        vb_ref[g, pl.ds(0, past), :] = vbuf[slot, g].astype(BF16)
        kb_ref[g, pl.ds(past, pad), :] = jnp.zeros((pad, d), BF16)
        vb_ref[g, pl.ds(past, pad), :] = jnp.zeros((pad, d), BF16)
        kb_ref[g, pl.ds(past, ts), :] = kn_ref[:, cols].astype(BF16)
        vb_ref[g, pl.ds(past, ts), :] = vn_ref[:, cols].astype(BF16)
        _, acc = _sb_group((q_ref[:, cols] * (SCALE * LOG2E)).astype(BF16), kb_ref[g], vb_ref[g], uu,
                           jnp.zeros((ts, 1), F32), jnp.zeros((ts, d), F32), vis)
        o_ref[:, cols] = acc.astype(o_ref.dtype)


def sb_attention_sample(q, kn, vn, kc, vc, nb, ts, past, nh, cb=SB_CB, hg=SAMPLE_HEAD_GROUP):
    assert past % cb == 0 and ts <= cb and nh % hg == 0
    new_spec = pl.BlockSpec((ts, hg * D_HEAD), lambda b, h: (b, h))
    hbm = pl.BlockSpec(memory_space=pl.ANY)
    return pl.pallas_call(
        functools.partial(_sb_sample_kernel, ts=ts, past=past, pad=cb, hg=hg),
        grid=(nb, nh // hg),
        in_specs=[new_spec, new_spec, new_spec, hbm, hbm, pl.BlockSpec((2 * cb, cb), lambda b, h: (0, 0))],
        out_specs=new_spec,
        out_shape=jax.ShapeDtypeStruct(q.shape, BF16),
        scratch_shapes=[pltpu.VMEM((2, hg, past, D_HEAD), F32), pltpu.VMEM((2, hg, past, D_HEAD), F32),
                        pltpu.SemaphoreType.DMA((2, hg)), pltpu.SemaphoreType.DMA((2, hg)),
                        pltpu.VMEM((hg, past + cb, D_HEAD), BF16), pltpu.VMEM((hg, past + cb, D_HEAD), BF16)],
        compiler_params=_params("arbitrary", "arbitrary"),
        name="sb_attention_sample",
    )(q, kn, vn, kc, vc, _suffix_matrix(cb))


def _band_bias_kernel(tab_ref, o_ref, *, rows, cols, width):
    h = pl.program_id(0)
    band = BAND_CHUNKS * CHUNK
    x = _iota((8, width), 1)
    idx = jnp.clip(x - rows - band, -REL_CLIP, REL_CLIP) + REL_CLIP

    def body(r, g):
        return jnp.where(idx == r, tab_ref[h, r] * LOG2E, g)

    g = lax.fori_loop(0, 2 * REL_CLIP + 1, body, jnp.zeros((8, width), F32))
    gt = jnp.broadcast_to(g[0:1, :], (rows, width))
    gt = pltpu.roll(gt, 0, 1, stride=1, stride_axis=0)
    bias = gt[:, rows:rows + cols]
    dc = _iota((rows, cols), 1) // CHUNK - _iota((rows, cols), 0) // CHUNK
    o_ref[0] = jnp.where((dc >= 0) & (dc <= BAND_CHUNKS), bias, NEG)


def band_bias(table, qchunks):
    nh = table.shape[0]
    rows = qchunks * CHUNK
    cols = (qchunks + BAND_CHUNKS) * CHUNK
    width = rows + cols
    assert width % LANES == 0 and rows % LANES == 0
    return pl.pallas_call(
        functools.partial(_band_bias_kernel, rows=rows, cols=cols, width=width),
        grid=(nh,),
        in_specs=[pl.BlockSpec(memory_space=pltpu.SMEM)],
        out_specs=pl.BlockSpec((1, rows, cols), lambda h: (h, 0, 0)),
        out_shape=jax.ShapeDtypeStruct((nh, rows, cols), F32),
        compiler_params=_params("arbitrary"),
        name="band_bias",
    )(table)


def _softmax_rows_base2(s2):
    m = jnp.max(s2, axis=-1, keepdims=True)
    p = jnp.exp2(s2 - m)
    return p / jnp.sum(p, axis=-1, keepdims=True)


def _band_prompt_kernel(q_ref, k_ref, v_ref, bias_ref, gq_ref, gk_ref, o_ref, kt_ref, vt_ref, kb_ref, vb_ref,
                        *, seq, rows, cols, keep):
    band = BAND_CHUNKS * CHUNK
    kn = _rms(k_ref[...], gk_ref[...])
    kt_ref[...] = kn[seq - keep:, :]
    vt_ref[...] = v_ref[pl.ds(seq - keep, keep), :]
    kb_ref[pl.ds(0, band), :] = jnp.zeros((band, D_HEAD), BF16)
    vb_ref[pl.ds(0, band), :] = jnp.zeros((band, D_HEAD), BF16)
    kb_ref[pl.ds(band, seq), :] = kn.astype(BF16)
    vb_ref[pl.ds(band, seq), :] = v_ref[...].astype(BF16)
    bias = bias_ref[0]
    gq = gq_ref[...]
    col = _iota((rows, cols), 1)

    def body(t, carry):
        r0 = pl.multiple_of(t * rows, rows)
        q = (_rms(q_ref[pl.ds(r0, rows), :], gq) * (SCALE * LOG2E)).astype(BF16)
        k = kb_ref[pl.ds(r0, cols), :]
        v = vb_ref[pl.ds(r0, cols), :]
        s = lax.dot_general(q, k, (((1,), (1,)), ((), ())), preferred_element_type=F32) + bias
        s = jnp.where(col >= band - r0, s, NEG)
        p = _softmax_rows_base2(s)
        o_ref[pl.ds(r0, rows), :] = jnp.dot(p.astype(BF16), v, preferred_element_type=F32).astype(o_ref.dtype)
        return carry

    lax.fori_loop(0, seq // rows, body, 0, unroll=BAND_UNROLL)


def band_attention_prompt(q, k, v, bias, g_q, g_k, nb, seq, keep):
    nh = q.shape[1] // D_HEAD
    rows, cols = bias.shape[1], bias.shape[2]
    band = BAND_CHUNKS * CHUNK
    blk = pl.BlockSpec((seq, D_HEAD), lambda b, h: (b, h))
    tail = pl.BlockSpec((keep, D_HEAD), lambda b, h: (b, h))
    gspec = pl.BlockSpec((1, D_HEAD), lambda b, h: (0, 0))
    return pl.pallas_call(
        functools.partial(_band_prompt_kernel, seq=seq, rows=rows, cols=cols, keep=keep),
        grid=(nb, nh),
        in_specs=[blk, blk, blk, pl.BlockSpec((1, rows, cols), lambda b, h: (h, 0, 0)), gspec, gspec],
        out_specs=[blk, tail, tail],
        out_shape=[jax.ShapeDtypeStruct(q.shape, BF16),
                   jax.ShapeDtypeStruct((nb * keep, q.shape[1]), F32),
                   jax.ShapeDtypeStruct((nb * keep, q.shape[1]), F32)],
        scratch_shapes=[pltpu.VMEM((band + seq, D_HEAD), BF16), pltpu.VMEM((band + seq, D_HEAD), BF16)],
        compiler_params=_params("arbitrary", "arbitrary"),
        name="band_attention_prompt",
    )(q, k, v, bias, g_q.reshape(1, D_HEAD), g_k.reshape(1, D_HEAD))


def _band_sample_kernel(q_ref, kn_ref, vn_ref, kc_hbm, vc_hbm, bias_ref, gq_ref, gk_ref, o_ref, ko_ref,
                        kbuf, vbuf, ksem, vsem, *, ts, w, cols, hg):
    slot = _prefetch_heads((kc_hbm, vc_hbm), (kbuf, vbuf), (ksem, vsem), hg)
    d = D_HEAD
    gq, gk = gq_ref[...], gk_ref[...]
    zeros = jnp.zeros((cols - w - ts, d), BF16)
    valid = _iota((ts, cols), 1) < w + ts
    for g in range(hg):
        hs = slice(g * d, (g + 1) * d)
        kn = _rms(kn_ref[:, hs], gk)
        ko_ref[:, hs] = kn
        k = jnp.concatenate([kbuf[slot, g].astype(BF16), kn.astype(BF16), zeros], axis=0)
        v = jnp.concatenate([vbuf[slot, g].astype(BF16), vn_ref[:, hs].astype(BF16), zeros], axis=0)
        q = (_rms(q_ref[:, hs], gq) * (SCALE * LOG2E)).astype(BF16)
        s = lax.dot_general(q, k, (((1,), (1,)), ((), ())), preferred_element_type=F32) + bias_ref[g]
        p = _softmax_rows_base2(jnp.where(valid, s, NEG))
        o_ref[:, hs] = jnp.dot(p.astype(BF16), v, preferred_element_type=F32).astype(o_ref.dtype)


def band_attention_sample(q, kn, vn, kc, vc, bias, g_q, g_k, nb, ts, w, nh, hg=SAMPLE_HEAD_GROUP):
    cols = -(-(w + ts) // LANES) * LANES
    assert w == BAND_CHUNKS * CHUNK and ts <= CHUNK and cols <= bias.shape[2] and ts % 16 == 0 and nh % hg == 0
    new_spec = pl.BlockSpec((ts, hg * D_HEAD), lambda b, h: (b, h))
    hbm = pl.BlockSpec(memory_space=pl.ANY)
    gspec = pl.BlockSpec((1, D_HEAD), lambda b, h: (0, 0))
    return pl.pallas_call(
        functools.partial(_band_sample_kernel, ts=ts, w=w, cols=cols, hg=hg),
        grid=(nb, nh // hg),
        in_specs=[new_spec, new_spec, new_spec, hbm, hbm,
                  pl.BlockSpec((hg, ts, cols), lambda b, h: (h, 0, 0)), gspec, gspec],
        out_specs=[new_spec, new_spec],
        out_shape=[jax.ShapeDtypeStruct(q.shape, BF16), jax.ShapeDtypeStruct(q.shape, F32)],
        scratch_shapes=[pltpu.VMEM((2, hg, w, D_HEAD), F32), pltpu.VMEM((2, hg, w, D_HEAD), F32),
                        pltpu.SemaphoreType.DMA((2, hg)), pltpu.SemaphoreType.DMA((2, hg))],
        compiler_params=_params("arbitrary", "arbitrary"),
        name="band_attention_sample",
    )(q, kn, vn, kc, vc, bias, g_q.reshape(1, D_HEAD), g_k.reshape(1, D_HEAD))


def _norm_rope(x, g, cosf, sina, sinb):
    half = ROT_DIM // 2
    y = _rms(x, g)
    return y * cosf + pltpu.roll(y, D_HEAD - half, 1) * sina + pltpu.roll(y, half, 1) * sinb


def rope_tables(pos):
    half = ROT_DIM // 2
    inv_freq = ROPE_THETA ** (-jnp.arange(half, dtype=F32) * (2.0 / ROT_DIM))
    ang = pos.astype(F32)[:, None] * inv_freq[None, :]
    cos, sin = jnp.cos(ang), jnp.sin(ang)
    n = pos.shape[0]
    rest = D_HEAD - ROT_DIM
    cosf = jnp.concatenate([cos, cos, jnp.ones((n, rest), F32)], axis=1)
    sina = jnp.concatenate([-sin, jnp.zeros((n, rest + half), F32)], axis=1)
    sinb = jnp.concatenate([jnp.zeros((n, half), F32), sin, jnp.zeros((n, rest), F32)], axis=1)
    return cosf, sina, sinb


def _online_block(q, k, v, mask, m, l, acc):
    s = lax.dot_general(q, k, (((1,), (1,)), ((), ())), preferred_element_type=F32)
    if mask is not None:
        s = jnp.where(mask, s, NEG)
    m_new = jnp.maximum(m, jnp.max(s, axis=-1, keepdims=True))
    alpha = jnp.exp2(m - m_new)
    p = jnp.exp2(s - m_new)
    l = alpha * l + jnp.sum(p, axis=-1, keepdims=True)
    acc = alpha * acc + jnp.dot(p.astype(BF16), v, preferred_element_type=F32)
    return m_new, l, acc


def _diff_lambda(lq1, lk1, lq2, lk2):
    return (jnp.exp(jnp.sum(lq1 * lk1, axis=-1, keepdims=True))
            - jnp.exp(jnp.sum(lq2 * lk2, axis=-1, keepdims=True)) + LAMBDA_INIT)


def _diff_finish(state, lam, gsub):
    (_, l1, a1), (_, l2, a2) = state
    o = a1 / l1 - lam * (a2 / l2)
    return _rms(o, gsub) * (1.0 - LAMBDA_INIT)


def _diff_init(tq):
    return (jnp.full((tq, 1), NEG, F32), jnp.zeros((tq, 1), F32), jnp.zeros((tq, 2 * D_HEAD), F32))


def _diff_prompt_kernel(q_ref, k_ref, v_ref, cos_ref, sa_ref, sb_ref, gq_ref, gk_ref, lam_ref, gsub_ref,
                        o_ref, ko_ref, k1_ref, k2_ref, vb_ref, *, t):
    qi = pl.program_id(2)
    d = D_HEAD

    @pl.when(qi == 0)
    def _():
        gk = gk_ref[...]
        cosf, sina, sinb = cos_ref[...], sa_ref[...], sb_ref[...]
        k1 = _norm_rope(k_ref[:, :d], gk, cosf, sina, sinb)
        k2 = _norm_rope(k_ref[:, d:], gk, cosf, sina, sinb)
        ko_ref[:, :d] = k1
        ko_ref[:, d:] = k2
        k1_ref[...] = k1.astype(BF16)
        k2_ref[...] = k2.astype(BF16)
        vb_ref[...] = v_ref[...].astype(BF16)

    r0 = pl.multiple_of(qi * t, t)
    gq = gq_ref[...]
    cosf, sina, sinb = cos_ref[pl.ds(r0, t), :], sa_ref[pl.ds(r0, t), :], sb_ref[pl.ds(r0, t), :]
    q1 = (_norm_rope(q_ref[:, :d], gq, cosf, sina, sinb) * (SCALE * LOG2E)).astype(BF16)
    q2 = (_norm_rope(q_ref[:, d:], gq, cosf, sina, sinb) * (SCALE * LOG2E)).astype(BF16)

    def step(k0, state, mask):
        v = vb_ref[pl.ds(k0, t), :]
        s1 = _online_block(q1, k1_ref[pl.ds(k0, t), :], v, mask, *state[0])
        s2 = _online_block(q2, k2_ref[pl.ds(k0, t), :], v, mask, *state[1])
        return s1, s2

    def body(kb, state):
        return step(pl.multiple_of(kb * t, t), state, None)

    state = lax.fori_loop(0, qi, body, (_diff_init(t), _diff_init(t)))
    mask = (_iota((t, t), 1) // CHUNK) <= (_iota((t, t), 0) // CHUNK)
    state = step(r0, state, mask)
    lam = _diff_lambda(lam_ref[0:1, :], lam_ref[1:2, :], lam_ref[2:3, :], lam_ref[3:4, :])
    o_ref[...] = _diff_finish(state, lam, gsub_ref[...]).astype(o_ref.dtype)


def diff_attention_prompt(q, k, v, tabs, g_q, g_k, lam4, g_sub, nb, seq, t=DIFF_T):
    d2 = 2 * D_HEAD
    nh = q.shape[1] // d2
    assert t % CHUNK == 0
    kv_spec = pl.BlockSpec((seq, d2), lambda b, h, qi: (b, h))
    q_spec = pl.BlockSpec((t, d2), lambda b, h, qi: (b * (seq // t) + qi, h))
    tab_spec = pl.BlockSpec((seq, D_HEAD), lambda b, h, qi: (0, 0))
    g_spec = pl.BlockSpec((1, D_HEAD), lambda b, h, qi: (0, 0))
    return pl.pallas_call(
        functools.partial(_diff_prompt_kernel, t=t),
        grid=(nb, nh, seq // t),
        in_specs=[q_spec, kv_spec, kv_spec, tab_spec, tab_spec, tab_spec, g_spec, g_spec,
                  pl.BlockSpec((4, D_HEAD), lambda b, h, qi: (0, 0)),
                  pl.BlockSpec((1, d2), lambda b, h, qi: (0, 0))],
        out_specs=[q_spec, kv_spec],
        out_shape=[jax.ShapeDtypeStruct(q.shape, BF16), jax.ShapeDtypeStruct(k.shape, F32)],
        scratch_shapes=[pltpu.VMEM((seq, D_HEAD), BF16), pltpu.VMEM((seq, D_HEAD), BF16),
                        pltpu.VMEM((seq, d2), BF16)],
        compiler_params=_params("arbitrary", "arbitrary", "arbitrary"),
        name="diff_attention_prompt",
    )(q, k, v, *tabs, g_q.reshape(1, D_HEAD), g_k.reshape(1, D_HEAD), lam4, g_sub.reshape(1, d2))


def _diff_sample_kernel(q_ref, kn_ref, vn_ref, kc_hbm, vc_hbm, cos_ref, sa_ref, sb_ref, gq_ref, gk_ref,
                        lam_ref, gsub_ref, o_ref, ko_ref, kbuf, vbuf, ksem, vsem, k1_ref, k2_ref, vb_ref,
                        *, ts, past, tn, hg):
    slot = _prefetch_heads((kc_hbm, vc_hbm), (kbuf, vbuf), (ksem, vsem), hg)
    d = D_HEAD
    cosf, sina, sinb = cos_ref[...], sa_ref[...], sb_ref[...]
    gq, gk = gq_ref[...], gk_ref[...]
    lam = _diff_lambda(lam_ref[0:1, :], lam_ref[1:2, :], lam_ref[2:3, :], lam_ref[3:4, :])
    gsub = gsub_ref[...]
    nk = past + tn
    kpos = _iota((ts, nk), 1)
    qpos = past + _iota((ts, nk), 0)
    mask = ((kpos // CHUNK) <= (qpos // CHUNK)) & (kpos < past + ts)
    for g in range(hg):
        c1, c2 = slice(2 * g * d, (2 * g + 1) * d), slice((2 * g + 1) * d, (2 * g + 2) * d)
        cv = slice(2 * g * d, (2 * g + 2) * d)
        k1 = _norm_rope(kn_ref[:, c1], gk, cosf, sina, sinb)
        k2 = _norm_rope(kn_ref[:, c2], gk, cosf, sina, sinb)
        ko_ref[:, c1] = k1
        ko_ref[:, c2] = k2
        k1_ref[g, pl.ds(0, past), :] = kbuf[slot, g, :, :d].astype(BF16)
        k2_ref[g, pl.ds(0, past), :] = kbuf[slot, g, :, d:].astype(BF16)
        vb_ref[g, pl.ds(0, past), :] = vbuf[slot, g].astype(BF16)
        k1_ref[g, pl.ds(past, tn), :] = jnp.zeros((tn, d), BF16)
        k2_ref[g, pl.ds(past, tn), :] = jnp.zeros((tn, d), BF16)
        vb_ref[g, pl.ds(past, tn), :] = jnp.zeros((tn, 2 * d), BF16)
        k1_ref[g, pl.ds(past, ts), :] = k1.astype(BF16)
        k2_ref[g, pl.ds(past, ts), :] = k2.astype(BF16)
        vb_ref[g, pl.ds(past, ts), :] = vn_ref[:, cv].astype(BF16)
        q1 = (_norm_rope(q_ref[:, c1], gq, cosf, sina, sinb) * (SCALE * LOG2E)).astype(BF16)
        q2 = (_norm_rope(q_ref[:, c2], gq, cosf, sina, sinb) * (SCALE * LOG2E)).astype(BF16)
        v = vb_ref[g]
        state = (_online_block(q1, k1_ref[g], v, mask, *_diff_init(ts)),
                 _online_block(q2, k2_ref[g], v, mask, *_diff_init(ts)))
        o_ref[:, cv] = _diff_finish(state, lam, gsub).astype(o_ref.dtype)


def diff_attention_sample(q, kn, vn, kc, vc, tabs, g_q, g_k, lam4, g_sub, nb, ts, past, nh, tn=LANES,
                          hg=SAMPLE_HEAD_GROUP):
    d2 = 2 * D_HEAD
    assert past % LANES == 0 and ts <= tn and nh % hg == 0
    new_spec = pl.BlockSpec((ts, hg * d2), lambda b, h: (b, h))
    hbm = pl.BlockSpec(memory_space=pl.ANY)
    tab_spec = pl.BlockSpec((ts, D_HEAD), lambda b, h: (0, 0))
    g_spec = pl.BlockSpec((1, D_HEAD), lambda b, h: (0, 0))
    return pl.pallas_call(
        functools.partial(_diff_sample_kernel, ts=ts, past=past, tn=tn, hg=hg),
        grid=(nb, nh // hg),
        in_specs=[new_spec, new_spec, new_spec, hbm, hbm, tab_spec, tab_spec, tab_spec,
                  g_spec, g_spec, pl.BlockSpec((4, D_HEAD), lambda b, h: (0, 0)),
                  pl.BlockSpec((1, d2), lambda b, h: (0, 0))],
        out_specs=[new_spec, new_spec],
        out_shape=[jax.ShapeDtypeStruct(q.shape, BF16), jax.ShapeDtypeStruct(q.shape, F32)],
        scratch_shapes=[pltpu.VMEM((2, hg, past, d2), F32), pltpu.VMEM((2, hg, past, d2), F32),
                        pltpu.SemaphoreType.DMA((2, hg)), pltpu.SemaphoreType.DMA((2, hg)),
                        pltpu.VMEM((hg, past + tn, D_HEAD), BF16), pltpu.VMEM((hg, past + tn, D_HEAD), BF16),
                        pltpu.VMEM((hg, past + tn, d2), BF16)],
        compiler_params=_params("arbitrary", "arbitrary"),
        name="diff_attention_sample",
    )(q, kn, vn, kc, vc, *tabs, g_q.reshape(1, D_HEAD), g_k.reshape(1, D_HEAD), lam4, g_sub.reshape(1, d2))


def _to_time_major(x, nb, ts):
    return x.reshape(nb, ts, -1).transpose(1, 0, 2).reshape(nb * ts, -1)


def _to_batch_major(x, nb, ts):
    return x.reshape(ts, nb, -1).transpose(1, 0, 2).reshape(nb * ts, -1)


def _conv_glu_layer(hp, hs, g_norm, w_up, w_conv, w_down_bf16, layer, state, nb_s, ts, seq):
    mp = hp.shape[0]
    hs_t = _to_time_major(hs, nb_s, ts)
    xn = rmsnorm_rows(hp, hs_t, g_norm)
    hidden, conv_p, conv_s_t = up_convglu(xn, w_up, w_conv, layer, state.transpose(1, 0, 2), mp, seq)
    hp, hs_t = down_project(hidden, w_down_bf16, layer, hp, hs_t)
    return hp, _to_batch_major(hs_t, nb_s, ts), conv_p, conv_s_t.transpose(1, 0, 2)


def kernel(x_prompt, x_sample, cache_k_sb, cache_v_sb, cache_k_band, cache_v_band, cache_k_diff, cache_v_diff, state_conv_ffn, norm_mix, norm_ffn, w_in_ab, w_out_ab, g_q_band, g_k_band, rel_bias_band, w_in_diff, w_out_diff, g_q_diff, g_k_diff, lambda_q1, lambda_k1, lambda_q2, lambda_k2, g_sub_diff, w_up, w_conv, w_down):
    bp, tp, dm = x_prompt.shape
    bs, ts, _ = x_sample.shape
    past = cache_k_sb.shape[1]
    h_sb, h_cb, h_df = cache_k_sb.shape[2], cache_k_band.shape[2], cache_k_diff.shape[2]
    d_sb, d_cb, d_df = h_sb * D_HEAD, h_cb * D_HEAD, h_df * 2 * D_HEAD
    w_band = cache_k_band.shape[1]
    keep = min(BAND_CHUNKS * CHUNK, tp)
    mp, ms = bp * tp, bs * ts

    hp = x_prompt.reshape(mp, dm)
    hs = x_sample.reshape(ms, dm)

    xn = rmsnorm_rows(hp, hs, norm_mix[0])
    sec = []
    col = 0
    for width in (d_sb, d_sb, d_sb, d_cb, d_cb, d_cb):
        sec.append(project(xn, w_in_ab, col, width, mp))
        col += width
    (qa_p, qa_s), (ka_p, ka_s), (va_p, va_s), (qb_p, qb_s), (kb_p, kb_s), (vb_p, vb_s) = sec

    oa_p = sb_attention_prompt(qa_p, ka_p, va_p, bp, tp)
    oa_s = sb_attention_sample(qa_s, ka_s, va_s, cache_k_sb, cache_v_sb, bs, ts, past, h_sb)
    bias = band_bias(rel_bias_band, BAND_QCHUNKS)
    ob_p, k_band_p, v_band_p = band_attention_prompt(qb_p, kb_p, vb_p, bias, g_q_band, g_k_band, bp, tp, keep)
    ob_s, k_band_s = band_attention_sample(qb_s, kb_s, vb_s, cache_k_band, cache_v_band, bias,
                                           g_q_band, g_k_band, bs, ts, w_band, h_cb)
    hp, hs = out_project([oa_p, ob_p], [oa_s, ob_s], w_out_ab, hp, hs)
    w_down_bf16 = w_down.astype(BF16)
    hp, hs, conv_p0, conv_s0 = _conv_glu_layer(hp, hs, norm_ffn[0], w_up, w_conv, w_down_bf16, 0,
                                               state_conv_ffn[0], bs, ts, tp)

    xn = rmsnorm_rows(hp, hs, norm_mix[1])
    (q_p, q_s), (k_p, k_s), (v_p, v_s) = [project(xn, w_in_diff, c * d_df, d_df, mp) for c in range(3)]
    lam4 = jnp.stack([lambda_q1, lambda_k1, lambda_q2, lambda_k2]).astype(F32)
    o_p, k_diff_p = diff_attention_prompt(q_p, k_p, v_p, rope_tables(jnp.arange(tp)), g_q_diff, g_k_diff,
                                          lam4, g_sub_diff, bp, tp)
    o_s, k_diff_s = diff_attention_sample(q_s, k_s, v_s, cache_k_diff, cache_v_diff,
                                          rope_tables(past + jnp.arange(ts)), g_q_diff, g_k_diff,
                                          lam4, g_sub_diff, bs, ts, past, h_df)
    hp, hs = out_project([o_p], [o_s], w_out_diff, hp, hs)
    hp, hs, conv_p1, conv_s1 = _conv_glu_layer(hp, hs, norm_ffn[1], w_up, w_conv, w_down_bf16, 1,
                                               state_conv_ffn[1], bs, ts, tp)

    return (hp.reshape(bp, tp, dm), hs.reshape(bs, ts, dm),
            ka_p.reshape(bp, tp, h_sb, D_HEAD), va_p.reshape(bp, tp, h_sb, D_HEAD),
            k_band_p.reshape(bp, keep, h_cb, D_HEAD), v_band_p.reshape(bp, keep, h_cb, D_HEAD),
            k_diff_p.reshape(bp, tp, h_df, 2 * D_HEAD), v_p.reshape(bp, tp, h_df, 2 * D_HEAD),
            jnp.stack([conv_p0, conv_p1]),
            ka_s.reshape(bs, ts, h_sb, D_HEAD), va_s.reshape(bs, ts, h_sb, D_HEAD),
            k_band_s.reshape(bs, ts, h_cb, D_HEAD), vb_s.reshape(bs, ts, h_cb, D_HEAD),
            k_diff_s.reshape(bs, ts, h_df, 2 * D_HEAD), v_s.reshape(bs, ts, h_df, 2 * D_HEAD),
            jnp.stack([conv_s0, conv_s1]))
```

```python
import functools
import math

import jax
import jax.numpy as jnp
from jax import lax
from jax.experimental import pallas as pl
from jax.experimental.pallas import tpu as pltpu

F32 = jnp.float32
BF16 = jnp.bfloat16

CHUNK = 64
D_HEAD = 128
BAND_CHUNKS = 8
REL_CLIP = 128
ROT_DIM = D_HEAD // 4
ROPE_THETA = 500000.0
CONV_W = 3
EPS = 1e-6
NEG = -1e30
LAMBDA_INIT = 0.8 - 0.6 * math.exp(-0.3 * 1)
SCALE = 1.0 / math.sqrt(D_HEAD)
LOG2E = math.log2(math.e)

V7X_VMEM_LIMIT_BYTES = 56 * 1024 * 1024
LANES = 128

ROW_TILE = 1024
PROJ_ROW_TILE = 512
PROJ_COL_TILE = 1024
OUT_ROW_TILE = 1024
UP_COL_BLOCK = 256
UP_ROW_CHUNK = 512
DOWN_ROW_TILE = 256
SB_T = 512
SB_CB = 256
BAND_QCHUNKS = 4
BAND_UNROLL = 4
SAMPLE_HEAD_GROUP = 4
DIFF_T = 512
DIFF_TK = 512


def _params(*sem):
    return pltpu.CompilerParams(dimension_semantics=sem, vmem_limit_bytes=V7X_VMEM_LIMIT_BYTES)


def _iota(shape, dim):
    return lax.broadcasted_iota(jnp.int32, shape, dim)


def _rms(x, g):
    return x * lax.rsqrt(jnp.mean(x * x, axis=-1, keepdims=True) + EPS) * g


def _rmsnorm_kernel(xp_ref, xs_ref, g_ref, o_ref, *, np_tiles):
    i = pl.program_id(0)

    @pl.when(i < np_tiles)
    def _():
        o_ref[...] = _rms(xp_ref[...], g_ref[...]).astype(o_ref.dtype)

    @pl.when(i >= np_tiles)
    def _():
        o_ref[...] = _rms(xs_ref[...], g_ref[...]).astype(o_ref.dtype)


def rmsnorm_rows(xp, xs, g, tm=256):
    mp, d = xp.shape
    ms = xs.shape[0]
    np_tiles, ns_tiles = mp // tm, ms // tm
    return pl.pallas_call(
        functools.partial(_rmsnorm_kernel, np_tiles=np_tiles),
        grid=(np_tiles + ns_tiles,),
        in_specs=[
            pl.BlockSpec((tm, d), lambda i: (jnp.minimum(i, np_tiles - 1), 0)),
            pl.BlockSpec((tm, d), lambda i: (jnp.maximum(i - np_tiles, 0), 0)),
            pl.BlockSpec((1, d), lambda i: (0, 0)),
        ],
        out_specs=pl.BlockSpec((tm, d), lambda i: (i, 0)),
        out_shape=jax.ShapeDtypeStruct((mp + ms, d), BF16),
        compiler_params=_params("arbitrary"),
        name="rmsnorm_rows",
    )(xp, xs, g.reshape(1, d))


def _refresh_weights(copies, wf_ref, wb_ref):
    jj, i = pl.program_id(0), pl.program_id(1)
    tb = wf_ref.shape[2]

    @pl.when(i == 0)
    def _():
        @pl.when(jj == 0)
        def _():
            for cp in copies(jj):
                cp.start()

        for cp in copies(jj):
            cp.wait()
        for n in range(wf_ref.shape[0]):
            wb_ref[:, n * tb:(n + 1) * tb] = wf_ref[n].astype(BF16)

        @pl.when(jj + 1 < pl.num_programs(0))
        def _():
            for cp in copies(jj + 1):
                cp.start()


def _proj_kernel(x_ref, w_hbm, op_ref, os_ref, wf_ref, wsem, wb_ref, *, col0, np_tiles, ms):
    i = pl.program_id(1)
    tn = wf_ref.shape[2]

    def copies(jj):
        c0 = pl.multiple_of(col0 + jj * tn, tn)
        return [pltpu.make_async_copy(w_hbm.at[:, pl.ds(c0, tn)], wf_ref.at[0], wsem.at[0])]

    _refresh_weights(copies, wf_ref, wb_ref)

    @pl.when(i < np_tiles)
    def _():
        op_ref[...] = jnp.dot(x_ref[...], wb_ref[...], preferred_element_type=F32)

    @pl.when(i >= np_tiles)
    def _():
        os_ref[...] = jnp.dot(x_ref[0:ms, :], wb_ref[...], preferred_element_type=F32)


def project(x, w, col0, n, mp, tm=PROJ_ROW_TILE, tn=PROJ_COL_TILE):
    m, k = x.shape
    ms = m - mp
    assert mp % tm == 0 and 0 < ms <= tm and n % tn == 0 and col0 % tn == 0
    np_tiles = mp // tm
    return pl.pallas_call(
        functools.partial(_proj_kernel, col0=col0, np_tiles=np_tiles, ms=ms),
        grid=(n // tn, np_tiles + 1),
        in_specs=[
            pl.BlockSpec((tm, k), lambda j, i: (i, 0)),
            pl.BlockSpec(memory_space=pl.ANY),
        ],
        out_specs=[
            pl.BlockSpec((tm, tn), lambda j, i: (jnp.minimum(i, np_tiles - 1), j)),
            pl.BlockSpec((ms, tn), lambda j, i: (0, j)),
        ],
        out_shape=[jax.ShapeDtypeStruct((mp, n), F32), jax.ShapeDtypeStruct((ms, n), F32)],
        scratch_shapes=[pltpu.VMEM((1, k, tn), F32), pltpu.SemaphoreType.DMA((1,)), pltpu.VMEM((k, tn), BF16)],
        compiler_params=_params("arbitrary", "arbitrary"),
        name="project",
    )(x, w)


def _outproj_kernel(*refs, np_tiles, nparts):
    ap = refs[0:nparts]
    a_s = refs[nparts:2 * nparts]
    w_hbm, rp_ref, rs_ref, op_ref, os_ref, wf_ref, wsem, wb_ref = refs[2 * nparts:]
    i = pl.program_id(1)
    kp, tn = wf_ref.shape[1], wf_ref.shape[2]

    def copies(jj):
        c0 = pl.multiple_of(jj * tn, tn)
        return [pltpu.make_async_copy(w_hbm.at[pl.ds(p * kp, kp), pl.ds(c0, tn)], wf_ref.at[p], wsem.at[p])
                for p in range(nparts)]

    _refresh_weights(copies, wf_ref, wb_ref)

    def run(a_refs, r_ref, o_ref):
        acc = r_ref[...]
        for p in range(nparts):
            acc = acc + jnp.dot(a_refs[p][...], wb_ref[:, p * tn:(p + 1) * tn], preferred_element_type=F32)
        o_ref[...] = acc

    @pl.when(i < np_tiles)
    def _():
        run(ap, rp_ref, op_ref)

    @pl.when(i >= np_tiles)
    def _():
        run(a_s, rs_ref, os_ref)


def out_project(parts_p, parts_s, w, res_p, res_s, tm=OUT_ROW_TILE, tn=512):
    nparts = len(parts_p)
    mp, kp = parts_p[0].shape
    ms = parts_s[0].shape[0]
    n = w.shape[1]
    assert mp % tm == 0 and n % tn == 0 and w.shape[0] == nparts * kp
    np_tiles = mp // tm
    pmap = lambda j, i: (jnp.minimum(i, np_tiles - 1), 0)
    pmap_o = lambda j, i: (jnp.minimum(i, np_tiles - 1), j)
    in_specs = ([pl.BlockSpec((tm, kp), pmap)] * nparts + [pl.BlockSpec((ms, kp), lambda j, i: (0, 0))] * nparts
                + [pl.BlockSpec(memory_space=pl.ANY),
                   pl.BlockSpec((tm, tn), pmap_o), pl.BlockSpec((ms, tn), lambda j, i: (0, j))])
    return pl.pallas_call(
        functools.partial(_outproj_kernel, np_tiles=np_tiles, nparts=nparts),
        grid=(n // tn, np_tiles + 1),
        in_specs=in_specs,
        out_specs=[pl.BlockSpec((tm, tn), pmap_o), pl.BlockSpec((ms, tn), lambda j, i: (0, j))],
        out_shape=[jax.ShapeDtypeStruct((mp, n), F32), jax.ShapeDtypeStruct((ms, n), F32)],
        scratch_shapes=[pltpu.VMEM((nparts, kp, tn), F32), pltpu.SemaphoreType.DMA((nparts,)),
                        pltpu.VMEM((kp, nparts * tn), BF16)],
        compiler_params=_params("arbitrary", "arbitrary"),
        name="out_project",
    )(*parts_p, *parts_s, w, res_p, res_s)


def _silu_mul(gc, u):
    return gc * (1.0 / (1.0 + jnp.exp(-gc))) * u


def _up_weight_copies(w_hbm, wf_ref, wsem, jj, *, layer, nj, tb):
    j0 = 2 * jj
    j1 = jnp.minimum(j0 + 1, nj - 1)
    cols = (j0, j1, nj + j0, nj + j1)
    return [pltpu.make_async_copy(w_hbm.at[layer, :, pl.ds(pl.multiple_of(c * tb, tb), tb)], wf_ref.at[n], wsem.at[n])
            for n, c in enumerate(cols)]


def _up_kernel(x_ref, w_hbm, wc_ref, hist_ref, h_ref, cp_ref, cs_ref, wf_ref, wsem, wb_ref, carry_ref,
               *, layer, nj, tb, np_tiles, tiles_per_seq, nb_s, ms, rc):
    i = pl.program_id(1)
    tm = x_ref.shape[0]
    tn = 2 * tb
    _refresh_weights(functools.partial(_up_weight_copies, w_hbm, wf_ref, wsem, layer=layer, nj=nj, tb=tb),
                     wf_ref, wb_ref)

    w0 = wc_ref[0:1, :]
    w1 = wc_ref[1:2, :]
    w2 = wc_ref[2:3, :]

    @pl.when(lax.rem(i, tiles_per_seq) == 0)
    def _():
        carry_ref[...] = jnp.zeros_like(carry_ref)

    @pl.when(i < np_tiles)
    def _():
        c0 = carry_ref[0:1, :]
        c1 = carry_ref[1:2, :]
        row = _iota((rc, tn), 0)
        for r in range(tm // rc):
            rows = slice(r * rc, (r + 1) * rc)
            gu = jnp.dot(x_ref[rows, :], wb_ref[...], preferred_element_type=F32)
            g = gu[:, :tn]
            u = gu[:, tn:]
            gm1 = jnp.where(row == 0, c1, pltpu.roll(g, 1, 0))
            gm2 = jnp.where(row == 0, c0, jnp.where(row == 1, c1, pltpu.roll(g, 2, 0)))
            gc = gm2 * w0 + gm1 * w1 + g * w2
            h_ref[rows, :] = _silu_mul(gc, u).astype(h_ref.dtype)
            c0 = g[rc - 2:rc - 1, :]
            c1 = g[rc - 1:rc, :]
        tail = jnp.concatenate([c0, c1], axis=0)
        carry_ref[0:2, :] = tail
        cp_ref[0] = tail

    @pl.when(i >= np_tiles)
    def _():
        gu = jnp.dot(x_ref[0:ms, :], wb_ref[...], preferred_element_type=F32)
        g = gu[:, :tn]
        u = gu[:, tn:]
        h0 = hist_ref[0]
        h1 = hist_ref[1]
        gm1 = jnp.concatenate([h1, g[:ms - nb_s]], axis=0)
        gm2 = jnp.concatenate([h0, h1, g[:ms - 2 * nb_s]], axis=0)
        gc = gm2 * w0 + gm1 * w1 + g * w2
        h_ref[0:ms, :] = _silu_mul(gc, u).astype(h_ref.dtype)
        cs_ref[0] = g[ms - 2 * nb_s:ms - nb_s]
        cs_ref[1] = g[ms - nb_s:]


def up_convglu(x, w_up, w_conv, layer, hist_t, mp, seq_len, tm=ROW_TILE, tb=UP_COL_BLOCK, rc=UP_ROW_CHUNK):
    m, k = x.shape
    f = w_up.shape[2] // 2
    nb_s = hist_t.shape[1]
    nb_p = mp // seq_len
    np_tiles = mp // tm
    ms = m - mp
    assert 0 < ms <= tm and seq_len % tm == 0 and tm % rc == 0 and f % tb == 0 and CONV_W == 3
    tiles_per_seq = seq_len // tm
    nj = f // tb
    tn = 2 * tb
    njj = -(-nj // 2)
    return pl.pallas_call(
        functools.partial(_up_kernel, layer=layer, nj=nj, tb=tb, np_tiles=np_tiles, tiles_per_seq=tiles_per_seq,
                          nb_s=nb_s, ms=ms, rc=rc),
        grid=(njj, np_tiles + 1),
        in_specs=[
            pl.BlockSpec((tm, k), lambda j, i: (i, 0)),
            pl.BlockSpec(memory_space=pl.ANY),
            pl.BlockSpec((None, CONV_W, tn), lambda j, i: (layer, 0, j)),
            pl.BlockSpec((CONV_W - 1, nb_s, tn), lambda j, i: (0, 0, j)),
        ],
        out_specs=[
            pl.BlockSpec((tm, tn), lambda j, i: (i, j)),
            pl.BlockSpec((1, CONV_W - 1, tn), lambda j, i: (jnp.minimum(i // tiles_per_seq, nb_p - 1), 0, j)),
            pl.BlockSpec((CONV_W - 1, nb_s, tn), lambda j, i: (0, 0, j)),
        ],
        out_shape=[
            jax.ShapeDtypeStruct((m, f), BF16),
            jax.ShapeDtypeStruct((nb_p, CONV_W - 1, f), F32),
            jax.ShapeDtypeStruct((CONV_W - 1, nb_s, f), F32),
        ],
        scratch_shapes=[pltpu.VMEM((4, k, tb), F32), pltpu.SemaphoreType.DMA((4,)),
                        pltpu.VMEM((k, 2 * tn), BF16), pltpu.VMEM((8, tn), F32)],
        compiler_params=_params("arbitrary", "arbitrary"),
        name="up_convglu",
    )(x, w_up, w_conv, hist_t)


def _down_kernel(a_ref, w_hbm, rp_ref, rs_ref, op_ref, os_ref, wf_ref, wsem, wb_ref, *, layer, np_tiles):
    i = pl.program_id(1)
    tn = wf_ref.shape[2]

    def copies(jj):
        c0 = pl.multiple_of(jj * tn, tn)
        return [pltpu.make_async_copy(w_hbm.at[layer, :, pl.ds(c0, tn)], wf_ref.at[0], wsem.at[0])]

    _refresh_weights(copies, wf_ref, wb_ref)

    @pl.when(i < np_tiles)
    def _():
        op_ref[...] = rp_ref[...] + jnp.dot(a_ref[...], wb_ref[...], preferred_element_type=F32)

    @pl.when(i >= np_tiles)
    def _():
        os_ref[...] = rs_ref[...] + jnp.dot(a_ref[...], wb_ref[...], preferred_element_type=F32)


def down_project(a, w, layer, res_p, res_s, tm=DOWN_ROW_TILE, tn=512):
    m, f = a.shape
    n = w.shape[2]
    mp = res_p.shape[0]
    ms = m - mp
    np_tiles, ns_tiles = mp // tm, ms // tm
    pmap_o = lambda j, i: (jnp.minimum(i, np_tiles - 1), j)
    smap_o = lambda j, i: (jnp.maximum(i - np_tiles, 0), j)
    return pl.pallas_call(
        functools.partial(_down_kernel, layer=layer, np_tiles=np_tiles),
        grid=(n // tn, np_tiles + ns_tiles),
        in_specs=[
            pl.BlockSpec((tm, f), lambda j, i: (i, 0)),
            pl.BlockSpec(memory_space=pl.ANY),
            pl.BlockSpec((tm, tn), pmap_o),
            pl.BlockSpec((tm, tn), smap_o),
        ],
        out_specs=[pl.BlockSpec((tm, tn), pmap_o), pl.BlockSpec((tm, tn), smap_o)],
        out_shape=[jax.ShapeDtypeStruct((mp, n), F32), jax.ShapeDtypeStruct((ms, n), F32)],
        scratch_shapes=[pltpu.VMEM((1, f, tn), F32), pltpu.SemaphoreType.DMA((1,)), pltpu.VMEM((f, tn), BF16)],
        compiler_params=_params("arbitrary", "arbitrary"),
        name="down_project",
    )(a, w, res_p, res_s)


def _sb_group(q, k, v, uu, run, acc, vis):
    nk = k.shape[0]
    cb = uu.shape[1]
    z2 = lax.dot_general(q, k, (((1,), (1,)), ((), ())), preferred_element_type=F32)
    s = jnp.maximum(z2, 0.0) + jnp.log2(1.0 + jnp.exp2(-jnp.abs(z2)))
    if vis is not None:
        s = jnp.where(vis, s, 0.0)
    hi = s.astype(BF16)
    lo = (s - hi.astype(F32)).astype(BF16)
    afters = []
    for blk in reversed(range(nk // cb)):
        sl = slice(blk * cb, (blk + 1) * cb)
        cs = jnp.dot(jnp.concatenate([hi[:, sl], lo[:, sl]], axis=1), uu, preferred_element_type=F32)
        afters.append(cs + run)
        run = run + jnp.sum(s[:, sl], axis=-1, keepdims=True)
    after = jnp.concatenate(afters[::-1], axis=1)
    w = jnp.exp2(z2 - s - after)
    if vis is not None:
        w = jnp.where(vis, w, 0.0)
    acc = acc + jnp.dot(w.astype(BF16), v, preferred_element_type=F32)
    return run, acc


def _sb_prompt_kernel(q_ref, k_ref, v_ref, uu_ref, o_ref, kb_ref, vb_ref, *, t):
    qi = pl.program_id(2)

    @pl.when(qi == 0)
    def _():
        kb_ref[...] = k_ref[...].astype(BF16)
        vb_ref[...] = v_ref[...].astype(BF16)

    q = (q_ref[...] * (SCALE * LOG2E)).astype(BF16)
    uu = uu_ref[...]
    r0 = pl.multiple_of(qi * t, t)
    run = jnp.zeros((t, 1), F32)
    acc = jnp.zeros((t, D_HEAD), F32)
    vis = _iota((t, t), 1) < _iota((t, t), 0)
    run, acc = _sb_group(q, kb_ref[pl.ds(r0, t), :], vb_ref[pl.ds(r0, t), :], uu, run, acc, vis)

    def body(g, carry):
        k0 = pl.multiple_of((qi - 1 - g) * t, t)
        return _sb_group(q, kb_ref[pl.ds(k0, t), :], vb_ref[pl.ds(k0, t), :], uu, carry[0], carry[1], None)

    run, acc = lax.fori_loop(0, qi, body, (run, acc))
    o_ref[...] = acc.astype(o_ref.dtype)


def _suffix_matrix(cb):
    j = jnp.arange(cb)[:, None]
    s = jnp.arange(cb)[None, :]
    u = (j > s).astype(BF16)
    return jnp.concatenate([u, u], axis=0)


def sb_attention_prompt(q, k, v, nb, seq, t=SB_T, cb=SB_CB):
    nh = q.shape[1] // D_HEAD
    kv_spec = pl.BlockSpec((seq, D_HEAD), lambda b, h, qi: (b, h))
    q_spec = pl.BlockSpec((t, D_HEAD), lambda b, h, qi: (b * (seq // t) + qi, h))
    return pl.pallas_call(
        functools.partial(_sb_prompt_kernel, t=t),
        grid=(nb, nh, seq // t),
        in_specs=[q_spec, kv_spec, kv_spec, pl.BlockSpec((2 * cb, cb), lambda b, h, qi: (0, 0))],
        out_specs=q_spec,
        out_shape=jax.ShapeDtypeStruct(q.shape, BF16),
        scratch_shapes=[pltpu.VMEM((seq, D_HEAD), BF16), pltpu.VMEM((seq, D_HEAD), BF16)],
        compiler_params=_params("arbitrary", "arbitrary", "arbitrary"),
        name="sb_attention_prompt",
    )(q, k, v, _suffix_matrix(cb))


def _head_copy(cache_hbm, buf_ref, sem_ref, b, h, slot, g):
    return pltpu.make_async_copy(cache_hbm.at[b, :, h, :], buf_ref.at[slot, g], sem_ref.at[slot, g])


def _prefetch_heads(caches, bufs, sems, hg):
    b, hh = pl.program_id(0), pl.program_id(1)
    nb, ng = pl.num_programs(0), pl.num_programs(1)
    step = b * ng + hh
    slot = lax.rem(step, 2)
    wrap = hh + 1 == ng
    b1 = jnp.where(wrap, b + 1, b)
    hh1 = jnp.where(wrap, 0, hh + 1)

    def copies(bb, hgrp, sl):
        return [_head_copy(c, buf, sem, bb, hgrp * hg + g, sl, g)
                for c, buf, sem in zip(caches, bufs, sems) for g in range(hg)]

    @pl.when(step == 0)
    def _():
        for cp in copies(b, hh, slot):
            cp.start()

    @pl.when(step + 1 < nb * ng)
    def _():
        for cp in copies(b1, hh1, 1 - slot):
            cp.start()

    for cp in copies(b, hh, slot):
        cp.wait()
    return slot


def _sb_sample_kernel(q_ref, kn_ref, vn_ref, kc_hbm, vc_hbm, uu_ref, o_ref, kbuf, vbuf, ksem, vsem, kb_ref, vb_ref,
                      *, ts, past, pad, hg):
    slot = _prefetch_heads((kc_hbm, vc_hbm), (kbuf, vbuf), (ksem, vsem), hg)
    d = D_HEAD
    nk = past + pad
    vis = _iota((ts, nk), 1) < past + _iota((ts, nk), 0)
    uu = uu_ref[...]
    for g in range(hg):
        cols = slice(g * d, (g + 1) * d)
        kb_ref[g, pl.ds(0, past), :] = kbuf[slot, g].astype(BF16)
        vb_ref[g, pl.ds(0, past), :] = vbuf[slot, g].astype(BF16)
        kb_ref[g, pl.ds(past, pad), :] = jnp.zeros((pad, d), BF16)
        vb_ref[g, pl.ds(past, pad), :] = jnp.zeros((pad, d), BF16)
        kb_ref[g, pl.ds(past, ts), :] = kn_ref[:, cols].astype(BF16)
        vb_ref[g, pl.ds(past, ts), :] = vn_ref[:, cols].astype(BF16)
        _, acc = _sb_group((q_ref[:, cols] * (SCALE * LOG2E)).astype(BF16), kb_ref[g], vb_ref[g], uu,
                           jnp.zeros((ts, 1), F32), jnp.zeros((ts, d), F32), vis)
        o_ref[:, cols] = acc.astype(o_ref.dtype)


def sb_attention_sample(q, kn, vn, kc, vc, nb, ts, past, nh, cb=SB_CB, hg=SAMPLE_HEAD_GROUP):
    assert past % cb == 0 and ts <= cb and nh % hg == 0
    new_spec = pl.BlockSpec((ts, hg * D_HEAD), lambda b, h: (b, h))
    hbm = pl.BlockSpec(memory_space=pl.ANY)
    return pl.pallas_call(
        functools.partial(_sb_sample_kernel, ts=ts, past=past, pad=cb, hg=hg),
        grid=(nb, nh // hg),
        in_specs=[new_spec, new_spec, new_spec, hbm, hbm, pl.BlockSpec((2 * cb, cb), lambda b, h: (0, 0))],
        out_specs=new_spec,
        out_shape=jax.ShapeDtypeStruct(q.shape, BF16),
        scratch_shapes=[pltpu.VMEM((2, hg, past, D_HEAD), F32), pltpu.VMEM((2, hg, past, D_HEAD), F32),
                        pltpu.SemaphoreType.DMA((2, hg)), pltpu.SemaphoreType.DMA((2, hg)),
                        pltpu.VMEM((hg, past + cb, D_HEAD), BF16), pltpu.VMEM((hg, past + cb, D_HEAD), BF16)],
        compiler_params=_params("arbitrary", "arbitrary"),
        name="sb_attention_sample",
    )(q, kn, vn, kc, vc, _suffix_matrix(cb))


def _band_bias_kernel(tab_ref, o_ref, *, rows, cols, width):
    h = pl.program_id(0)
    band = BAND_CHUNKS * CHUNK
    x = _iota((8, width), 1)
    idx = jnp.clip(x - rows - band, -REL_CLIP, REL_CLIP) + REL_CLIP

    def body(r, g):
        return jnp.where(idx == r, tab_ref[h, r] * LOG2E, g)

    g = lax.fori_loop(0, 2 * REL_CLIP + 1, body, jnp.zeros((8, width), F32))
    gt = jnp.broadcast_to(g[0:1, :], (rows, width))
    gt = pltpu.roll(gt, 0, 1, stride=1, stride_axis=0)
    bias = gt[:, rows:rows + cols]
    dc = _iota((rows, cols), 1) // CHUNK - _iota((rows, cols), 0) // CHUNK
    o_ref[0] = jnp.where((dc >= 0) & (dc <= BAND_CHUNKS), bias, NEG)


def band_bias(table, qchunks):
    nh = table.shape[0]
    rows = qchunks * CHUNK
    cols = (qchunks + BAND_CHUNKS) * CHUNK
    width = rows + cols
    assert width % LANES == 0 and rows % LANES == 0
    return pl.pallas_call(
        functools.partial(_band_bias_kernel, rows=rows, cols=cols, width=width),
        grid=(nh,),
        in_specs=[pl.BlockSpec(memory_space=pltpu.SMEM)],
        out_specs=pl.BlockSpec((1, rows, cols), lambda h: (h, 0, 0)),
        out_shape=jax.ShapeDtypeStruct((nh, rows, cols), F32),
        compiler_params=_params("arbitrary"),
        name="band_bias",
    )(table)


def _softmax_rows_base2(s2):
    m = jnp.max(s2, axis=-1, keepdims=True)
    p = jnp.exp2(s2 - m)
    return p / jnp.sum(p, axis=-1, keepdims=True)


def _band_prompt_kernel(q_ref, k_ref, v_ref, bias_ref, gq_ref, gk_ref, o_ref, kt_ref, vt_ref, kb_ref, vb_ref,
                        *, seq, rows, cols, keep):
    band = BAND_CHUNKS * CHUNK
    kn = _rms(k_ref[...], gk_ref[...])
    kt_ref[...] = kn[seq - keep:, :]
    vt_ref[...] = v_ref[pl.ds(seq - keep, keep), :]
    kb_ref[pl.ds(0, band), :] = jnp.zeros((band, D_HEAD), BF16)
    vb_ref[pl.ds(0, band), :] = jnp.zeros((band, D_HEAD), BF16)
    kb_ref[pl.ds(band, seq), :] = kn.astype(BF16)
    vb_ref[pl.ds(band, seq), :] = v_ref[...].astype(BF16)
    bias = bias_ref[0]
    gq = gq_ref[...]
    col = _iota((rows, cols), 1)

    def body(t, carry):
        r0 = pl.multiple_of(t * rows, rows)
        q = (_rms(q_ref[pl.ds(r0, rows), :], gq) * (SCALE * LOG2E)).astype(BF16)
        k = kb_ref[pl.ds(r0, cols), :]
        v = vb_ref[pl.ds(r0, cols), :]
        s = lax.dot_general(q, k, (((1,), (1,)), ((), ())), preferred_element_type=F32) + bias
        s = jnp.where(col >= band - r0, s, NEG)
        p = _softmax_rows_base2(s)
        o_ref[pl.ds(r0, rows), :] = jnp.dot(p.astype(BF16), v, preferred_element_type=F32).astype(o_ref.dtype)
        return carry

    lax.fori_loop(0, seq // rows, body, 0, unroll=BAND_UNROLL)


def band_attention_prompt(q, k, v, bias, g_q, g_k, nb, seq, keep):
    nh = q.shape[1] // D_HEAD
    rows, cols = bias.shape[1], bias.shape[2]
    band = BAND_CHUNKS * CHUNK
    blk = pl.BlockSpec((seq, D_HEAD), lambda b, h: (b, h))
    tail = pl.BlockSpec((keep, D_HEAD), lambda b, h: (b, h))
    gspec = pl.BlockSpec((1, D_HEAD), lambda b, h: (0, 0))
    return pl.pallas_call(
        functools.partial(_band_prompt_kernel, seq=seq, rows=rows, cols=cols, keep=keep),
        grid=(nb, nh),
        in_specs=[blk, blk, blk, pl.BlockSpec((1, rows, cols), lambda b, h: (h, 0, 0)), gspec, gspec],
        out_specs=[blk, tail, tail],
        out_shape=[jax.ShapeDtypeStruct(q.shape, BF16),
                   jax.ShapeDtypeStruct((nb * keep, q.shape[1]), F32),
                   jax.ShapeDtypeStruct((nb * keep, q.shape[1]), F32)],
        scratch_shapes=[pltpu.VMEM((band + seq, D_HEAD), BF16), pltpu.VMEM((band + seq, D_HEAD), BF16)],
        compiler_params=_params("arbitrary", "arbitrary"),
        name="band_attention_prompt",
    )(q, k, v, bias, g_q.reshape(1, D_HEAD), g_k.reshape(1, D_HEAD))


def _band_sample_kernel(q_ref, kn_ref, vn_ref, kc_hbm, vc_hbm, bias_ref, gq_ref, gk_ref, o_ref, ko_ref,
                        kbuf, vbuf, ksem, vsem, *, ts, w, cols, hg):
    slot = _prefetch_heads((kc_hbm, vc_hbm), (kbuf, vbuf), (ksem, vsem), hg)
    d = D_HEAD
    gq, gk = gq_ref[...], gk_ref[...]
    zeros = jnp.zeros((cols - w - ts, d), BF16)
    valid = _iota((ts, cols), 1) < w + ts
    for g in range(hg):
        hs = slice(g * d, (g + 1) * d)
        kn = _rms(kn_ref[:, hs], gk)
        ko_ref[:, hs] = kn
        k = jnp.concatenate([kbuf[slot, g].astype(BF16), kn.astype(BF16), zeros], axis=0)
        v = jnp.concatenate([vbuf[slot, g].astype(BF16), vn_ref[:, hs].astype(BF16), zeros], axis=0)
        q = (_rms(q_ref[:, hs], gq) * (SCALE * LOG2E)).astype(BF16)
        s = lax.dot_general(q, k, (((1,), (1,)), ((), ())), preferred_element_type=F32) + bias_ref[g]
        p = _softmax_rows_base2(jnp.where(valid, s, NEG))
        o_ref[:, hs] = jnp.dot(p.astype(BF16), v, preferred_element_type=F32).astype(o_ref.dtype)


def band_attention_sample(q, kn, vn, kc, vc, bias, g_q, g_k, nb, ts, w, nh, hg=SAMPLE_HEAD_GROUP):
    cols = -(-(w + ts) // LANES) * LANES
    assert w == BAND_CHUNKS * CHUNK and ts <= CHUNK and cols <= bias.shape[2] and ts % 16 == 0 and nh % hg == 0
    new_spec = pl.BlockSpec((ts, hg * D_HEAD), lambda b, h: (b, h))
    hbm = pl.BlockSpec(memory_space=pl.ANY)
    gspec = pl.BlockSpec((1, D_HEAD), lambda b, h: (0, 0))
    return pl.pallas_call(
        functools.partial(_band_sample_kernel, ts=ts, w=w, cols=cols, hg=hg),
        grid=(nb, nh // hg),
        in_specs=[new_spec, new_spec, new_spec, hbm, hbm,
                  pl.BlockSpec((hg, ts, cols), lambda b, h: (h, 0, 0)), gspec, gspec],
        out_specs=[new_spec, new_spec],
        out_shape=[jax.ShapeDtypeStruct(q.shape, BF16), jax.ShapeDtypeStruct(q.shape, F32)],
        scratch_shapes=[pltpu.VMEM((2, hg, w, D_HEAD), F32), pltpu.VMEM((2, hg, w, D_HEAD), F32),
                        pltpu.SemaphoreType.DMA((2, hg)), pltpu.SemaphoreType.DMA((2, hg))],
        compiler_params=_params("arbitrary", "arbitrary"),
        name="band_attention_sample",
    )(q, kn, vn, kc, vc, bias, g_q.reshape(1, D_HEAD), g_k.reshape(1, D_HEAD))


def _norm_rope(x, g, cosf, sina, sinb):
    half = ROT_DIM // 2
    y = _rms(x, g)
    return y * cosf + pltpu.roll(y, D_HEAD - half, 1) * sina + pltpu.roll(y, half, 1) * sinb


def rope_tables(pos):
    half = ROT_DIM // 2
    inv_freq = ROPE_THETA ** (-jnp.arange(half, dtype=F32) * (2.0 / ROT_DIM))
    ang = pos.astype(F32)[:, None] * inv_freq[None, :]
    cos, sin = jnp.cos(ang), jnp.sin(ang)
    n = pos.shape[0]
    rest = D_HEAD - ROT_DIM
    cosf = jnp.concatenate([cos, cos, jnp.ones((n, rest), F32)], axis=1)
    sina = jnp.concatenate([-sin, jnp.zeros((n, rest + half), F32)], axis=1)
    sinb = jnp.concatenate([jnp.zeros((n, half), F32), sin, jnp.zeros((n, rest), F32)], axis=1)
    return cosf, sina, sinb


def _online_block(q, k, v, mask, m, l, acc):
    s = lax.dot_general(q, k, (((1,), (1,)), ((), ())), preferred_element_type=F32)
    if mask is not None:
        s = jnp.where(mask, s, NEG)
    m_new = jnp.maximum(m, jnp.max(s, axis=-1, keepdims=True))
    alpha = jnp.exp2(m - m_new)
    p = jnp.exp2(s - m_new)
    l = alpha * l + jnp.sum(p, axis=-1, keepdims=True)
    acc = alpha * acc + jnp.dot(p.astype(BF16), v, preferred_element_type=F32)
    return m_new, l, acc


def _diff_lambda(lq1, lk1, lq2, lk2):
    return (jnp.exp(jnp.sum(lq1 * lk1, axis=-1, keepdims=True))
            - jnp.exp(jnp.sum(lq2 * lk2, axis=-1, keepdims=True)) + LAMBDA_INIT)


def _diff_finish(state, lam, gsub):
    (_, l1, a1), (_, l2, a2) = state
    o = a1 / l1 - lam * (a2 / l2)
    return _rms(o, gsub) * (1.0 - LAMBDA_INIT)


def _diff_init(tq):
    return (jnp.full((tq, 1), NEG, F32), jnp.zeros((tq, 1), F32), jnp.zeros((tq, 2 * D_HEAD), F32))


def _diff_prompt_kernel(q_ref, k_ref, v_ref, cos_ref, sa_ref, sb_ref, gq_ref, gk_ref, lam_ref, gsub_ref,
                        o_ref, ko_ref, k1_ref, k2_ref, vb_ref, *, t, tk):
    qi = pl.program_id(2)
    d = D_HEAD

    @pl.when(qi == 0)
    def _():
        gk = gk_ref[...]
        cosf, sina, sinb = cos_ref[...], sa_ref[...], sb_ref[...]
        k1 = _norm_rope(k_ref[:, :d], gk, cosf, sina, sinb)
        k2 = _norm_rope(k_ref[:, d:], gk, cosf, sina, sinb)
        ko_ref[:, :d] = k1
        ko_ref[:, d:] = k2
        k1_ref[...] = k1.astype(BF16)
        k2_ref[...] = k2.astype(BF16)
        vb_ref[...] = v_ref[...].astype(BF16)

    r0 = pl.multiple_of(qi * t, t)
    gq = gq_ref[...]
    cosf, sina, sinb = cos_ref[pl.ds(r0, t), :], sa_ref[pl.ds(r0, t), :], sb_ref[pl.ds(r0, t), :]
    q1 = (_norm_rope(q_ref[:, :d], gq, cosf, sina, sinb) * (SCALE * LOG2E)).astype(BF16)
    q2 = (_norm_rope(q_ref[:, d:], gq, cosf, sina, sinb) * (SCALE * LOG2E)).astype(BF16)

    def step(k0, state, mask):
        v = vb_ref[pl.ds(k0, tk), :]
        s1 = _online_block(q1, k1_ref[pl.ds(k0, tk), :], v, mask, *state[0])
        s2 = _online_block(q2, k2_ref[pl.ds(k0, tk), :], v, mask, *state[1])
        return s1, s2

    def body(kb, state):
        return step(pl.multiple_of(kb * tk, tk), state, None)

    state = lax.fori_loop(0, qi * (t // tk), body, (_diff_init(t), _diff_init(t)))
    for c in range(t // tk):
        mask = ((c * tk + _iota((t, tk), 1)) // CHUNK) <= (_iota((t, tk), 0) // CHUNK)
        state = step(pl.multiple_of(r0 + c * tk, tk), state, mask)
    lam = _diff_lambda(lam_ref[0:1, :], lam_ref[1:2, :], lam_ref[2:3, :], lam_ref[3:4, :])
    o_ref[...] = _diff_finish(state, lam, gsub_ref[...]).astype(o_ref.dtype)


def diff_attention_prompt(q, k, v, tabs, g_q, g_k, lam4, g_sub, nb, seq, t=DIFF_T, tk=DIFF_TK):
    d2 = 2 * D_HEAD
    nh = q.shape[1] // d2
    assert t % tk == 0 and tk % CHUNK == 0
    kv_spec = pl.BlockSpec((seq, d2), lambda b, h, qi: (b, h))
    q_spec = pl.BlockSpec((t, d2), lambda b, h, qi: (b * (seq // t) + qi, h))
    tab_spec = pl.BlockSpec((seq, D_HEAD), lambda b, h, qi: (0, 0))
    g_spec = pl.BlockSpec((1, D_HEAD), lambda b, h, qi: (0, 0))
    return pl.pallas_call(
        functools.partial(_diff_prompt_kernel, t=t, tk=tk),
        grid=(nb, nh, seq // t),
        in_specs=[q_spec, kv_spec, kv_spec, tab_spec, tab_spec, tab_spec, g_spec, g_spec,
                  pl.BlockSpec((4, D_HEAD), lambda b, h, qi: (0, 0)),
                  pl.BlockSpec((1, d2), lambda b, h, qi: (0, 0))],
        out_specs=[q_spec, kv_spec],
        out_shape=[jax.ShapeDtypeStruct(q.shape, BF16), jax.ShapeDtypeStruct(k.shape, F32)],
        scratch_shapes=[pltpu.VMEM((seq, D_HEAD), BF16), pltpu.VMEM((seq, D_HEAD), BF16),
                        pltpu.VMEM((seq, d2), BF16)],
        compiler_params=_params("arbitrary", "arbitrary", "arbitrary"),
        name="diff_attention_prompt",
    )(q, k, v, *tabs, g_q.reshape(1, D_HEAD), g_k.reshape(1, D_HEAD), lam4, g_sub.reshape(1, d2))


def _diff_sample_kernel(q_ref, kn_ref, vn_ref, kc_hbm, vc_hbm, cos_ref, sa_ref, sb_ref, gq_ref, gk_ref,
                        lam_ref, gsub_ref, o_ref, ko_ref, kbuf, vbuf, ksem, vsem, k1_ref, k2_ref, vb_ref,
                        *, ts, past, tn, hg):
    slot = _prefetch_heads((kc_hbm, vc_hbm), (kbuf, vbuf), (ksem, vsem), hg)
    d = D_HEAD
    cosf, sina, sinb = cos_ref[...], sa_ref[...], sb_ref[...]
    gq, gk = gq_ref[...], gk_ref[...]
    lam = _diff_lambda(lam_ref[0:1, :], lam_ref[1:2, :], lam_ref[2:3, :], lam_ref[3:4, :])
    gsub = gsub_ref[...]
    nk = past + tn
    kpos = _iota((ts, nk), 1)
    qpos = past + _iota((ts, nk), 0)
    mask = ((kpos // CHUNK) <= (qpos // CHUNK)) & (kpos < past + ts)
    for g in range(hg):
        c1, c2 = slice(2 * g * d, (2 * g + 1) * d), slice((2 * g + 1) * d, (2 * g + 2) * d)
        cv = slice(2 * g * d, (2 * g + 2) * d)
        k1 = _norm_rope(kn_ref[:, c1], gk, cosf, sina, sinb)
        k2 = _norm_rope(kn_ref[:, c2], gk, cosf, sina, sinb)
        ko_ref[:, c1] = k1
        ko_ref[:, c2] = k2
        k1_ref[g, pl.ds(0, past), :] = kbuf[slot, g, :, :d].astype(BF16)
        k2_ref[g, pl.ds(0, past), :] = kbuf[slot, g, :, d:].astype(BF16)
        vb_ref[g, pl.ds(0, past), :] = vbuf[slot, g].astype(BF16)
        k1_ref[g, pl.ds(past, tn), :] = jnp.zeros((tn, d), BF16)
        k2_ref[g, pl.ds(past, tn), :] = jnp.zeros((tn, d), BF16)
        vb_ref[g, pl.ds(past, tn), :] = jnp.zeros((tn, 2 * d), BF16)
        k1_ref[g, pl.ds(past, ts), :] = k1.astype(BF16)
        k2_ref[g, pl.ds(past, ts), :] = k2.astype(BF16)
        vb_ref[g, pl.ds(past, ts), :] = vn_ref[:, cv].astype(BF16)
        q1 = (_norm_rope(q_ref[:, c1], gq, cosf, sina, sinb) * (SCALE * LOG2E)).astype(BF16)
        q2 = (_norm_rope(q_ref[:, c2], gq, cosf, sina, sinb) * (SCALE * LOG2E)).astype(BF16)
        v = vb_ref[g]
        state = (_online_block(q1, k1_ref[g], v, mask, *_diff_init(ts)),
                 _online_block(q2, k2_ref[g], v, mask, *_diff_init(ts)))
        o_ref[:, cv] = _diff_finish(state, lam, gsub).astype(o_ref.dtype)


def diff_attention_sample(q, kn, vn, kc, vc, tabs, g_q, g_k, lam4, g_sub, nb, ts, past, nh, tn=LANES,
                          hg=SAMPLE_HEAD_GROUP):
    d2 = 2 * D_HEAD
    assert past % LANES == 0 and ts <= tn and nh % hg == 0
    new_spec = pl.BlockSpec((ts, hg * d2), lambda b, h: (b, h))
    hbm = pl.BlockSpec(memory_space=pl.ANY)
    tab_spec = pl.BlockSpec((ts, D_HEAD), lambda b, h: (0, 0))
    g_spec = pl.BlockSpec((1, D_HEAD), lambda b, h: (0, 0))
    return pl.pallas_call(
        functools.partial(_diff_sample_kernel, ts=ts, past=past, tn=tn, hg=hg),
        grid=(nb, nh // hg),
        in_specs=[new_spec, new_spec, new_spec, hbm, hbm, tab_spec, tab_spec, tab_spec,
                  g_spec, g_spec, pl.BlockSpec((4, D_HEAD), lambda b, h: (0, 0)),
                  pl.BlockSpec((1, d2), lambda b, h: (0, 0))],
        out_specs=[new_spec, new_spec],
        out_shape=[jax.ShapeDtypeStruct(q.shape, BF16), jax.ShapeDtypeStruct(q.shape, F32)],
        scratch_shapes=[pltpu.VMEM((2, hg, past, d2), F32), pltpu.VMEM((2, hg, past, d2), F32),
                        pltpu.SemaphoreType.DMA((2, hg)), pltpu.SemaphoreType.DMA((2, hg)),
                        pltpu.VMEM((hg, past + tn, D_HEAD), BF16), pltpu.VMEM((hg, past + tn, D_HEAD), BF16),
                        pltpu.VMEM((hg, past + tn, d2), BF16)],
        compiler_params=_params("arbitrary", "arbitrary"),
        name="diff_attention_sample",
    )(q, kn, vn, kc, vc, *tabs, g_q.reshape(1, D_HEAD), g_k.reshape(1, D_HEAD), lam4, g_sub.reshape(1, d2))


def _to_time_major(x, nb, ts):
    return x.reshape(nb, ts, -1).transpose(1, 0, 2).reshape(nb * ts, -1)


def _to_batch_major(x, nb, ts):
    return x.reshape(ts, nb, -1).transpose(1, 0, 2).reshape(nb * ts, -1)


def _conv_glu_layer(hp, hs, g_norm, w_up, w_conv, w_down, layer, state, nb_s, ts, seq):
    mp = hp.shape[0]
    hs_t = _to_time_major(hs, nb_s, ts)
    xn = rmsnorm_rows(hp, hs_t, g_norm)
    hidden, conv_p, conv_s_t = up_convglu(xn, w_up, w_conv, layer, state.transpose(1, 0, 2), mp, seq)
    hp, hs_t = down_project(hidden, w_down, layer, hp, hs_t)
    return hp, _to_batch_major(hs_t, nb_s, ts), conv_p, conv_s_t.transpose(1, 0, 2)


def kernel(x_prompt, x_sample, cache_k_sb, cache_v_sb, cache_k_band, cache_v_band, cache_k_diff, cache_v_diff, state_conv_ffn, norm_mix, norm_ffn, w_in_ab, w_out_ab, g_q_band, g_k_band, rel_bias_band, w_in_diff, w_out_diff, g_q_diff, g_k_diff, lambda_q1, lambda_k1, lambda_q2, lambda_k2, g_sub_diff, w_up, w_conv, w_down):
    bp, tp, dm = x_prompt.shape
    bs, ts, _ = x_sample.shape
    past = cache_k_sb.shape[1]
    h_sb, h_cb, h_df = cache_k_sb.shape[2], cache_k_band.shape[2], cache_k_diff.shape[2]
    d_sb, d_cb, d_df = h_sb * D_HEAD, h_cb * D_HEAD, h_df * 2 * D_HEAD
    w_band = cache_k_band.shape[1]
    keep = min(BAND_CHUNKS * CHUNK, tp)
    mp, ms = bp * tp, bs * ts

    hp = x_prompt.reshape(mp, dm)
    hs = x_sample.reshape(ms, dm)

    xn = rmsnorm_rows(hp, hs, norm_mix[0])
    sec = []
    col = 0
    for width in (d_sb, d_sb, d_sb, d_cb, d_cb, d_cb):
        sec.append(project(xn, w_in_ab, col, width, mp))
        col += width
    (qa_p, qa_s), (ka_p, ka_s), (va_p, va_s), (qb_p, qb_s), (kb_p, kb_s), (vb_p, vb_s) = sec

    oa_p = sb_attention_prompt(qa_p, ka_p, va_p, bp, tp)
    oa_s = sb_attention_sample(qa_s, ka_s, va_s, cache_k_sb, cache_v_sb, bs, ts, past, h_sb)
    bias = band_bias(rel_bias_band, BAND_QCHUNKS)
    ob_p, k_band_p, v_band_p = band_attention_prompt(qb_p, kb_p, vb_p, bias, g_q_band, g_k_band, bp, tp, keep)
    ob_s, k_band_s = band_attention_sample(qb_s, kb_s, vb_s, cache_k_band, cache_v_band, bias,
                                           g_q_band, g_k_band, bs, ts, w_band, h_cb)
    hp, hs = out_project([oa_p, ob_p], [oa_s, ob_s], w_out_ab, hp, hs)
    hp, hs, conv_p0, conv_s0 = _conv_glu_layer(hp, hs, norm_ffn[0], w_up, w_conv, w_down, 0,
                                               state_conv_ffn[0], bs, ts, tp)

    xn = rmsnorm_rows(hp, hs, norm_mix[1])
    (q_p, q_s), (k_p, k_s), (v_p, v_s) = [project(xn, w_in_diff, c * d_df, d_df, mp) for c in range(3)]
    lam4 = jnp.stack([lambda_q1, lambda_k1, lambda_q2, lambda_k2]).astype(F32)
    o_p, k_diff_p = diff_attention_prompt(q_p, k_p, v_p, rope_tables(jnp.arange(tp)), g_q_diff, g_k_diff,
                                          lam4, g_sub_diff, bp, tp)
    o_s, k_diff_s = diff_attention_sample(q_s, k_s, v_s, cache_k_diff, cache_v_diff,
                                          rope_tables(past + jnp.arange(ts)), g_q_diff, g_k_diff,
                                          lam4, g_sub_diff, bs, ts, past, h_df)
    hp, hs = out_project([o_p], [o_s], w_out_diff, hp, hs)
    hp, hs, conv_p1, conv_s1 = _conv_glu_layer(hp, hs, norm_ffn[1], w_up, w_conv, w_down, 1,
                                               state_conv_ffn[1], bs, ts, tp)

    return (hp.reshape(bp, tp, dm), hs.reshape(bs, ts, dm),
            ka_p.reshape(bp, tp, h_sb, D_HEAD), va_p.reshape(bp, tp, h_sb, D_HEAD),
            k_band_p.reshape(bp, keep, h_cb, D_HEAD), v_band_p.reshape(bp, keep, h_cb, D_HEAD),
            k_diff_p.reshape(bp, tp, h_df, 2 * D_HEAD), v_p.reshape(bp, tp, h_df, 2 * D_HEAD),
            jnp.stack([conv_p0, conv_p1]),
            ka_s.reshape(bs, ts, h_sb, D_HEAD), va_s.reshape(bs, ts, h_sb, D_HEAD),
            k_band_s.reshape(bs, ts, h_cb, D_HEAD), vb_s.reshape(bs, ts, h_cb, D_HEAD),
            k_diff_s.reshape(bs, ts, h_df, 2 * D_HEAD), v_s.reshape(bs, ts, h_df, 2 * D_HEAD),
            jnp.stack([conv_s0, conv_s1]))
```

```python
import functools
import math

import jax
import jax.numpy as jnp
from jax import lax
from jax.experimental import pallas as pl
from jax.experimental.pallas import tpu as pltpu

F32 = jnp.float32
BF16 = jnp.bfloat16

CHUNK = 64
D_HEAD = 128
BAND_CHUNKS = 8
REL_CLIP = 128
ROT_DIM = D_HEAD // 4
ROPE_THETA = 500000.0
CONV_W = 3
EPS = 1e-6
NEG = -1e30
LAMBDA_INIT = 0.8 - 0.6 * math.exp(-0.3 * 1)
SCALE = 1.0 / math.sqrt(D_HEAD)
LOG2E = math.log2(math.e)

V7X_VMEM_LIMIT_BYTES = 56 * 1024 * 1024
LANES = 128

ROW_TILE = 1024
PROJ_ROW_TILE = 512
PROJ_COL_TILE = 1024
OUT_ROW_TILE = 1024
UP_COL_BLOCK = 256
UP_ROW_CHUNK = 512
DOWN_ROW_TILE = 256
SB_T = 512
SB_CB = 256
BAND_QCHUNKS = 4
BAND_UNROLL = 8
SAMPLE_HEAD_GROUP = 4
DIFF_T = 512
DIFF_TK = 512
DIFF_TR = 128


def _params(*sem):
    return pltpu.CompilerParams(dimension_semantics=sem, vmem_limit_bytes=V7X_VMEM_LIMIT_BYTES)


def _iota(shape, dim):
    return lax.broadcasted_iota(jnp.int32, shape, dim)


def _rms(x, g):
    return x * lax.rsqrt(jnp.mean(x * x, axis=-1, keepdims=True) + EPS) * g


def _rmsnorm_kernel(xp_ref, xs_ref, g_ref, o_ref, *, np_tiles):
    i = pl.program_id(0)

    @pl.when(i < np_tiles)
    def _():
        o_ref[...] = _rms(xp_ref[...], g_ref[...]).astype(o_ref.dtype)

    @pl.when(i >= np_tiles)
    def _():
        o_ref[...] = _rms(xs_ref[...], g_ref[...]).astype(o_ref.dtype)


def rmsnorm_rows(xp, xs, g, tm=256):
    mp, d = xp.shape
    ms = xs.shape[0]
    np_tiles, ns_tiles = mp // tm, ms // tm
    return pl.pallas_call(
        functools.partial(_rmsnorm_kernel, np_tiles=np_tiles),
        grid=(np_tiles + ns_tiles,),
        in_specs=[
            pl.BlockSpec((tm, d), lambda i: (jnp.minimum(i, np_tiles - 1), 0)),
            pl.BlockSpec((tm, d), lambda i: (jnp.maximum(i - np_tiles, 0), 0)),
            pl.BlockSpec((1, d), lambda i: (0, 0)),
        ],
        out_specs=pl.BlockSpec((tm, d), lambda i: (i, 0)),
        out_shape=jax.ShapeDtypeStruct((mp + ms, d), BF16),
        compiler_params=_params("arbitrary"),
        name="rmsnorm_rows",
    )(xp, xs, g.reshape(1, d))


def _refresh_weights(copies, wf_ref, wb_ref):
    jj, i = pl.program_id(0), pl.program_id(1)
    tb = wf_ref.shape[2]

    @pl.when(i == 0)
    def _():
        @pl.when(jj == 0)
        def _():
            for cp in copies(jj):
                cp.start()

        for cp in copies(jj):
            cp.wait()
        for n in range(wf_ref.shape[0]):
            wb_ref[:, n * tb:(n + 1) * tb] = wf_ref[n].astype(BF16)

        @pl.when(jj + 1 < pl.num_programs(0))
        def _():
            for cp in copies(jj + 1):
                cp.start()


def _proj_kernel(x_ref, w_hbm, op_ref, os_ref, wf_ref, wsem, wb_ref, *, col0, np_tiles, ms):
    i = pl.program_id(1)
    tn = wf_ref.shape[2]

    def copies(jj):
        c0 = pl.multiple_of(col0 + jj * tn, tn)
        return [pltpu.make_async_copy(w_hbm.at[:, pl.ds(c0, tn)], wf_ref.at[0], wsem.at[0])]

    _refresh_weights(copies, wf_ref, wb_ref)

    @pl.when(i < np_tiles)
    def _():
        op_ref[...] = jnp.dot(x_ref[...], wb_ref[...], preferred_element_type=F32)

    @pl.when(i >= np_tiles)
    def _():
        os_ref[...] = jnp.dot(x_ref[0:ms, :], wb_ref[...], preferred_element_type=F32)


def project(x, w, col0, n, mp, tm=PROJ_ROW_TILE, tn=PROJ_COL_TILE):
    m, k = x.shape
    ms = m - mp
    assert mp % tm == 0 and 0 < ms <= tm and n % tn == 0 and col0 % tn == 0
    np_tiles = mp // tm
    return pl.pallas_call(
        functools.partial(_proj_kernel, col0=col0, np_tiles=np_tiles, ms=ms),
        grid=(n // tn, np_tiles + 1),
        in_specs=[
            pl.BlockSpec((tm, k), lambda j, i: (i, 0)),
            pl.BlockSpec(memory_space=pl.ANY),
        ],
        out_specs=[
            pl.BlockSpec((tm, tn), lambda j, i: (jnp.minimum(i, np_tiles - 1), j)),
            pl.BlockSpec((ms, tn), lambda j, i: (0, j)),
        ],
        out_shape=[jax.ShapeDtypeStruct((mp, n), F32), jax.ShapeDtypeStruct((ms, n), F32)],
        scratch_shapes=[pltpu.VMEM((1, k, tn), F32), pltpu.SemaphoreType.DMA((1,)), pltpu.VMEM((k, tn), BF16)],
        compiler_params=_params("arbitrary", "arbitrary"),
        name="project",
    )(x, w)


def _outproj_kernel(*refs, np_tiles, nparts):
    ap = refs[0:nparts]
    a_s = refs[nparts:2 * nparts]
    w_hbm, rp_ref, rs_ref, op_ref, os_ref, wf_ref, wsem, wb_ref = refs[2 * nparts:]
    i = pl.program_id(1)
    kp, tn = wf_ref.shape[1], wf_ref.shape[2]

    def copies(jj):
        c0 = pl.multiple_of(jj * tn, tn)
        return [pltpu.make_async_copy(w_hbm.at[pl.ds(p * kp, kp), pl.ds(c0, tn)], wf_ref.at[p], wsem.at[p])
                for p in range(nparts)]

    _refresh_weights(copies, wf_ref, wb_ref)

    def run(a_refs, r_ref, o_ref):
        acc = r_ref[...]
        for p in range(nparts):
            acc = acc + jnp.dot(a_refs[p][...], wb_ref[:, p * tn:(p + 1) * tn], preferred_element_type=F32)
        o_ref[...] = acc

    @pl.when(i < np_tiles)
    def _():
        run(ap, rp_ref, op_ref)

    @pl.when(i >= np_tiles)
    def _():
        run(a_s, rs_ref, os_ref)


def out_project(parts_p, parts_s, w, res_p, res_s, tm=OUT_ROW_TILE, tn=512):
    nparts = len(parts_p)
    mp, kp = parts_p[0].shape
    ms = parts_s[0].shape[0]
    n = w.shape[1]
    assert mp % tm == 0 and n % tn == 0 and w.shape[0] == nparts * kp
    np_tiles = mp // tm
    pmap = lambda j, i: (jnp.minimum(i, np_tiles - 1), 0)
    pmap_o = lambda j, i: (jnp.minimum(i, np_tiles - 1), j)
    in_specs = ([pl.BlockSpec((tm, kp), pmap)] * nparts + [pl.BlockSpec((ms, kp), lambda j, i: (0, 0))] * nparts
                + [pl.BlockSpec(memory_space=pl.ANY),
                   pl.BlockSpec((tm, tn), pmap_o), pl.BlockSpec((ms, tn), lambda j, i: (0, j))])
    return pl.pallas_call(
        functools.partial(_outproj_kernel, np_tiles=np_tiles, nparts=nparts),
        grid=(n // tn, np_tiles + 1),
        in_specs=in_specs,
        out_specs=[pl.BlockSpec((tm, tn), pmap_o), pl.BlockSpec((ms, tn), lambda j, i: (0, j))],
        out_shape=[jax.ShapeDtypeStruct((mp, n), F32), jax.ShapeDtypeStruct((ms, n), F32)],
        scratch_shapes=[pltpu.VMEM((nparts, kp, tn), F32), pltpu.SemaphoreType.DMA((nparts,)),
                        pltpu.VMEM((kp, nparts * tn), BF16)],
        compiler_params=_params("arbitrary", "arbitrary"),
        name="out_project",
    )(*parts_p, *parts_s, w, res_p, res_s)


def _silu_mul(gc, u):
    return gc * (1.0 / (1.0 + jnp.exp(-gc))) * u


def _up_weight_copies(w_hbm, wf_ref, wsem, jj, *, layer, nj, tb):
    j0 = 2 * jj
    j1 = jnp.minimum(j0 + 1, nj - 1)
    cols = (j0, j1, nj + j0, nj + j1)
    return [pltpu.make_async_copy(w_hbm.at[layer, :, pl.ds(pl.multiple_of(c * tb, tb), tb)], wf_ref.at[n], wsem.at[n])
            for n, c in enumerate(cols)]


def _up_kernel(x_ref, w_hbm, wc_ref, hist_ref, h_ref, cp_ref, cs_ref, wf_ref, wsem, wb_ref, carry_ref,
               *, layer, nj, tb, np_tiles, tiles_per_seq, nb_s, ms, rc):
    i = pl.program_id(1)
    tm = x_ref.shape[0]
    tn = 2 * tb
    _refresh_weights(functools.partial(_up_weight_copies, w_hbm, wf_ref, wsem, layer=layer, nj=nj, tb=tb),
                     wf_ref, wb_ref)

    w0 = wc_ref[0:1, :]
    w1 = wc_ref[1:2, :]
    w2 = wc_ref[2:3, :]

    @pl.when(lax.rem(i, tiles_per_seq) == 0)
    def _():
        carry_ref[...] = jnp.zeros_like(carry_ref)

    @pl.when(i < np_tiles)
    def _():
        c0 = carry_ref[0:1, :]
        c1 = carry_ref[1:2, :]
        row = _iota((rc, tn), 0)
        for r in range(tm // rc):
            rows = slice(r * rc, (r + 1) * rc)
            gu = jnp.dot(x_ref[rows, :], wb_ref[...], preferred_element_type=F32)
            g = gu[:, :tn]
            u = gu[:, tn:]
            gm1 = jnp.where(row == 0, c1, pltpu.roll(g, 1, 0))
            gm2 = jnp.where(row == 0, c0, jnp.where(row == 1, c1, pltpu.roll(g, 2, 0)))
            gc = gm2 * w0 + gm1 * w1 + g * w2
            h_ref[rows, :] = _silu_mul(gc, u).astype(h_ref.dtype)
            c0 = g[rc - 2:rc - 1, :]
            c1 = g[rc - 1:rc, :]
        tail = jnp.concatenate([c0, c1], axis=0)
        carry_ref[0:2, :] = tail
        cp_ref[0] = tail

    @pl.when(i >= np_tiles)
    def _():
        gu = jnp.dot(x_ref[0:ms, :], wb_ref[...], preferred_element_type=F32)
        g = gu[:, :tn]
        u = gu[:, tn:]
        h0 = hist_ref[0]
        h1 = hist_ref[1]
        gm1 = jnp.concatenate([h1, g[:ms - nb_s]], axis=0)
        gm2 = jnp.concatenate([h0, h1, g[:ms - 2 * nb_s]], axis=0)
        gc = gm2 * w0 + gm1 * w1 + g * w2
        h_ref[0:ms, :] = _silu_mul(gc, u).astype(h_ref.dtype)
        cs_ref[0] = g[ms - 2 * nb_s:ms - nb_s]
        cs_ref[1] = g[ms - nb_s:]


def up_convglu(x, w_up, w_conv, layer, hist_t, mp, seq_len, tm=ROW_TILE, tb=UP_COL_BLOCK, rc=UP_ROW_CHUNK):
    m, k = x.shape
    f = w_up.shape[2] // 2
    nb_s = hist_t.shape[1]
    nb_p = mp // seq_len
    np_tiles = mp // tm
    ms = m - mp
    assert 0 < ms <= tm and seq_len % tm == 0 and tm % rc == 0 and f % tb == 0 and CONV_W == 3
    tiles_per_seq = seq_len // tm
    nj = f // tb
    tn = 2 * tb
    njj = -(-nj // 2)
    return pl.pallas_call(
        functools.partial(_up_kernel, layer=layer, nj=nj, tb=tb, np_tiles=np_tiles, tiles_per_seq=tiles_per_seq,
                          nb_s=nb_s, ms=ms, rc=rc),
        grid=(njj, np_tiles + 1),
        in_specs=[
            pl.BlockSpec((tm, k), lambda j, i: (i, 0)),
            pl.BlockSpec(memory_space=pl.ANY),
            pl.BlockSpec((None, CONV_W, tn), lambda j, i: (layer, 0, j)),
            pl.BlockSpec((CONV_W - 1, nb_s, tn), lambda j, i: (0, 0, j)),
        ],
        out_specs=[
            pl.BlockSpec((tm, tn), lambda j, i: (i, j)),
            pl.BlockSpec((1, CONV_W - 1, tn), lambda j, i: (jnp.minimum(i // tiles_per_seq, nb_p - 1), 0, j)),
            pl.BlockSpec((CONV_W - 1, nb_s, tn), lambda j, i: (0, 0, j)),
        ],
        out_shape=[
            jax.ShapeDtypeStruct((m, f), BF16),
            jax.ShapeDtypeStruct((nb_p, CONV_W - 1, f), F32),
            jax.ShapeDtypeStruct((CONV_W - 1, nb_s, f), F32),
        ],
        scratch_shapes=[pltpu.VMEM((4, k, tb), F32), pltpu.SemaphoreType.DMA((4,)),
                        pltpu.VMEM((k, 2 * tn), BF16), pltpu.VMEM((8, tn), F32)],
        compiler_params=_params("arbitrary", "arbitrary"),
        name="up_convglu",
    )(x, w_up, w_conv, hist_t)


def _down_kernel(a_ref, w_hbm, rp_ref, rs_ref, op_ref, os_ref, wf_ref, wsem, wb_ref, *, layer, np_tiles):
    i = pl.program_id(1)
    tn = wf_ref.shape[2]

    def copies(jj):
        c0 = pl.multiple_of(jj * tn, tn)
        return [pltpu.make_async_copy(w_hbm.at[layer, :, pl.ds(c0, tn)], wf_ref.at[0], wsem.at[0])]

    _refresh_weights(copies, wf_ref, wb_ref)

    @pl.when(i < np_tiles)
    def _():
        op_ref[...] = rp_ref[...] + jnp.dot(a_ref[...], wb_ref[...], preferred_element_type=F32)

    @pl.when(i >= np_tiles)
    def _():
        os_ref[...] = rs_ref[...] + jnp.dot(a_ref[...], wb_ref[...], preferred_element_type=F32)


def down_project(a, w, layer, res_p, res_s, tm=DOWN_ROW_TILE, tn=512):
    m, f = a.shape
    n = w.shape[2]
    mp = res_p.shape[0]
    ms = m - mp
    np_tiles, ns_tiles = mp // tm, ms // tm
    pmap_o = lambda j, i: (jnp.minimum(i, np_tiles - 1), j)
    smap_o = lambda j, i: (jnp.maximum(i - np_tiles, 0), j)
    return pl.pallas_call(
        functools.partial(_down_kernel, layer=layer, np_tiles=np_tiles),
        grid=(n // tn, np_tiles + ns_tiles),
        in_specs=[
            pl.BlockSpec((tm, f), lambda j, i: (i, 0)),
            pl.BlockSpec(memory_space=pl.ANY),
            pl.BlockSpec((tm, tn), pmap_o),
            pl.BlockSpec((tm, tn), smap_o),
        ],
        out_specs=[pl.BlockSpec((tm, tn), pmap_o), pl.BlockSpec((tm, tn), smap_o)],
        out_shape=[jax.ShapeDtypeStruct((mp, n), F32), jax.ShapeDtypeStruct((ms, n), F32)],
        scratch_shapes=[pltpu.VMEM((1, f, tn), F32), pltpu.SemaphoreType.DMA((1,)), pltpu.VMEM((f, tn), BF16)],
        compiler_params=_params("arbitrary", "arbitrary"),
        name="down_project",
    )(a, w, res_p, res_s)


def _sb_group(q, k, v, uu, run, acc, vis):
    nk = k.shape[0]
    cb = uu.shape[1]
    z2 = lax.dot_general(q, k, (((1,), (1,)), ((), ())), preferred_element_type=F32)
    s = jnp.maximum(z2, 0.0) + jnp.log2(1.0 + jnp.exp2(-jnp.abs(z2)))
    if vis is not None:
        s = jnp.where(vis, s, 0.0)
    hi = s.astype(BF16)
    lo = (s - hi.astype(F32)).astype(BF16)
    afters = []
    for blk in reversed(range(nk // cb)):
        sl = slice(blk * cb, (blk + 1) * cb)
        cs = jnp.dot(jnp.concatenate([hi[:, sl], lo[:, sl]], axis=1), uu, preferred_element_type=F32)
        afters.append(cs + run)
        run = run + jnp.sum(s[:, sl], axis=-1, keepdims=True)
    after = jnp.concatenate(afters[::-1], axis=1)
    w = jnp.exp2(z2 - s - after)
    if vis is not None:
        w = jnp.where(vis, w, 0.0)
    acc = acc + jnp.dot(w.astype(BF16), v, preferred_element_type=F32)
    return run, acc


def _publish_heads(srcs, outs_hbm, sem_ref):
    b, h, qi = pl.program_id(0), pl.program_id(1), pl.program_id(2)
    copies = [pltpu.make_async_copy(src, out.at[b, :, h, :], sem_ref.at[n])
              for n, (src, out) in enumerate(zip(srcs, outs_hbm))]

    @pl.when(qi == 0)
    def _():
        for cp in copies:
            cp.start()

    @pl.when(qi == pl.num_programs(2) - 1)
    def _():
        for cp in copies:
            cp.wait()


def _sb_prompt_kernel(q_ref, k_ref, v_ref, uu_ref, o_ref, ko_hbm, vo_hbm, kb_ref, vb_ref, osem, *, t):
    qi = pl.program_id(2)

    @pl.when(qi == 0)
    def _():
        kb_ref[...] = k_ref[...].astype(BF16)
        vb_ref[...] = v_ref[...].astype(BF16)

    _publish_heads((k_ref, v_ref), (ko_hbm, vo_hbm), osem)

    q = (q_ref[...] * (SCALE * LOG2E)).astype(BF16)
    uu = uu_ref[...]
    r0 = pl.multiple_of(qi * t, t)
    run = jnp.zeros((t, 1), F32)
    acc = jnp.zeros((t, D_HEAD), F32)
    vis = _iota((t, t), 1) < _iota((t, t), 0)
    run, acc = _sb_group(q, kb_ref[pl.ds(r0, t), :], vb_ref[pl.ds(r0, t), :], uu, run, acc, vis)

    def body(g, carry):
        k0 = pl.multiple_of((qi - 1 - g) * t, t)
        return _sb_group(q, kb_ref[pl.ds(k0, t), :], vb_ref[pl.ds(k0, t), :], uu, carry[0], carry[1], None)

    run, acc = lax.fori_loop(0, qi, body, (run, acc))
    o_ref[...] = acc.astype(o_ref.dtype)


def _suffix_matrix(cb):
    j = jnp.arange(cb)[:, None]
    s = jnp.arange(cb)[None, :]
    u = (j > s).astype(BF16)
    return jnp.concatenate([u, u], axis=0)


def sb_attention_prompt(q, k, v, nb, seq, t=SB_T, cb=SB_CB):
    nh = q.shape[1] // D_HEAD
    assert seq % t == 0 and t % cb == 0
    kv_spec = pl.BlockSpec((seq, D_HEAD), lambda b, h, qi: (b, h))
    q_spec = pl.BlockSpec((t, D_HEAD), lambda b, h, qi: (b * (seq // t) + qi, h))
    hbm = pl.BlockSpec(memory_space=pl.ANY)
    cache = jax.ShapeDtypeStruct((nb, seq, nh, D_HEAD), F32)
    return pl.pallas_call(
        functools.partial(_sb_prompt_kernel, t=t),
        grid=(nb, nh, seq // t),
        in_specs=[q_spec, kv_spec, kv_spec, pl.BlockSpec((2 * cb, cb), lambda b, h, qi: (0, 0))],
        out_specs=[q_spec, hbm, hbm],
        out_shape=[jax.ShapeDtypeStruct(q.shape, BF16), cache, cache],
        scratch_shapes=[pltpu.VMEM((seq, D_HEAD), BF16), pltpu.VMEM((seq, D_HEAD), BF16),
                        pltpu.SemaphoreType.DMA((2,))],
        compiler_params=_params("arbitrary", "arbitrary", "arbitrary"),
        name="sb_attention_prompt",
    )(q, k, v, _suffix_matrix(cb))


def _head_copy(cache_hbm, buf_ref, sem_ref, b, h, slot, g):
    return pltpu.make_async_copy(cache_hbm.at[b, :, h, :], buf_ref.at[slot, g], sem_ref.at[slot, g])


def _prefetch_heads(caches, bufs, sems, hg):
    b, hh = pl.program_id(0), pl.program_id(1)
    nb, ng = pl.num_programs(0), pl.num_programs(1)
    step = b * ng + hh
    slot = lax.rem(step, 2)
    wrap = hh + 1 == ng
    b1 = jnp.where(wrap, b + 1, b)
    hh1 = jnp.where(wrap, 0, hh + 1)

    def copies(bb, hgrp, sl):
        return [_head_copy(c, buf, sem, bb, hgrp * hg + g, sl, g)
                for c, buf, sem in zip(caches, bufs, sems) for g in range(hg)]

    @pl.when(step == 0)
    def _():
        for cp in copies(b, hh, slot):
            cp.start()

    @pl.when(step + 1 < nb * ng)
    def _():
        for cp in copies(b1, hh1, 1 - slot):
            cp.start()

    for cp in copies(b, hh, slot):
        cp.wait()
    return slot


def _sb_sample_kernel(q_ref, kn_ref, vn_ref, kc_hbm, vc_hbm, uu_ref, o_ref, kbuf, vbuf, ksem, vsem, kb_ref, vb_ref,
                      *, ts, past, pad, hg):
    slot = _prefetch_heads((kc_hbm, vc_hbm), (kbuf, vbuf), (ksem, vsem), hg)
    d = D_HEAD
    nk = past + pad
    vis = _iota((ts, nk), 1) < past + _iota((ts, nk), 0)
    uu = uu_ref[...]
    for g in range(hg):
        cols = slice(g * d, (g + 1) * d)
        kb_ref[g, pl.ds(0, past), :] = kbuf[slot, g].astype(BF16)
        vb_ref[g, pl.ds(0, past), :] = vbuf[slot, g].astype(BF16)
        kb_ref[g, pl.ds(past, pad), :] = jnp.zeros((pad, d), BF16)
        vb_ref[g, pl.ds(past, pad), :] = jnp.zeros((pad, d), BF16)
        kb_ref[g, pl.ds(past, ts), :] = kn_ref[:, cols].astype(BF16)
        vb_ref[g, pl.ds(past, ts), :] = vn_ref[:, cols].astype(BF16)
        _, acc = _sb_group((q_ref[:, cols] * (SCALE * LOG2E)).astype(BF16), kb_ref[g], vb_ref[g], uu,
                           jnp.zeros((ts, 1), F32), jnp.zeros((ts, d), F32), vis)
        o_ref[:, cols] = acc.astype(o_ref.dtype)


def sb_attention_sample(q, kn, vn, kc, vc, nb, ts, past, nh, cb=SB_CB, hg=SAMPLE_HEAD_GROUP):
    assert past % cb == 0 and ts <= cb and nh % hg == 0
    new_spec = pl.BlockSpec((ts, hg * D_HEAD), lambda b, h: (b, h))
    hbm = pl.BlockSpec(memory_space=pl.ANY)
    return pl.pallas_call(
        functools.partial(_sb_sample_kernel, ts=ts, past=past, pad=cb, hg=hg),
        grid=(nb, nh // hg),
        in_specs=[new_spec, new_spec, new_spec, hbm, hbm, pl.BlockSpec((2 * cb, cb), lambda b, h: (0, 0))],
        out_specs=new_spec,
        out_shape=jax.ShapeDtypeStruct(q.shape, BF16),
        scratch_shapes=[pltpu.VMEM((2, hg, past, D_HEAD), F32), pltpu.VMEM((2, hg, past, D_HEAD), F32),
                        pltpu.SemaphoreType.DMA((2, hg)), pltpu.SemaphoreType.DMA((2, hg)),
                        pltpu.VMEM((hg, past + cb, D_HEAD), BF16), pltpu.VMEM((hg, past + cb, D_HEAD), BF16)],
        compiler_params=_params("arbitrary", "arbitrary"),
        name="sb_attention_sample",
    )(q, kn, vn, kc, vc, _suffix_matrix(cb))


def _band_bias_kernel(tab_ref, o_ref, *, rows, cols, width):
    h = pl.program_id(0)
    band = BAND_CHUNKS * CHUNK
    x = _iota((8, width), 1)
    idx = jnp.clip(x - rows - band, -REL_CLIP, REL_CLIP) + REL_CLIP

    def body(r, g):
        return jnp.where(idx == r, tab_ref[h, r] * LOG2E, g)

    g = lax.fori_loop(0, 2 * REL_CLIP + 1, body, jnp.zeros((8, width), F32))
    gt = jnp.broadcast_to(g[0:1, :], (rows, width))
    gt = pltpu.roll(gt, 0, 1, stride=1, stride_axis=0)
    bias = gt[:, rows:rows + cols]
    dc = _iota((rows, cols), 1) // CHUNK - _iota((rows, cols), 0) // CHUNK
    o_ref[0] = jnp.where((dc >= 0) & (dc <= BAND_CHUNKS), bias, NEG)


def band_bias(table, qchunks):
    nh = table.shape[0]
    rows = qchunks * CHUNK
    cols = (qchunks + BAND_CHUNKS) * CHUNK
    width = rows + cols
    assert width % LANES == 0 and rows % LANES == 0
    return pl.pallas_call(
        functools.partial(_band_bias_kernel, rows=rows, cols=cols, width=width),
        grid=(nh,),
        in_specs=[pl.BlockSpec(memory_space=pltpu.SMEM)],
        out_specs=pl.BlockSpec((1, rows, cols), lambda h: (h, 0, 0)),
        out_shape=jax.ShapeDtypeStruct((nh, rows, cols), F32),
        compiler_params=_params("arbitrary"),
        name="band_bias",
    )(table)


def _softmax_rows_base2(s2):
    m = jnp.max(s2, axis=-1, keepdims=True)
    p = jnp.exp2(s2 - m)
    return p / jnp.sum(p, axis=-1, keepdims=True)


def _band_prompt_kernel(q_ref, k_ref, v_ref, bias_ref, gq_ref, gk_ref, o_ref, kt_ref, vt_ref, kb_ref, vb_ref,
                        *, seq, rows, cols, keep):
    band = BAND_CHUNKS * CHUNK
    kn = _rms(k_ref[...], gk_ref[...])
    kt_ref[...] = kn[seq - keep:, :]
    vt_ref[...] = v_ref[pl.ds(seq - keep, keep), :]
    kb_ref[pl.ds(0, band), :] = jnp.zeros((band, D_HEAD), BF16)
    vb_ref[pl.ds(0, band), :] = jnp.zeros((band, D_HEAD), BF16)
    kb_ref[pl.ds(band, seq), :] = kn.astype(BF16)
    vb_ref[pl.ds(band, seq), :] = v_ref[...].astype(BF16)
    bias = bias_ref[0]
    gq = gq_ref[...]
    col = _iota((rows, cols), 1)

    def body(t, carry):
        r0 = pl.multiple_of(t * rows, rows)
        q = (_rms(q_ref[pl.ds(r0, rows), :], gq) * (SCALE * LOG2E)).astype(BF16)
        k = kb_ref[pl.ds(r0, cols), :]
        v = vb_ref[pl.ds(r0, cols), :]
        s = lax.dot_general(q, k, (((1,), (1,)), ((), ())), preferred_element_type=F32) + bias
        s = jnp.where(col >= band - r0, s, NEG)
        p = _softmax_rows_base2(s)
        o_ref[pl.ds(r0, rows), :] = jnp.dot(p.astype(BF16), v, preferred_element_type=F32).astype(o_ref.dtype)
        return carry

    lax.fori_loop(0, seq // rows, body, 0, unroll=BAND_UNROLL)


def band_attention_prompt(q, k, v, bias, g_q, g_k, nb, seq, keep):
    nh = q.shape[1] // D_HEAD
    rows, cols = bias.shape[1], bias.shape[2]
    band = BAND_CHUNKS * CHUNK
    blk = pl.BlockSpec((seq, D_HEAD), lambda b, h: (b, h))
    tail = pl.BlockSpec((keep, D_HEAD), lambda b, h: (b, h))
    gspec = pl.BlockSpec((1, D_HEAD), lambda b, h: (0, 0))
    return pl.pallas_call(
        functools.partial(_band_prompt_kernel, seq=seq, rows=rows, cols=cols, keep=keep),
        grid=(nb, nh),
        in_specs=[blk, blk, blk, pl.BlockSpec((1, rows, cols), lambda b, h: (h, 0, 0)), gspec, gspec],
        out_specs=[blk, tail, tail],
        out_shape=[jax.ShapeDtypeStruct(q.shape, BF16),
                   jax.ShapeDtypeStruct((nb * keep, q.shape[1]), F32),
                   jax.ShapeDtypeStruct((nb * keep, q.shape[1]), F32)],
        scratch_shapes=[pltpu.VMEM((band + seq, D_HEAD), BF16), pltpu.VMEM((band + seq, D_HEAD), BF16)],
        compiler_params=_params("arbitrary", "arbitrary"),
        name="band_attention_prompt",
    )(q, k, v, bias, g_q.reshape(1, D_HEAD), g_k.reshape(1, D_HEAD))


def _band_sample_kernel(q_ref, kn_ref, vn_ref, kc_hbm, vc_hbm, bias_ref, gq_ref, gk_ref, o_ref, ko_ref,
                        kbuf, vbuf, ksem, vsem, *, ts, w, cols, hg):
    slot = _prefetch_heads((kc_hbm, vc_hbm), (kbuf, vbuf), (ksem, vsem), hg)
    d = D_HEAD
    gq, gk = gq_ref[...], gk_ref[...]
    zeros = jnp.zeros((cols - w - ts, d), BF16)
    valid = _iota((ts, cols), 1) < w + ts
    for g in range(hg):
        hs = slice(g * d, (g + 1) * d)
        kn = _rms(kn_ref[:, hs], gk)
        ko_ref[:, hs] = kn
        k = jnp.concatenate([kbuf[slot, g].astype(BF16), kn.astype(BF16), zeros], axis=0)
        v = jnp.concatenate([vbuf[slot, g].astype(BF16), vn_ref[:, hs].astype(BF16), zeros], axis=0)
        q = (_rms(q_ref[:, hs], gq) * (SCALE * LOG2E)).astype(BF16)
        s = lax.dot_general(q, k, (((1,), (1,)), ((), ())), preferred_element_type=F32) + bias_ref[g]
        p = _softmax_rows_base2(jnp.where(valid, s, NEG))
        o_ref[:, hs] = jnp.dot(p.astype(BF16), v, preferred_element_type=F32).astype(o_ref.dtype)


def band_attention_sample(q, kn, vn, kc, vc, bias, g_q, g_k, nb, ts, w, nh, hg=SAMPLE_HEAD_GROUP):
    cols = -(-(w + ts) // LANES) * LANES
    assert w == BAND_CHUNKS * CHUNK and ts <= CHUNK and cols <= bias.shape[2] and ts % 16 == 0 and nh % hg == 0
    new_spec = pl.BlockSpec((ts, hg * D_HEAD), lambda b, h: (b, h))
    hbm = pl.BlockSpec(memory_space=pl.ANY)
    gspec = pl.BlockSpec((1, D_HEAD), lambda b, h: (0, 0))
    return pl.pallas_call(
        functools.partial(_band_sample_kernel, ts=ts, w=w, cols=cols, hg=hg),
        grid=(nb, nh // hg),
        in_specs=[new_spec, new_spec, new_spec, hbm, hbm,
                  pl.BlockSpec((hg, ts, cols), lambda b, h: (h, 0, 0)), gspec, gspec],
        out_specs=[new_spec, new_spec],
        out_shape=[jax.ShapeDtypeStruct(q.shape, BF16), jax.ShapeDtypeStruct(q.shape, F32)],
        scratch_shapes=[pltpu.VMEM((2, hg, w, D_HEAD), F32), pltpu.VMEM((2, hg, w, D_HEAD), F32),
                        pltpu.SemaphoreType.DMA((2, hg)), pltpu.SemaphoreType.DMA((2, hg))],
        compiler_params=_params("arbitrary", "arbitrary"),
        name="band_attention_sample",
    )(q, kn, vn, kc, vc, bias, g_q.reshape(1, D_HEAD), g_k.reshape(1, D_HEAD))


def _norm_rope(x, g, cosf, sina, sinb):
    half = ROT_DIM // 2
    y = _rms(x, g)
    return y * cosf + pltpu.roll(y, D_HEAD - half, 1) * sina + pltpu.roll(y, half, 1) * sinb


def rope_tables(pos):
    half = ROT_DIM // 2
    inv_freq = ROPE_THETA ** (-jnp.arange(half, dtype=F32) * (2.0 / ROT_DIM))
    ang = pos.astype(F32)[:, None] * inv_freq[None, :]
    cos, sin = jnp.cos(ang), jnp.sin(ang)
    n = pos.shape[0]
    rest = D_HEAD - ROT_DIM
    cosf = jnp.concatenate([cos, cos, jnp.ones((n, rest), F32)], axis=1)
    sina = jnp.concatenate([-sin, jnp.zeros((n, rest + half), F32)], axis=1)
    sinb = jnp.concatenate([jnp.zeros((n, half), F32), sin, jnp.zeros((n, rest), F32)], axis=1)
    return cosf, sina, sinb


def _online_block(q, k, v, mask, m, l, acc):
    s = lax.dot_general(q, k, (((1,), (1,)), ((), ())), preferred_element_type=F32)
    if mask is not None:
        s = jnp.where(mask, s, NEG)
    m_new = jnp.maximum(m, jnp.max(s, axis=-1, keepdims=True))
    alpha = jnp.exp2(m - m_new)
    p = jnp.exp2(s - m_new)
    l = alpha * l + jnp.sum(p, axis=-1, keepdims=True)
    acc = alpha * acc + jnp.dot(p.astype(BF16), v, preferred_element_type=F32)
    return m_new, l, acc


def _diff_lambda(lq1, lk1, lq2, lk2):
    return (jnp.exp(jnp.sum(lq1 * lk1, axis=-1, keepdims=True))
            - jnp.exp(jnp.sum(lq2 * lk2, axis=-1, keepdims=True)) + LAMBDA_INIT)


def _diff_finish(state, lam, gsub):
    (_, l1, a1), (_, l2, a2) = state
    o = a1 / l1 - lam * (a2 / l2)
    return _rms(o, gsub) * (1.0 - LAMBDA_INIT)


def _diff_init(tq):
    return (jnp.full((tq, 1), NEG, F32), jnp.zeros((tq, 1), F32), jnp.zeros((tq, 2 * D_HEAD), F32))


def _diff_prompt_kernel(q_ref, k_ref, v_ref, cos_ref, sa_ref, sb_ref, gq_ref, gk_ref, lam_ref, gsub_ref,
                        o_ref, ko_hbm, vo_hbm, kf_ref, k1_ref, k2_ref, vb_ref, osem, *, t, tk, tr):
    qi = pl.program_id(2)
    d = D_HEAD

    @pl.when(qi == 0)
    def _():
        gk = gk_ref[...]
        cosf, sina, sinb = cos_ref[...], sa_ref[...], sb_ref[...]
        k1 = _norm_rope(k_ref[:, :d], gk, cosf, sina, sinb)
        k2 = _norm_rope(k_ref[:, d:], gk, cosf, sina, sinb)
        kf_ref[:, :d] = k1
        kf_ref[:, d:] = k2
        k1_ref[...] = k1.astype(BF16)
        k2_ref[...] = k2.astype(BF16)
        vb_ref[...] = v_ref[...].astype(BF16)

    _publish_heads((kf_ref, v_ref), (ko_hbm, vo_hbm), osem)

    r0 = pl.multiple_of(qi * t, t)
    gq = gq_ref[...]
    cosf, sina, sinb = cos_ref[pl.ds(r0, t), :], sa_ref[pl.ds(r0, t), :], sb_ref[pl.ds(r0, t), :]
    q1 = (_norm_rope(q_ref[:, :d], gq, cosf, sina, sinb) * (SCALE * LOG2E)).astype(BF16)
    q2 = (_norm_rope(q_ref[:, d:], gq, cosf, sina, sinb) * (SCALE * LOG2E)).astype(BF16)

    subs = [slice(r * tr, (r + 1) * tr) for r in range(t // tr)]

    def step(k0, state, mask):
        v = vb_ref[pl.ds(k0, tk), :]
        k1 = k1_ref[pl.ds(k0, tk), :]
        k2 = k2_ref[pl.ds(k0, tk), :]
        out = []
        for rows, (st1, st2) in zip(subs, state):
            mr = None if mask is None else mask[rows]
            out.append((_online_block(q1[rows], k1, v, mr, *st1), _online_block(q2[rows], k2, v, mr, *st2)))
        return tuple(out)

    def body(kb, state):
        return step(pl.multiple_of(kb * tk, tk), state, None)

    state = lax.fori_loop(0, qi * (t // tk), body, tuple((_diff_init(tr), _diff_init(tr)) for _ in subs))
    for c in range(t // tk):
        mask = ((c * tk + _iota((t, tk), 1)) // CHUNK) <= (_iota((t, tk), 0) // CHUNK)
        state = step(pl.multiple_of(r0 + c * tk, tk), state, mask)
    lam = _diff_lambda(lam_ref[0:1, :], lam_ref[1:2, :], lam_ref[2:3, :], lam_ref[3:4, :])
    gsub = gsub_ref[...]
    for rows, st in zip(subs, state):
        o_ref[rows, :] = _diff_finish(st, lam, gsub).astype(o_ref.dtype)


def diff_attention_prompt(q, k, v, tabs, g_q, g_k, lam4, g_sub, nb, seq, t=DIFF_T, tk=DIFF_TK, tr=DIFF_TR):
    d2 = 2 * D_HEAD
    nh = q.shape[1] // d2
    assert t % tk == 0 and tk % CHUNK == 0 and t % tr == 0
    kv_spec = pl.BlockSpec((seq, d2), lambda b, h, qi: (b, h))
    q_spec = pl.BlockSpec((t, d2), lambda b, h, qi: (b * (seq // t) + qi, h))
    tab_spec = pl.BlockSpec((seq, D_HEAD), lambda b, h, qi: (0, 0))
    g_spec = pl.BlockSpec((1, D_HEAD), lambda b, h, qi: (0, 0))
    hbm = pl.BlockSpec(memory_space=pl.ANY)
    cache = jax.ShapeDtypeStruct((nb, seq, nh, d2), F32)
    return pl.pallas_call(
        functools.partial(_diff_prompt_kernel, t=t, tk=tk, tr=tr),
        grid=(nb, nh, seq // t),
        in_specs=[q_spec, kv_spec, kv_spec, tab_spec, tab_spec, tab_spec, g_spec, g_spec,
                  pl.BlockSpec((4, D_HEAD), lambda b, h, qi: (0, 0)),
                  pl.BlockSpec((1, d2), lambda b, h, qi: (0, 0))],
        out_specs=[q_spec, hbm, hbm],
        out_shape=[jax.ShapeDtypeStruct(q.shape, BF16), cache, cache],
        scratch_shapes=[pltpu.VMEM((seq, d2), F32), pltpu.VMEM((seq, D_HEAD), BF16), pltpu.VMEM((seq, D_HEAD), BF16),
                        pltpu.VMEM((seq, d2), BF16), pltpu.SemaphoreType.DMA((2,))],
        compiler_params=_params("arbitrary", "arbitrary", "arbitrary"),
        name="diff_attention_prompt",
    )(q, k, v, *tabs, g_q.reshape(1, D_HEAD), g_k.reshape(1, D_HEAD), lam4, g_sub.reshape(1, d2))


def _diff_sample_kernel(q_ref, kn_ref, vn_ref, kc_hbm, vc_hbm, cos_ref, sa_ref, sb_ref, gq_ref, gk_ref,
                        lam_ref, gsub_ref, o_ref, ko_ref, kbuf, vbuf, ksem, vsem, k1_ref, k2_ref, vb_ref,
                        *, ts, past, tn, hg):
    slot = _prefetch_heads((kc_hbm, vc_hbm), (kbuf, vbuf), (ksem, vsem), hg)
    d = D_HEAD
    cosf, sina, sinb = cos_ref[...], sa_ref[...], sb_ref[...]
    gq, gk = gq_ref[...], gk_ref[...]
    lam = _diff_lambda(lam_ref[0:1, :], lam_ref[1:2, :], lam_ref[2:3, :], lam_ref[3:4, :])
    gsub = gsub_ref[...]
    nk = past + tn
    kpos = _iota((ts, nk), 1)
    qpos = past + _iota((ts, nk), 0)
    mask = ((kpos // CHUNK) <= (qpos // CHUNK)) & (kpos < past + ts)
    for g in range(hg):
        c1, c2 = slice(2 * g * d, (2 * g + 1) * d), slice((2 * g + 1) * d, (2 * g + 2) * d)
        cv = slice(2 * g * d, (2 * g + 2) * d)
        k1 = _norm_rope(kn_ref[:, c1], gk, cosf, sina, sinb)
        k2 = _norm_rope(kn_ref[:, c2], gk, cosf, sina, sinb)
        ko_ref[:, c1] = k1
        ko_ref[:, c2] = k2
        k1_ref[g, pl.ds(0, past), :] = kbuf[slot, g, :, :d].astype(BF16)
        k2_ref[g, pl.ds(0, past), :] = kbuf[slot, g, :, d:].astype(BF16)
        vb_ref[g, pl.ds(0, past), :] = vbuf[slot, g].astype(BF16)
        k1_ref[g, pl.ds(past, tn), :] = jnp.zeros((tn, d), BF16)
        k2_ref[g, pl.ds(past, tn), :] = jnp.zeros((tn, d), BF16)
        vb_ref[g, pl.ds(past, tn), :] = jnp.zeros((tn, 2 * d), BF16)
        k1_ref[g, pl.ds(past, ts), :] = k1.astype(BF16)
        k2_ref[g, pl.ds(past, ts), :] = k2.astype(BF16)
        vb_ref[g, pl.ds(past, ts), :] = vn_ref[:, cv].astype(BF16)
        q1 = (_norm_rope(q_ref[:, c1], gq, cosf, sina, sinb) * (SCALE * LOG2E)).astype(BF16)
        q2 = (_norm_rope(q_ref[:, c2], gq, cosf, sina, sinb) * (SCALE * LOG2E)).astype(BF16)
        v = vb_ref[g]
        state = (_online_block(q1, k1_ref[g], v, mask, *_diff_init(ts)),
                 _online_block(q2, k2_ref[g], v, mask, *_diff_init(ts)))
        o_ref[:, cv] = _diff_finish(state, lam, gsub).astype(o_ref.dtype)


def diff_attention_sample(q, kn, vn, kc, vc, tabs, g_q, g_k, lam4, g_sub, nb, ts, past, nh, tn=LANES,
                          hg=SAMPLE_HEAD_GROUP):
    d2 = 2 * D_HEAD
    assert past % LANES == 0 and ts <= tn and nh % hg == 0
    new_spec = pl.BlockSpec((ts, hg * d2), lambda b, h: (b, h))
    hbm = pl.BlockSpec(memory_space=pl.ANY)
    tab_spec = pl.BlockSpec((ts, D_HEAD), lambda b, h: (0, 0))
    g_spec = pl.BlockSpec((1, D_HEAD), lambda b, h: (0, 0))
    return pl.pallas_call(
        functools.partial(_diff_sample_kernel, ts=ts, past=past, tn=tn, hg=hg),
        grid=(nb, nh // hg),
        in_specs=[new_spec, new_spec, new_spec, hbm, hbm, tab_spec, tab_spec, tab_spec,
                  g_spec, g_spec, pl.BlockSpec((4, D_HEAD), lambda b, h: (0, 0)),
                  pl.BlockSpec((1, d2), lambda b, h: (0, 0))],
        out_specs=[new_spec, new_spec],
        out_shape=[jax.ShapeDtypeStruct(q.shape, BF16), jax.ShapeDtypeStruct(q.shape, F32)],
        scratch_shapes=[pltpu.VMEM((2, hg, past, d2), F32), pltpu.VMEM((2, hg, past, d2), F32),
                        pltpu.SemaphoreType.DMA((2, hg)), pltpu.SemaphoreType.DMA((2, hg)),
                        pltpu.VMEM((hg, past + tn, D_HEAD), BF16), pltpu.VMEM((hg, past + tn, D_HEAD), BF16),
                        pltpu.VMEM((hg, past + tn, d2), BF16)],
        compiler_params=_params("arbitrary", "arbitrary"),
        name="diff_attention_sample",
    )(q, kn, vn, kc, vc, *tabs, g_q.reshape(1, D_HEAD), g_k.reshape(1, D_HEAD), lam4, g_sub.reshape(1, d2))


def _to_time_major(x, nb, ts):
    return x.reshape(nb, ts, -1).transpose(1, 0, 2).reshape(nb * ts, -1)


def _to_batch_major(x, nb, ts):
    return x.reshape(ts, nb, -1).transpose(1, 0, 2).reshape(nb * ts, -1)


def _conv_glu_layer(hp, hs, g_norm, w_up, w_conv, w_down, layer, state, nb_s, ts, seq):
    mp = hp.shape[0]
    hs_t = _to_time_major(hs, nb_s, ts)
    xn = rmsnorm_rows(hp, hs_t, g_norm)
    hidden, conv_p, conv_s_t = up_convglu(xn, w_up, w_conv, layer, state.transpose(1, 0, 2), mp, seq)
    hp, hs_t = down_project(hidden, w_down, layer, hp, hs_t)
    return hp, _to_batch_major(hs_t, nb_s, ts), conv_p, conv_s_t.transpose(1, 0, 2)


def kernel(x_prompt, x_sample, cache_k_sb, cache_v_sb, cache_k_band, cache_v_band, cache_k_diff, cache_v_diff, state_conv_ffn, norm_mix, norm_ffn, w_in_ab, w_out_ab, g_q_band, g_k_band, rel_bias_band, w_in_diff, w_out_diff, g_q_diff, g_k_diff, lambda_q1, lambda_k1, lambda_q2, lambda_k2, g_sub_diff, w_up, w_conv, w_down):
    bp, tp, dm = x_prompt.shape
    bs, ts, _ = x_sample.shape
    past = cache_k_sb.shape[1]
    h_sb, h_cb, h_df = cache_k_sb.shape[2], cache_k_band.shape[2], cache_k_diff.shape[2]
    d_sb, d_cb, d_df = h_sb * D_HEAD, h_cb * D_HEAD, h_df * 2 * D_HEAD
    w_band = cache_k_band.shape[1]
    keep = min(BAND_CHUNKS * CHUNK, tp)
    mp, ms = bp * tp, bs * ts

    hp = x_prompt.reshape(mp, dm)
    hs = x_sample.reshape(ms, dm)

    xn = rmsnorm_rows(hp, hs, norm_mix[0])
    sec = []
    col = 0
    for width in (d_sb, d_sb, d_sb, d_cb, d_cb, d_cb):
        sec.append(project(xn, w_in_ab, col, width, mp))
        col += width
    (qa_p, qa_s), (ka_p, ka_s), (va_p, va_s), (qb_p, qb_s), (kb_p, kb_s), (vb_p, vb_s) = sec

    oa_p, k_sb_p, v_sb_p = sb_attention_prompt(qa_p, ka_p, va_p, bp, tp)
    oa_s = sb_attention_sample(qa_s, ka_s, va_s, cache_k_sb, cache_v_sb, bs, ts, past, h_sb)
    bias = band_bias(rel_bias_band, BAND_QCHUNKS)
    ob_p, k_band_p, v_band_p = band_attention_prompt(qb_p, kb_p, vb_p, bias, g_q_band, g_k_band, bp, tp, keep)
    ob_s, k_band_s = band_attention_sample(qb_s, kb_s, vb_s, cache_k_band, cache_v_band, bias,
                                           g_q_band, g_k_band, bs, ts, w_band, h_cb)
    hp, hs = out_project([oa_p, ob_p], [oa_s, ob_s], w_out_ab, hp, hs)
    hp, hs, conv_p0, conv_s0 = _conv_glu_layer(hp, hs, norm_ffn[0], w_up, w_conv, w_down, 0,
                                               state_conv_ffn[0], bs, ts, tp)

    xn = rmsnorm_rows(hp, hs, norm_mix[1])
    (q_p, q_s), (k_p, k_s), (v_p, v_s) = [project(xn, w_in_diff, c * d_df, d_df, mp) for c in range(3)]
    lam4 = jnp.stack([lambda_q1, lambda_k1, lambda_q2, lambda_k2]).astype(F32)
    o_p, k_diff_p, v_diff_p = diff_attention_prompt(q_p, k_p, v_p, rope_tables(jnp.arange(tp)), g_q_diff,
                                                    g_k_diff, lam4, g_sub_diff, bp, tp)
    o_s, k_diff_s = diff_attention_sample(q_s, k_s, v_s, cache_k_diff, cache_v_diff,
                                          rope_tables(past + jnp.arange(ts)), g_q_diff, g_k_diff,
                                          lam4, g_sub_diff, bs, ts, past, h_df)
    hp, hs = out_project([o_p], [o_s], w_out_diff, hp, hs)
    hp, hs, conv_p1, conv_s1 = _conv_glu_layer(hp, hs, norm_ffn[1], w_up, w_conv, w_down, 1,
                                               state_conv_ffn[1], bs, ts, tp)

    return (hp.reshape(bp, tp, dm), hs.reshape(bs, ts, dm),
            k_sb_p, v_sb_p,
            k_band_p.reshape(bp, keep, h_cb, D_HEAD), v_band_p.reshape(bp, keep, h_cb, D_HEAD),
            k_diff_p, v_diff_p,
            jnp.stack([conv_p0, conv_p1]),
            ka_s.reshape(bs, ts, h_sb, D_HEAD), va_s.reshape(bs, ts, h_sb, D_HEAD),
            k_band_s.reshape(bs, ts, h_cb, D_HEAD), vb_s.reshape(bs, ts, h_cb, D_HEAD),
            k_diff_s.reshape(bs, ts, h_df, 2 * D_HEAD), v_s.reshape(bs, ts, h_df, 2 * D_HEAD),
            jnp.stack([conv_s0, conv_s1]))
```

```python
import functools
import math

import jax
import jax.numpy as jnp
from jax import lax
from jax.experimental import pallas as pl
from jax.experimental.pallas import tpu as pltpu

F32 = jnp.float32
BF16 = jnp.bfloat16

CHUNK = 64
D_HEAD = 128
BAND_CHUNKS = 8
REL_CLIP = 128
ROT_DIM = D_HEAD // 4
ROPE_THETA = 500000.0
CONV_W = 3
EPS = 1e-6
NEG = -1e30
LAMBDA_INIT = 0.8 - 0.6 * math.exp(-0.3 * 1)
SCALE = 1.0 / math.sqrt(D_HEAD)
LOG2E = math.log2(math.e)

V7X_VMEM_LIMIT_BYTES = 56 * 1024 * 1024
LANES = 128

ROW_TILE = 1024
PROJ_ROW_TILE = 512
PROJ_COL_TILE = 1024
OUT_ROW_TILE = 1024
UP_COL_BLOCK = 256
UP_ROW_CHUNK = 512
DOWN_ROW_TILE = 256
SB_T = 512
SB_CB = 256
BAND_QCHUNKS = 4
BAND_UNROLL = 8
SAMPLE_HEAD_GROUP = 4
SAMPLE_HEAD_GROUP_128 = 8
DIFF_T = 512
DIFF_TK = 512
DIFF_TR = 128


def _params(*sem):
    return pltpu.CompilerParams(dimension_semantics=sem, vmem_limit_bytes=V7X_VMEM_LIMIT_BYTES)


def _iota(shape, dim):
    return lax.broadcasted_iota(jnp.int32, shape, dim)


def _rms(x, g):
    return x * lax.rsqrt(jnp.mean(x * x, axis=-1, keepdims=True) + EPS) * g


def _rmsnorm_kernel(xp_ref, xs_ref, g_ref, o_ref, *, np_tiles):
    i = pl.program_id(0)

    @pl.when(i < np_tiles)
    def _():
        o_ref[...] = _rms(xp_ref[...], g_ref[...]).astype(o_ref.dtype)

    @pl.when(i >= np_tiles)
    def _():
        o_ref[...] = _rms(xs_ref[...], g_ref[...]).astype(o_ref.dtype)


def rmsnorm_rows(xp, xs, g, tm=256):
    mp, d = xp.shape
    ms = xs.shape[0]
    np_tiles, ns_tiles = mp // tm, ms // tm
    return pl.pallas_call(
        functools.partial(_rmsnorm_kernel, np_tiles=np_tiles),
        grid=(np_tiles + ns_tiles,),
        in_specs=[
            pl.BlockSpec((tm, d), lambda i: (jnp.minimum(i, np_tiles - 1), 0)),
            pl.BlockSpec((tm, d), lambda i: (jnp.maximum(i - np_tiles, 0), 0)),
            pl.BlockSpec((1, d), lambda i: (0, 0)),
        ],
        out_specs=pl.BlockSpec((tm, d), lambda i: (i, 0)),
        out_shape=jax.ShapeDtypeStruct((mp + ms, d), BF16),
        compiler_params=_params("arbitrary"),
        name="rmsnorm_rows",
    )(xp, xs, g.reshape(1, d))


def _refresh_weights(copies, wf_ref, wb_ref):
    jj, i = pl.program_id(0), pl.program_id(1)
    tb = wf_ref.shape[2]

    @pl.when(i == 0)
    def _():
        @pl.when(jj == 0)
        def _():
            for cp in copies(jj):
                cp.start()

        for cp in copies(jj):
            cp.wait()
        for n in range(wf_ref.shape[0]):
            wb_ref[:, n * tb:(n + 1) * tb] = wf_ref[n].astype(BF16)

        @pl.when(jj + 1 < pl.num_programs(0))
        def _():
            for cp in copies(jj + 1):
                cp.start()


def _proj_kernel(x_ref, w_hbm, op_ref, os_ref, wf_ref, wsem, wb_ref, *, col0, np_tiles, ms):
    i = pl.program_id(1)
    tn = wf_ref.shape[2]

    def copies(jj):
        c0 = pl.multiple_of(col0 + jj * tn, tn)
        return [pltpu.make_async_copy(w_hbm.at[:, pl.ds(c0, tn)], wf_ref.at[0], wsem.at[0])]

    _refresh_weights(copies, wf_ref, wb_ref)

    @pl.when(i < np_tiles)
    def _():
        op_ref[...] = jnp.dot(x_ref[...], wb_ref[...], preferred_element_type=F32)

    @pl.when(i >= np_tiles)
    def _():
        os_ref[...] = jnp.dot(x_ref[0:ms, :], wb_ref[...], preferred_element_type=F32)


def project(x, w, col0, n, mp, tm=PROJ_ROW_TILE, tn=PROJ_COL_TILE):
    m, k = x.shape
    ms = m - mp
    assert mp % tm == 0 and 0 < ms <= tm and n % tn == 0 and col0 % tn == 0
    np_tiles = mp // tm
    return pl.pallas_call(
        functools.partial(_proj_kernel, col0=col0, np_tiles=np_tiles, ms=ms),
        grid=(n // tn, np_tiles + 1),
        in_specs=[
            pl.BlockSpec((tm, k), lambda j, i: (i, 0)),
            pl.BlockSpec(memory_space=pl.ANY),
        ],
        out_specs=[
            pl.BlockSpec((tm, tn), lambda j, i: (jnp.minimum(i, np_tiles - 1), j)),
            pl.BlockSpec((ms, tn), lambda j, i: (0, j)),
        ],
        out_shape=[jax.ShapeDtypeStruct((mp, n), F32), jax.ShapeDtypeStruct((ms, n), F32)],
        scratch_shapes=[pltpu.VMEM((1, k, tn), F32), pltpu.SemaphoreType.DMA((1,)), pltpu.VMEM((k, tn), BF16)],
        compiler_params=_params("arbitrary", "arbitrary"),
        name="project",
    )(x, w)


def _outproj_kernel(*refs, np_tiles, nparts):
    ap = refs[0:nparts]
    a_s = refs[nparts:2 * nparts]
    w_hbm, rp_ref, rs_ref, op_ref, os_ref, wf_ref, wsem, wb_ref = refs[2 * nparts:]
    i = pl.program_id(1)
    kp, tn = wf_ref.shape[1], wf_ref.shape[2]

    def copies(jj):
        c0 = pl.multiple_of(jj * tn, tn)
        return [pltpu.make_async_copy(w_hbm.at[pl.ds(p * kp, kp), pl.ds(c0, tn)], wf_ref.at[p], wsem.at[p])
                for p in range(nparts)]

    _refresh_weights(copies, wf_ref, wb_ref)

    def run(a_refs, r_ref, o_ref):
        acc = r_ref[...]
        for p in range(nparts):
            acc = acc + jnp.dot(a_refs[p][...], wb_ref[:, p * tn:(p + 1) * tn], preferred_element_type=F32)
        o_ref[...] = acc

    @pl.when(i < np_tiles)
    def _():
        run(ap, rp_ref, op_ref)

    @pl.when(i >= np_tiles)
    def _():
        run(a_s, rs_ref, os_ref)


def out_project(parts_p, parts_s, w, res_p, res_s, tm=OUT_ROW_TILE, tn=512):
    nparts = len(parts_p)
    mp, kp = parts_p[0].shape
    ms = parts_s[0].shape[0]
    n = w.shape[1]
    assert mp % tm == 0 and n % tn == 0 and w.shape[0] == nparts * kp
    np_tiles = mp // tm
    pmap = lambda j, i: (jnp.minimum(i, np_tiles - 1), 0)
    pmap_o = lambda j, i: (jnp.minimum(i, np_tiles - 1), j)
    in_specs = ([pl.BlockSpec((tm, kp), pmap)] * nparts + [pl.BlockSpec((ms, kp), lambda j, i: (0, 0))] * nparts
                + [pl.BlockSpec(memory_space=pl.ANY),
                   pl.BlockSpec((tm, tn), pmap_o), pl.BlockSpec((ms, tn), lambda j, i: (0, j))])
    return pl.pallas_call(
        functools.partial(_outproj_kernel, np_tiles=np_tiles, nparts=nparts),
        grid=(n // tn, np_tiles + 1),
        in_specs=in_specs,
        out_specs=[pl.BlockSpec((tm, tn), pmap_o), pl.BlockSpec((ms, tn), lambda j, i: (0, j))],
        out_shape=[jax.ShapeDtypeStruct((mp, n), F32), jax.ShapeDtypeStruct((ms, n), F32)],
        scratch_shapes=[pltpu.VMEM((nparts, kp, tn), F32), pltpu.SemaphoreType.DMA((nparts,)),
                        pltpu.VMEM((kp, nparts * tn), BF16)],
        compiler_params=_params("arbitrary", "arbitrary"),
        name="out_project",
    )(*parts_p, *parts_s, w, res_p, res_s)


def _silu_mul(gc, u):
    return gc * (1.0 / (1.0 + jnp.exp(-gc))) * u


def _up_weight_copies(w_hbm, wf_ref, wsem, jj, *, layer, nj, tb):
    j0 = 2 * jj
    j1 = jnp.minimum(j0 + 1, nj - 1)
    cols = (j0, j1, nj + j0, nj + j1)
    return [pltpu.make_async_copy(w_hbm.at[layer, :, pl.ds(pl.multiple_of(c * tb, tb), tb)], wf_ref.at[n], wsem.at[n])
            for n, c in enumerate(cols)]


def _up_kernel(x_ref, w_hbm, wc_ref, hist_ref, h_ref, cp_ref, cs_ref, wf_ref, wsem, wb_ref, carry_ref,
               *, layer, nj, tb, np_tiles, tiles_per_seq, nb_s, ms, rc):
    i = pl.program_id(1)
    tm = x_ref.shape[0]
    tn = 2 * tb
    _refresh_weights(functools.partial(_up_weight_copies, w_hbm, wf_ref, wsem, layer=layer, nj=nj, tb=tb),
                     wf_ref, wb_ref)

    w0 = wc_ref[0:1, :]
    w1 = wc_ref[1:2, :]
    w2 = wc_ref[2:3, :]

    @pl.when(lax.rem(i, tiles_per_seq) == 0)
    def _():
        carry_ref[...] = jnp.zeros_like(carry_ref)

    @pl.when(i < np_tiles)
    def _():
        c0 = carry_ref[0:1, :]
        c1 = carry_ref[1:2, :]
        row = _iota((rc, tn), 0)
        for r in range(tm // rc):
            rows = slice(r * rc, (r + 1) * rc)
            gu = jnp.dot(x_ref[rows, :], wb_ref[...], preferred_element_type=F32)
            g = gu[:, :tn]
            u = gu[:, tn:]
            gm1 = jnp.where(row == 0, c1, pltpu.roll(g, 1, 0))
            gm2 = jnp.where(row == 0, c0, jnp.where(row == 1, c1, pltpu.roll(g, 2, 0)))
            gc = gm2 * w0 + gm1 * w1 + g * w2
            h_ref[rows, :] = _silu_mul(gc, u).astype(h_ref.dtype)
            c0 = g[rc - 2:rc - 1, :]
            c1 = g[rc - 1:rc, :]
        tail = jnp.concatenate([c0, c1], axis=0)
        carry_ref[0:2, :] = tail
        cp_ref[0] = tail

    @pl.when(i >= np_tiles)
    def _():
        gu = jnp.dot(x_ref[0:ms, :], wb_ref[...], preferred_element_type=F32)
        g = gu[:, :tn]
        u = gu[:, tn:]
        h0 = hist_ref[0]
        h1 = hist_ref[1]
        gm1 = jnp.concatenate([h1, g[:ms - nb_s]], axis=0)
        gm2 = jnp.concatenate([h0, h1, g[:ms - 2 * nb_s]], axis=0)
        gc = gm2 * w0 + gm1 * w1 + g * w2
        h_ref[0:ms, :] = _silu_mul(gc, u).astype(h_ref.dtype)
        cs_ref[0] = g[ms - 2 * nb_s:ms - nb_s]
        cs_ref[1] = g[ms - nb_s:]


def up_convglu(x, w_up, w_conv, layer, hist_t, mp, seq_len, tm=ROW_TILE, tb=UP_COL_BLOCK, rc=UP_ROW_CHUNK):
    m, k = x.shape
    f = w_up.shape[2] // 2
    nb_s = hist_t.shape[1]
    nb_p = mp // seq_len
    np_tiles = mp // tm
    ms = m - mp
    assert 0 < ms <= tm and seq_len % tm == 0 and tm % rc == 0 and f % tb == 0 and CONV_W == 3
    tiles_per_seq = seq_len // tm
    nj = f // tb
    tn = 2 * tb
    njj = -(-nj // 2)
    return pl.pallas_call(
        functools.partial(_up_kernel, layer=layer, nj=nj, tb=tb, np_tiles=np_tiles, tiles_per_seq=tiles_per_seq,
                          nb_s=nb_s, ms=ms, rc=rc),
        grid=(njj, np_tiles + 1),
        in_specs=[
            pl.BlockSpec((tm, k), lambda j, i: (i, 0)),
            pl.BlockSpec(memory_space=pl.ANY),
            pl.BlockSpec((None, CONV_W, tn), lambda j, i: (layer, 0, j)),
            pl.BlockSpec((CONV_W - 1, nb_s, tn), lambda j, i: (0, 0, j)),
        ],
        out_specs=[
            pl.BlockSpec((tm, tn), lambda j, i: (i, j)),
            pl.BlockSpec((1, CONV_W - 1, tn), lambda j, i: (jnp.minimum(i // tiles_per_seq, nb_p - 1), 0, j)),
            pl.BlockSpec((CONV_W - 1, nb_s, tn), lambda j, i: (0, 0, j)),
        ],
        out_shape=[
            jax.ShapeDtypeStruct((m, f), BF16),
            jax.ShapeDtypeStruct((nb_p, CONV_W - 1, f), F32),
            jax.ShapeDtypeStruct((CONV_W - 1, nb_s, f), F32),
        ],
        scratch_shapes=[pltpu.VMEM((4, k, tb), F32), pltpu.SemaphoreType.DMA((4,)),
                        pltpu.VMEM((k, 2 * tn), BF16), pltpu.VMEM((8, tn), F32)],
        compiler_params=_params("arbitrary", "arbitrary"),
        name="up_convglu",
    )(x, w_up, w_conv, hist_t)


def _down_kernel(a_ref, w_hbm, rp_ref, rs_ref, op_ref, os_ref, wf_ref, wsem, wb_ref, *, layer, np_tiles):
    i = pl.program_id(1)
    tn = wf_ref.shape[2]

    def copies(jj):
        c0 = pl.multiple_of(jj * tn, tn)
        return [pltpu.make_async_copy(w_hbm.at[layer, :, pl.ds(c0, tn)], wf_ref.at[0], wsem.at[0])]

    _refresh_weights(copies, wf_ref, wb_ref)

    @pl.when(i < np_tiles)
    def _():
        op_ref[...] = rp_ref[...] + jnp.dot(a_ref[...], wb_ref[...], preferred_element_type=F32)

    @pl.when(i >= np_tiles)
    def _():
        os_ref[...] = rs_ref[...] + jnp.dot(a_ref[...], wb_ref[...], preferred_element_type=F32)


def down_project(a, w, layer, res_p, res_s, tm=DOWN_ROW_TILE, tn=512):
    m, f = a.shape
    n = w.shape[2]
    mp = res_p.shape[0]
    ms = m - mp
    np_tiles, ns_tiles = mp // tm, ms // tm
    pmap_o = lambda j, i: (jnp.minimum(i, np_tiles - 1), j)
    smap_o = lambda j, i: (jnp.maximum(i - np_tiles, 0), j)
    return pl.pallas_call(
        functools.partial(_down_kernel, layer=layer, np_tiles=np_tiles),
        grid=(n // tn, np_tiles + ns_tiles),
        in_specs=[
            pl.BlockSpec((tm, f), lambda j, i: (i, 0)),
            pl.BlockSpec(memory_space=pl.ANY),
            pl.BlockSpec((tm, tn), pmap_o),
            pl.BlockSpec((tm, tn), smap_o),
        ],
        out_specs=[pl.BlockSpec((tm, tn), pmap_o), pl.BlockSpec((tm, tn), smap_o)],
        out_shape=[jax.ShapeDtypeStruct((mp, n), F32), jax.ShapeDtypeStruct((ms, n), F32)],
        scratch_shapes=[pltpu.VMEM((1, f, tn), F32), pltpu.SemaphoreType.DMA((1,)), pltpu.VMEM((f, tn), BF16)],
        compiler_params=_params("arbitrary", "arbitrary"),
        name="down_project",
    )(a, w, res_p, res_s)


def _sb_group(q, k, v, uu, run, acc, vis):
    nk = k.shape[0]
    cb = uu.shape[1]
    z2 = lax.dot_general(q, k, (((1,), (1,)), ((), ())), preferred_element_type=F32)
    s = jnp.maximum(z2, 0.0) + jnp.log2(1.0 + jnp.exp2(-jnp.abs(z2)))
    if vis is not None:
        s = jnp.where(vis, s, 0.0)
    hi = s.astype(BF16)
    lo = (s - hi.astype(F32)).astype(BF16)
    afters = []
    for blk in reversed(range(nk // cb)):
        sl = slice(blk * cb, (blk + 1) * cb)
        cs = jnp.dot(jnp.concatenate([hi[:, sl], lo[:, sl]], axis=1), uu, preferred_element_type=F32)
        afters.append(cs + run)
        run = run + jnp.sum(s[:, sl], axis=-1, keepdims=True)
    after = jnp.concatenate(afters[::-1], axis=1)
    w = jnp.exp2(z2 - s - after)
    if vis is not None:
        w = jnp.where(vis, w, 0.0)
    acc = acc + jnp.dot(w.astype(BF16), v, preferred_element_type=F32)
    return run, acc


def _publish_heads(srcs, outs_hbm, sem_ref):
    b, h, qi = pl.program_id(0), pl.program_id(1), pl.program_id(2)
    copies = [pltpu.make_async_copy(src, out.at[b, :, h, :], sem_ref.at[n])
              for n, (src, out) in enumerate(zip(srcs, outs_hbm))]

    @pl.when(qi == 0)
    def _():
        for cp in copies:
            cp.start()

    @pl.when(qi == pl.num_programs(2) - 1)
    def _():
        for cp in copies:
            cp.wait()


def _sb_prompt_kernel(q_ref, k_ref, v_ref, uu_ref, o_ref, ko_hbm, vo_hbm, kb_ref, vb_ref, osem, *, t):
    qi = pl.program_id(2)

    @pl.when(qi == 0)
    def _():
        kb_ref[...] = k_ref[...].astype(BF16)
        vb_ref[...] = v_ref[...].astype(BF16)

    _publish_heads((k_ref, v_ref), (ko_hbm, vo_hbm), osem)

    q = (q_ref[...] * (SCALE * LOG2E)).astype(BF16)
    uu = uu_ref[...]
    r0 = pl.multiple_of(qi * t, t)
    run = jnp.zeros((t, 1), F32)
    acc = jnp.zeros((t, D_HEAD), F32)
    vis = _iota((t, t), 1) < _iota((t, t), 0)
    run, acc = _sb_group(q, kb_ref[pl.ds(r0, t), :], vb_ref[pl.ds(r0, t), :], uu, run, acc, vis)

    def body(g, carry):
        k0 = pl.multiple_of((qi - 1 - g) * t, t)
        return _sb_group(q, kb_ref[pl.ds(k0, t), :], vb_ref[pl.ds(k0, t), :], uu, carry[0], carry[1], None)

    run, acc = lax.fori_loop(0, qi, body, (run, acc))
    o_ref[...] = acc.astype(o_ref.dtype)


def _suffix_matrix(cb):
    j = jnp.arange(cb)[:, None]
    s = jnp.arange(cb)[None, :]
    u = (j > s).astype(BF16)
    return jnp.concatenate([u, u], axis=0)


def sb_attention_prompt(q, k, v, nb, seq, t=SB_T, cb=SB_CB):
    nh = q.shape[1] // D_HEAD
    assert seq % t == 0 and t % cb == 0
    kv_spec = pl.BlockSpec((seq, D_HEAD), lambda b, h, qi: (b, h))
    q_spec = pl.BlockSpec((t, D_HEAD), lambda b, h, qi: (b * (seq // t) + qi, h))
    hbm = pl.BlockSpec(memory_space=pl.ANY)
    cache = jax.ShapeDtypeStruct((nb, seq, nh, D_HEAD), F32)
    return pl.pallas_call(
        functools.partial(_sb_prompt_kernel, t=t),
        grid=(nb, nh, seq // t),
        in_specs=[q_spec, kv_spec, kv_spec, pl.BlockSpec((2 * cb, cb), lambda b, h, qi: (0, 0))],
        out_specs=[q_spec, hbm, hbm],
        out_shape=[jax.ShapeDtypeStruct(q.shape, BF16), cache, cache],
        scratch_shapes=[pltpu.VMEM((seq, D_HEAD), BF16), pltpu.VMEM((seq, D_HEAD), BF16),
                        pltpu.SemaphoreType.DMA((2,))],
        compiler_params=_params("arbitrary", "arbitrary", "arbitrary"),
        name="sb_attention_prompt",
    )(q, k, v, _suffix_matrix(cb))


def _head_copy(cache_hbm, buf_ref, sem_ref, b, h, slot, g):
    return pltpu.make_async_copy(cache_hbm.at[b, :, h, :], buf_ref.at[slot, g], sem_ref.at[slot, g])


def _prefetch_heads(caches, bufs, sems, hg):
    b, hh = pl.program_id(0), pl.program_id(1)
    nb, ng = pl.num_programs(0), pl.num_programs(1)
    step = b * ng + hh
    slot = lax.rem(step, 2)
    wrap = hh + 1 == ng
    b1 = jnp.where(wrap, b + 1, b)
    hh1 = jnp.where(wrap, 0, hh + 1)

    def copies(bb, hgrp, sl):
        return [_head_copy(c, buf, sem, bb, hgrp * hg + g, sl, g)
                for c, buf, sem in zip(caches, bufs, sems) for g in range(hg)]

    @pl.when(step == 0)
    def _():
        for cp in copies(b, hh, slot):
            cp.start()

    @pl.when(step + 1 < nb * ng)
    def _():
        for cp in copies(b1, hh1, 1 - slot):
            cp.start()

    for cp in copies(b, hh, slot):
        cp.wait()
    return slot


def _sb_sample_kernel(q_ref, kn_ref, vn_ref, kc_hbm, vc_hbm, uu_ref, o_ref, kbuf, vbuf, ksem, vsem, kb_ref, vb_ref,
                      *, ts, past, pad, hg):
    slot = _prefetch_heads((kc_hbm, vc_hbm), (kbuf, vbuf), (ksem, vsem), hg)
    d = D_HEAD
    nk = past + pad
    vis = _iota((ts, nk), 1) < past + _iota((ts, nk), 0)
    uu = uu_ref[...]
    for g in range(hg):
        cols = slice(g * d, (g + 1) * d)
        kb_ref[g, pl.ds(0, past), :] = kbuf[slot, g].astype(BF16)
        vb_ref[g, pl.ds(0, past), :] = vbuf[slot, g].astype(BF16)
        kb_ref[g, pl.ds(past, pad), :] = jnp.zeros((pad, d), BF16)
        vb_ref[g, pl.ds(past, pad), :] = jnp.zeros((pad, d), BF16)
        kb_ref[g, pl.ds(past, ts), :] = kn_ref[:, cols].astype(BF16)
        vb_ref[g, pl.ds(past, ts), :] = vn_ref[:, cols].astype(BF16)
        _, acc = _sb_group((q_ref[:, cols] * (SCALE * LOG2E)).astype(BF16), kb_ref[g], vb_ref[g], uu,
                           jnp.zeros((ts, 1), F32), jnp.zeros((ts, d), F32), vis)
        o_ref[:, cols] = acc.astype(o_ref.dtype)


def sb_attention_sample(q, kn, vn, kc, vc, nb, ts, past, nh, cb=SB_CB, hg=SAMPLE_HEAD_GROUP_128):
    assert past % cb == 0 and ts <= cb and nh % hg == 0
    new_spec = pl.BlockSpec((ts, hg * D_HEAD), lambda b, h: (b, h))
    hbm = pl.BlockSpec(memory_space=pl.ANY)
    return pl.pallas_call(
        functools.partial(_sb_sample_kernel, ts=ts, past=past, pad=cb, hg=hg),
        grid=(nb, nh // hg),
        in_specs=[new_spec, new_spec, new_spec, hbm, hbm, pl.BlockSpec((2 * cb, cb), lambda b, h: (0, 0))],
        out_specs=new_spec,
        out_shape=jax.ShapeDtypeStruct(q.shape, BF16),
        scratch_shapes=[pltpu.VMEM((2, hg, past, D_HEAD), F32), pltpu.VMEM((2, hg, past, D_HEAD), F32),
                        pltpu.SemaphoreType.DMA((2, hg)), pltpu.SemaphoreType.DMA((2, hg)),
                        pltpu.VMEM((hg, past + cb, D_HEAD), BF16), pltpu.VMEM((hg, past + cb, D_HEAD), BF16)],
        compiler_params=_params("arbitrary", "arbitrary"),
        name="sb_attention_sample",
    )(q, kn, vn, kc, vc, _suffix_matrix(cb))


def _band_bias_kernel(tab_ref, o_ref, *, rows, cols, width):
    h = pl.program_id(0)
    band = BAND_CHUNKS * CHUNK
    x = _iota((8, width), 1)
    idx = jnp.clip(x - rows - band, -REL_CLIP, REL_CLIP) + REL_CLIP

    def body(r, g):
        return jnp.where(idx == r, tab_ref[h, r] * LOG2E, g)

    g = lax.fori_loop(0, 2 * REL_CLIP + 1, body, jnp.zeros((8, width), F32))
    gt = jnp.broadcast_to(g[0:1, :], (rows, width))
    gt = pltpu.roll(gt, 0, 1, stride=1, stride_axis=0)
    bias = gt[:, rows:rows + cols]
    dc = _iota((rows, cols), 1) // CHUNK - _iota((rows, cols), 0) // CHUNK
    o_ref[0] = jnp.where((dc >= 0) & (dc <= BAND_CHUNKS), bias, NEG)


def band_bias(table, qchunks):
    nh = table.shape[0]
    rows = qchunks * CHUNK
    cols = (qchunks + BAND_CHUNKS) * CHUNK
    width = rows + cols
    assert width % LANES == 0 and rows % LANES == 0
    return pl.pallas_call(
        functools.partial(_band_bias_kernel, rows=rows, cols=cols, width=width),
        grid=(nh,),
        in_specs=[pl.BlockSpec(memory_space=pltpu.SMEM)],
        out_specs=pl.BlockSpec((1, rows, cols), lambda h: (h, 0, 0)),
        out_shape=jax.ShapeDtypeStruct((nh, rows, cols), F32),
        compiler_params=_params("arbitrary"),
        name="band_bias",
    )(table)


def _softmax_rows_base2(s2):
    m = jnp.max(s2, axis=-1, keepdims=True)
    p = jnp.exp2(s2 - m)
    return p / jnp.sum(p, axis=-1, keepdims=True)


def _band_prompt_kernel(q_ref, k_ref, v_ref, bias_ref, gq_ref, gk_ref, o_ref, kt_ref, vt_ref, kb_ref, vb_ref,
                        *, seq, rows, cols, keep):
    band = BAND_CHUNKS * CHUNK
    kn = _rms(k_ref[...], gk_ref[...])
    kt_ref[...] = kn[seq - keep:, :]
    vt_ref[...] = v_ref[pl.ds(seq - keep, keep), :]
    kb_ref[pl.ds(0, band), :] = jnp.zeros((band, D_HEAD), BF16)
    vb_ref[pl.ds(0, band), :] = jnp.zeros((band, D_HEAD), BF16)
    kb_ref[pl.ds(band, seq), :] = kn.astype(BF16)
    vb_ref[pl.ds(band, seq), :] = v_ref[...].astype(BF16)
    bias = bias_ref[0]
    gq = gq_ref[...]
    col = _iota((rows, cols), 1)

    def body(t, carry):
        r0 = pl.multiple_of(t * rows, rows)
        q = (_rms(q_ref[pl.ds(r0, rows), :], gq) * (SCALE * LOG2E)).astype(BF16)
        k = kb_ref[pl.ds(r0, cols), :]
        v = vb_ref[pl.ds(r0, cols), :]
        s = lax.dot_general(q, k, (((1,), (1,)), ((), ())), preferred_element_type=F32) + bias
        s = jnp.where(col >= band - r0, s, NEG)
        p = _softmax_rows_base2(s)
        o_ref[pl.ds(r0, rows), :] = jnp.dot(p.astype(BF16), v, preferred_element_type=F32).astype(o_ref.dtype)
        return carry

    lax.fori_loop(0, seq // rows, body, 0, unroll=BAND_UNROLL)


def band_attention_prompt(q, k, v, bias, g_q, g_k, nb, seq, keep):
    nh = q.shape[1] // D_HEAD
    rows, cols = bias.shape[1], bias.shape[2]
    band = BAND_CHUNKS * CHUNK
    blk = pl.BlockSpec((seq, D_HEAD), lambda b, h: (b, h))
    tail = pl.BlockSpec((keep, D_HEAD), lambda b, h: (b, h))
    gspec = pl.BlockSpec((1, D_HEAD), lambda b, h: (0, 0))
    return pl.pallas_call(
        functools.partial(_band_prompt_kernel, seq=seq, rows=rows, cols=cols, keep=keep),
        grid=(nb, nh),
        in_specs=[blk, blk, blk, pl.BlockSpec((1, rows, cols), lambda b, h: (h, 0, 0)), gspec, gspec],
        out_specs=[blk, tail, tail],
        out_shape=[jax.ShapeDtypeStruct(q.shape, BF16),
                   jax.ShapeDtypeStruct((nb * keep, q.shape[1]), F32),
                   jax.ShapeDtypeStruct((nb * keep, q.shape[1]), F32)],
        scratch_shapes=[pltpu.VMEM((band + seq, D_HEAD), BF16), pltpu.VMEM((band + seq, D_HEAD), BF16)],
        compiler_params=_params("arbitrary", "arbitrary"),
        name="band_attention_prompt",
    )(q, k, v, bias, g_q.reshape(1, D_HEAD), g_k.reshape(1, D_HEAD))


def _band_sample_kernel(q_ref, kn_ref, vn_ref, kc_hbm, vc_hbm, bias_ref, gq_ref, gk_ref, o_ref, ko_ref,
                        kbuf, vbuf, ksem, vsem, *, ts, w, cols, hg):
    slot = _prefetch_heads((kc_hbm, vc_hbm), (kbuf, vbuf), (ksem, vsem), hg)
    d = D_HEAD
    gq, gk = gq_ref[...], gk_ref[...]
    zeros = jnp.zeros((cols - w - ts, d), BF16)
    valid = _iota((ts, cols), 1) < w + ts
    for g in range(hg):
        hs = slice(g * d, (g + 1) * d)
        kn = _rms(kn_ref[:, hs], gk)
        ko_ref[:, hs] = kn
        k = jnp.concatenate([kbuf[slot, g].astype(BF16), kn.astype(BF16), zeros], axis=0)
        v = jnp.concatenate([vbuf[slot, g].astype(BF16), vn_ref[:, hs].astype(BF16), zeros], axis=0)
        q = (_rms(q_ref[:, hs], gq) * (SCALE * LOG2E)).astype(BF16)
        s = lax.dot_general(q, k, (((1,), (1,)), ((), ())), preferred_element_type=F32) + bias_ref[g]
        p = _softmax_rows_base2(jnp.where(valid, s, NEG))
        o_ref[:, hs] = jnp.dot(p.astype(BF16), v, preferred_element_type=F32).astype(o_ref.dtype)


def band_attention_sample(q, kn, vn, kc, vc, bias, g_q, g_k, nb, ts, w, nh, hg=SAMPLE_HEAD_GROUP_128):
    cols = -(-(w + ts) // LANES) * LANES
    assert w == BAND_CHUNKS * CHUNK and ts <= CHUNK and cols <= bias.shape[2] and ts % 16 == 0 and nh % hg == 0
    new_spec = pl.BlockSpec((ts, hg * D_HEAD), lambda b, h: (b, h))
    hbm = pl.BlockSpec(memory_space=pl.ANY)
    gspec = pl.BlockSpec((1, D_HEAD), lambda b, h: (0, 0))
    return pl.pallas_call(
        functools.partial(_band_sample_kernel, ts=ts, w=w, cols=cols, hg=hg),
        grid=(nb, nh // hg),
        in_specs=[new_spec, new_spec, new_spec, hbm, hbm,
                  pl.BlockSpec((hg, ts, cols), lambda b, h: (h, 0, 0)), gspec, gspec],
        out_specs=[new_spec, new_spec],
        out_shape=[jax.ShapeDtypeStruct(q.shape, BF16), jax.ShapeDtypeStruct(q.shape, F32)],
        scratch_shapes=[pltpu.VMEM((2, hg, w, D_HEAD), F32), pltpu.VMEM((2, hg, w, D_HEAD), F32),
                        pltpu.SemaphoreType.DMA((2, hg)), pltpu.SemaphoreType.DMA((2, hg))],
        compiler_params=_params("arbitrary", "arbitrary"),
        name="band_attention_sample",
    )(q, kn, vn, kc, vc, bias, g_q.reshape(1, D_HEAD), g_k.reshape(1, D_HEAD))


def _norm_rope(x, g, cosf, sinf, rot):
    y = _rms(x, g)
    hi = y.astype(BF16)
    lo = (y - hi.astype(F32)).astype(BF16)
    swapped = jnp.dot(jnp.concatenate([hi, lo], axis=1), rot, preferred_element_type=F32)
    return y * cosf + swapped * sinf


def rope_tables(pos):
    half = ROT_DIM // 2
    inv_freq = ROPE_THETA ** (-jnp.arange(half, dtype=F32) * (2.0 / ROT_DIM))
    ang = pos.astype(F32)[:, None] * inv_freq[None, :]
    cos, sin = jnp.cos(ang), jnp.sin(ang)
    n = pos.shape[0]
    rest = D_HEAD - ROT_DIM
    cosf = jnp.concatenate([cos, cos, jnp.ones((n, rest), F32)], axis=1)
    sinf = jnp.concatenate([sin, sin, jnp.zeros((n, rest), F32)], axis=1)
    src = jnp.arange(D_HEAD)[:, None]
    dst = jnp.arange(D_HEAD)[None, :]
    rot = (jnp.where((dst < half) & (src == dst + half), -1.0, 0.0)
           + jnp.where((dst >= half) & (dst < ROT_DIM) & (src == dst - half), 1.0, 0.0)).astype(BF16)
    return cosf, sinf, jnp.concatenate([rot, rot], axis=0)


def _online_block(q, k, v, mask, m, l, acc):
    s = lax.dot_general(q, k, (((1,), (1,)), ((), ())), preferred_element_type=F32)
    if mask is not None:
        s = jnp.where(mask, s, NEG)
    m_new = jnp.maximum(m, jnp.max(s, axis=-1, keepdims=True))
    alpha = jnp.exp2(m - m_new)
    p = jnp.exp2(s - m_new)
    l = alpha * l + jnp.sum(p, axis=-1, keepdims=True)
    acc = alpha * acc + jnp.dot(p.astype(BF16), v, preferred_element_type=F32)
    return m_new, l, acc


def _diff_lambda(lq1, lk1, lq2, lk2):
    return (jnp.exp(jnp.sum(lq1 * lk1, axis=-1, keepdims=True))
            - jnp.exp(jnp.sum(lq2 * lk2, axis=-1, keepdims=True)) + LAMBDA_INIT)


def _diff_finish(state, lam, gsub):
    (_, l1, a1), (_, l2, a2) = state
    o = a1 / l1 - lam * (a2 / l2)
    return _rms(o, gsub) * (1.0 - LAMBDA_INIT)


def _diff_init(tq):
    return (jnp.full((tq, 1), NEG, F32), jnp.zeros((tq, 1), F32), jnp.zeros((tq, 2 * D_HEAD), F32))


def _diff_prompt_kernel(q_ref, k_ref, v_ref, cos_ref, sin_ref, rot_ref, gq_ref, gk_ref, lam_ref, gsub_ref,
                        o_ref, ko_hbm, vo_hbm, kf_ref, k1_ref, k2_ref, vb_ref, osem, *, t, tk, tr):
    qi = pl.program_id(2)
    d = D_HEAD

    @pl.when(qi == 0)
    def _():
        gk = gk_ref[...]
        cosf, sinf, rot = cos_ref[...], sin_ref[...], rot_ref[...]
        k1 = _norm_rope(k_ref[:, :d], gk, cosf, sinf, rot)
        k2 = _norm_rope(k_ref[:, d:], gk, cosf, sinf, rot)
        kf_ref[:, :d] = k1
        kf_ref[:, d:] = k2
        k1_ref[...] = k1.astype(BF16)
        k2_ref[...] = k2.astype(BF16)
        vb_ref[...] = v_ref[...].astype(BF16)

    _publish_heads((kf_ref, v_ref), (ko_hbm, vo_hbm), osem)

    r0 = pl.multiple_of(qi * t, t)
    gq = gq_ref[...]
    cosf, sinf, rot = cos_ref[pl.ds(r0, t), :], sin_ref[pl.ds(r0, t), :], rot_ref[...]
    q1 = (_norm_rope(q_ref[:, :d], gq, cosf, sinf, rot) * (SCALE * LOG2E)).astype(BF16)
    q2 = (_norm_rope(q_ref[:, d:], gq, cosf, sinf, rot) * (SCALE * LOG2E)).astype(BF16)

    subs = [slice(r * tr, (r + 1) * tr) for r in range(t // tr)]

    def step(k0, state, mask):
        v = vb_ref[pl.ds(k0, tk), :]
        k1 = k1_ref[pl.ds(k0, tk), :]
        k2 = k2_ref[pl.ds(k0, tk), :]
        out = []
        for rows, (st1, st2) in zip(subs, state):
            mr = None if mask is None else mask[rows]
            out.append((_online_block(q1[rows], k1, v, mr, *st1), _online_block(q2[rows], k2, v, mr, *st2)))
        return tuple(out)

    def body(kb, state):
        return step(pl.multiple_of(kb * tk, tk), state, None)

    state = lax.fori_loop(0, qi * (t // tk), body, tuple((_diff_init(tr), _diff_init(tr)) for _ in subs))
    for c in range(t // tk):
        mask = ((c * tk + _iota((t, tk), 1)) // CHUNK) <= (_iota((t, tk), 0) // CHUNK)
        state = step(pl.multiple_of(r0 + c * tk, tk), state, mask)
    lam = _diff_lambda(lam_ref[0:1, :], lam_ref[1:2, :], lam_ref[2:3, :], lam_ref[3:4, :])
    gsub = gsub_ref[...]
    for rows, st in zip(subs, state):
        o_ref[rows, :] = _diff_finish(st, lam, gsub).astype(o_ref.dtype)


def diff_attention_prompt(q, k, v, tabs, g_q, g_k, lam4, g_sub, nb, seq, t=DIFF_T, tk=DIFF_TK, tr=DIFF_TR):
    d2 = 2 * D_HEAD
    nh = q.shape[1] // d2
    assert t % tk == 0 and tk % CHUNK == 0 and t % tr == 0
    kv_spec = pl.BlockSpec((seq, d2), lambda b, h, qi: (b, h))
    q_spec = pl.BlockSpec((t, d2), lambda b, h, qi: (b * (seq // t) + qi, h))
    tab_spec = pl.BlockSpec((seq, D_HEAD), lambda b, h, qi: (0, 0))
    g_spec = pl.BlockSpec((1, D_HEAD), lambda b, h, qi: (0, 0))
    hbm = pl.BlockSpec(memory_space=pl.ANY)
    cache = jax.ShapeDtypeStruct((nb, seq, nh, d2), F32)
    return pl.pallas_call(
        functools.partial(_diff_prompt_kernel, t=t, tk=tk, tr=tr),
        grid=(nb, nh, seq // t),
        in_specs=[q_spec, kv_spec, kv_spec, tab_spec, tab_spec,
                  pl.BlockSpec((d2, D_HEAD), lambda b, h, qi: (0, 0)), g_spec, g_spec,
                  pl.BlockSpec((4, D_HEAD), lambda b, h, qi: (0, 0)),
                  pl.BlockSpec((1, d2), lambda b, h, qi: (0, 0))],
        out_specs=[q_spec, hbm, hbm],
        out_shape=[jax.ShapeDtypeStruct(q.shape, BF16), cache, cache],
        scratch_shapes=[pltpu.VMEM((seq, d2), F32), pltpu.VMEM((seq, D_HEAD), BF16), pltpu.VMEM((seq, D_HEAD), BF16),
                        pltpu.VMEM((seq, d2), BF16), pltpu.SemaphoreType.DMA((2,))],
        compiler_params=_params("arbitrary", "arbitrary", "arbitrary"),
        name="diff_attention_prompt",
    )(q, k, v, *tabs, g_q.reshape(1, D_HEAD), g_k.reshape(1, D_HEAD), lam4, g_sub.reshape(1, d2))


def _diff_sample_kernel(q_ref, kn_ref, vn_ref, kc_hbm, vc_hbm, cos_ref, sin_ref, rot_ref, gq_ref, gk_ref,
                        lam_ref, gsub_ref, o_ref, ko_ref, kbuf, vbuf, ksem, vsem, k1_ref, k2_ref, vb_ref,
                        *, ts, past, tn, hg):
    slot = _prefetch_heads((kc_hbm, vc_hbm), (kbuf, vbuf), (ksem, vsem), hg)
    d = D_HEAD
    cosf, sinf, rot = cos_ref[...], sin_ref[...], rot_ref[...]
    gq, gk = gq_ref[...], gk_ref[...]
    lam = _diff_lambda(lam_ref[0:1, :], lam_ref[1:2, :], lam_ref[2:3, :], lam_ref[3:4, :])
    gsub = gsub_ref[...]
    nk = past + tn
    kpos = _iota((ts, nk), 1)
    qpos = past + _iota((ts, nk), 0)
    mask = ((kpos // CHUNK) <= (qpos // CHUNK)) & (kpos < past + ts)
    for g in range(hg):
        c1, c2 = slice(2 * g * d, (2 * g + 1) * d), slice((2 * g + 1) * d, (2 * g + 2) * d)
        cv = slice(2 * g * d, (2 * g + 2) * d)
        k1 = _norm_rope(kn_ref[:, c1], gk, cosf, sinf, rot)
        k2 = _norm_rope(kn_ref[:, c2], gk, cosf, sinf, rot)
        ko_ref[:, c1] = k1
        ko_ref[:, c2] = k2
        k1_ref[g, pl.ds(0, past), :] = kbuf[slot, g, :, :d].astype(BF16)
        k2_ref[g, pl.ds(0, past), :] = kbuf[slot, g, :, d:].astype(BF16)
        vb_ref[g, pl.ds(0, past), :] = vbuf[slot, g].astype(BF16)
        k1_ref[g, pl.ds(past, tn), :] = jnp.zeros((tn, d), BF16)
        k2_ref[g, pl.ds(past, tn), :] = jnp.zeros((tn, d), BF16)
        vb_ref[g, pl.ds(past, tn), :] = jnp.zeros((tn, 2 * d), BF16)
        k1_ref[g, pl.ds(past, ts), :] = k1.astype(BF16)
        k2_ref[g, pl.ds(past, ts), :] = k2.astype(BF16)
        vb_ref[g, pl.ds(past, ts), :] = vn_ref[:, cv].astype(BF16)
        q1 = (_norm_rope(q_ref[:, c1], gq, cosf, sinf, rot) * (SCALE * LOG2E)).astype(BF16)
        q2 = (_norm_rope(q_ref[:, c2], gq, cosf, sinf, rot) * (SCALE * LOG2E)).astype(BF16)
        v = vb_ref[g]
        state = (_online_block(q1, k1_ref[g], v, mask, *_diff_init(ts)),
                 _online_block(q2, k2_ref[g], v, mask, *_diff_init(ts)))
        o_ref[:, cv] = _diff_finish(state, lam, gsub).astype(o_ref.dtype)


def diff_attention_sample(q, kn, vn, kc, vc, tabs, g_q, g_k, lam4, g_sub, nb, ts, past, nh, tn=LANES,
                          hg=SAMPLE_HEAD_GROUP):
    d2 = 2 * D_HEAD
    assert past % LANES == 0 and ts <= tn and nh % hg == 0
    new_spec = pl.BlockSpec((ts, hg * d2), lambda b, h: (b, h))
    hbm = pl.BlockSpec(memory_space=pl.ANY)
    tab_spec = pl.BlockSpec((ts, D_HEAD), lambda b, h: (0, 0))
    g_spec = pl.BlockSpec((1, D_HEAD), lambda b, h: (0, 0))
    return pl.pallas_call(
        functools.partial(_diff_sample_kernel, ts=ts, past=past, tn=tn, hg=hg),
        grid=(nb, nh // hg),
        in_specs=[new_spec, new_spec, new_spec, hbm, hbm, tab_spec, tab_spec,
                  pl.BlockSpec((d2, D_HEAD), lambda b, h: (0, 0)),
                  g_spec, g_spec, pl.BlockSpec((4, D_HEAD), lambda b, h: (0, 0)),
                  pl.BlockSpec((1, d2), lambda b, h: (0, 0))],
        out_specs=[new_spec, new_spec],
        out_shape=[jax.ShapeDtypeStruct(q.shape, BF16), jax.ShapeDtypeStruct(q.shape, F32)],
        scratch_shapes=[pltpu.VMEM((2, hg, past, d2), F32), pltpu.VMEM((2, hg, past, d2), F32),
                        pltpu.SemaphoreType.DMA((2, hg)), pltpu.SemaphoreType.DMA((2, hg)),
                        pltpu.VMEM((hg, past + tn, D_HEAD), BF16), pltpu.VMEM((hg, past + tn, D_HEAD), BF16),
                        pltpu.VMEM((hg, past + tn, d2), BF16)],
        compiler_params=_params("arbitrary", "arbitrary"),
        name="diff_attention_sample",
    )(q, kn, vn, kc, vc, *tabs, g_q.reshape(1, D_HEAD), g_k.reshape(1, D_HEAD), lam4, g_sub.reshape(1, d2))


def _to_time_major(x, nb, ts):
    return x.reshape(nb, ts, -1).transpose(1, 0, 2).reshape(nb * ts, -1)


def _to_batch_major(x, nb, ts):
    return x.reshape(ts, nb, -1).transpose(1, 0, 2).reshape(nb * ts, -1)


def _conv_glu_layer(hp, hs, g_norm, w_up, w_conv, w_down, layer, state, nb_s, ts, seq):
    mp = hp.shape[0]
    hs_t = _to_time_major(hs, nb_s, ts)
    xn = rmsnorm_rows(hp, hs_t, g_norm)
    hidden, conv_p, conv_s_t = up_convglu(xn, w_up, w_conv, layer, state.transpose(1, 0, 2), mp, seq)
    hp, hs_t = down_project(hidden, w_down, layer, hp, hs_t)
    return hp, _to_batch_major(hs_t, nb_s, ts), conv_p, conv_s_t.transpose(1, 0, 2)


def kernel(x_prompt, x_sample, cache_k_sb, cache_v_sb, cache_k_band, cache_v_band, cache_k_diff, cache_v_diff, state_conv_ffn, norm_mix, norm_ffn, w_in_ab, w_out_ab, g_q_band, g_k_band, rel_bias_band, w_in_diff, w_out_diff, g_q_diff, g_k_diff, lambda_q1, lambda_k1, lambda_q2, lambda_k2, g_sub_diff, w_up, w_conv, w_down):
    bp, tp, dm = x_prompt.shape
    bs, ts, _ = x_sample.shape
    past = cache_k_sb.shape[1]
    h_sb, h_cb, h_df = cache_k_sb.shape[2], cache_k_band.shape[2], cache_k_diff.shape[2]
    d_sb, d_cb, d_df = h_sb * D_HEAD, h_cb * D_HEAD, h_df * 2 * D_HEAD
    w_band = cache_k_band.shape[1]
    keep = min(BAND_CHUNKS * CHUNK, tp)
    mp, ms = bp * tp, bs * ts

    hp = x_prompt.reshape(mp, dm)
    hs = x_sample.reshape(ms, dm)

    xn = rmsnorm_rows(hp, hs, norm_mix[0])
    sec = []
    col = 0
    for width in (d_sb, d_sb, d_sb, d_cb, d_cb, d_cb):
        sec.append(project(xn, w_in_ab, col, width, mp))
        col += width
    (qa_p, qa_s), (ka_p, ka_s), (va_p, va_s), (qb_p, qb_s), (kb_p, kb_s), (vb_p, vb_s) = sec

    oa_p, k_sb_p, v_sb_p = sb_attention_prompt(qa_p, ka_p, va_p, bp, tp)
    oa_s = sb_attention_sample(qa_s, ka_s, va_s, cache_k_sb, cache_v_sb, bs, ts, past, h_sb)
    bias = band_bias(rel_bias_band, BAND_QCHUNKS)
    ob_p, k_band_p, v_band_p = band_attention_prompt(qb_p, kb_p, vb_p, bias, g_q_band, g_k_band, bp, tp, keep)
    ob_s, k_band_s = band_attention_sample(qb_s, kb_s, vb_s, cache_k_band, cache_v_band, bias,
                                           g_q_band, g_k_band, bs, ts, w_band, h_cb)
    hp, hs = out_project([oa_p, ob_p], [oa_s, ob_s], w_out_ab, hp, hs)
    hp, hs, conv_p0, conv_s0 = _conv_glu_layer(hp, hs, norm_ffn[0], w_up, w_conv, w_down, 0,
                                               state_conv_ffn[0], bs, ts, tp)

    xn = rmsnorm_rows(hp, hs, norm_mix[1])
    (q_p, q_s), (k_p, k_s), (v_p, v_s) = [project(xn, w_in_diff, c * d_df, d_df, mp) for c in range(3)]
    lam4 = jnp.stack([lambda_q1, lambda_k1, lambda_q2, lambda_k2]).astype(F32)
    o_p, k_diff_p, v_diff_p = diff_attention_prompt(q_p, k_p, v_p, rope_tables(jnp.arange(tp)), g_q_diff,
                                                    g_k_diff, lam4, g_sub_diff, bp, tp)
    o_s, k_diff_s = diff_attention_sample(q_s, k_s, v_s, cache_k_diff, cache_v_diff,
                                          rope_tables(past + jnp.arange(ts)), g_q_diff, g_k_diff,
                                          lam4, g_sub_diff, bs, ts, past, h_df)
    hp, hs = out_project([o_p], [o_s], w_out_diff, hp, hs)
    hp, hs, conv_p1, conv_s1 = _conv_glu_layer(hp, hs, norm_ffn[1], w_up, w_conv, w_down, 1,
                                               state_conv_ffn[1], bs, ts, tp)

    return (hp.reshape(bp, tp, dm), hs.reshape(bs, ts, dm),
            k_sb_p, v_sb_p,
            k_band_p.reshape(bp, keep, h_cb, D_HEAD), v_band_p.reshape(bp, keep, h_cb, D_HEAD),
            k_diff_p, v_diff_p,
            jnp.stack([conv_p0, conv_p1]),
            ka_s.reshape(bs, ts, h_sb, D_HEAD), va_s.reshape(bs, ts, h_sb, D_HEAD),
            k_band_s.reshape(bs, ts, h_cb, D_HEAD), vb_s.reshape(bs, ts, h_cb, D_HEAD),
            k_diff_s.reshape(bs, ts, h_df, 2 * D_HEAD), v_s.reshape(bs, ts, h_df, 2 * D_HEAD),
            jnp.stack([conv_s0, conv_s1]))
```

```python
import functools
import math

import jax
import jax.numpy as jnp
from jax import lax
from jax.experimental import pallas as pl
from jax.experimental.pallas import tpu as pltpu

F32 = jnp.float32
BF16 = jnp.bfloat16

CHUNK = 64
D_HEAD = 128
BAND_CHUNKS = 8
REL_CLIP = 128
ROT_DIM = D_HEAD // 4
ROPE_THETA = 500000.0
CONV_W = 3
EPS = 1e-6
NEG = -1e30
LAMBDA_INIT = 0.8 - 0.6 * math.exp(-0.3 * 1)
SCALE = 1.0 / math.sqrt(D_HEAD)
LOG2E = math.log2(math.e)

V7X_VMEM_LIMIT_BYTES = 56 * 1024 * 1024
LANES = 128

ROW_TILE = 1024
PROJ_ROW_TILE = 512
PROJ_COL_TILE = 1024
OUT_ROW_TILE = 1024
UP_COL_BLOCK = 256
UP_ROW_CHUNK = 512
DOWN_ROW_TILE = 256
SB_T = 512
SB_CB = 256
SB_HEAD_GROUP = 2
BAND_QCHUNKS = 4
BAND_UNROLL = 8
SAMPLE_HEAD_GROUP = 4
SAMPLE_HEAD_GROUP_128 = 8
DIFF_T = 512
DIFF_TK = 512
DIFF_TR = 128


def _params(*sem):
    return pltpu.CompilerParams(dimension_semantics=sem, vmem_limit_bytes=V7X_VMEM_LIMIT_BYTES)


def _iota(shape, dim):
    return lax.broadcasted_iota(jnp.int32, shape, dim)


def _rms(x, g):
    return x * lax.rsqrt(jnp.mean(x * x, axis=-1, keepdims=True) + EPS) * g


def _rmsnorm_kernel(xp_ref, xs_ref, g_ref, o_ref, *, np_tiles):
    i = pl.program_id(0)

    @pl.when(i < np_tiles)
    def _():
        o_ref[...] = _rms(xp_ref[...], g_ref[...]).astype(o_ref.dtype)

    @pl.when(i >= np_tiles)
    def _():
        o_ref[...] = _rms(xs_ref[...], g_ref[...]).astype(o_ref.dtype)


def rmsnorm_rows(xp, xs, g, tm=256):
    mp, d = xp.shape
    ms = xs.shape[0]
    np_tiles, ns_tiles = mp // tm, ms // tm
    return pl.pallas_call(
        functools.partial(_rmsnorm_kernel, np_tiles=np_tiles),
        grid=(np_tiles + ns_tiles,),
        in_specs=[
            pl.BlockSpec((tm, d), lambda i: (jnp.minimum(i, np_tiles - 1), 0)),
            pl.BlockSpec((tm, d), lambda i: (jnp.maximum(i - np_tiles, 0), 0)),
            pl.BlockSpec((1, d), lambda i: (0, 0)),
        ],
        out_specs=pl.BlockSpec((tm, d), lambda i: (i, 0)),
        out_shape=jax.ShapeDtypeStruct((mp + ms, d), BF16),
        compiler_params=_params("arbitrary"),
        name="rmsnorm_rows",
    )(xp, xs, g.reshape(1, d))


def _refresh_weights(copies, wf_ref, wb_ref):
    jj, i = pl.program_id(0), pl.program_id(1)
    tb = wf_ref.shape[2]

    @pl.when(i == 0)
    def _():
        @pl.when(jj == 0)
        def _():
            for cp in copies(jj):
                cp.start()

        for cp in copies(jj):
            cp.wait()
        for n in range(wf_ref.shape[0]):
            wb_ref[:, n * tb:(n + 1) * tb] = wf_ref[n].astype(BF16)

        @pl.when(jj + 1 < pl.num_programs(0))
        def _():
            for cp in copies(jj + 1):
                cp.start()


def _proj_kernel(x_ref, w_hbm, op_ref, os_ref, wf_ref, wsem, wb_ref, *, col0, np_tiles, ms):
    i = pl.program_id(1)
    tn = wf_ref.shape[2]

    def copies(jj):
        c0 = pl.multiple_of(col0 + jj * tn, tn)
        return [pltpu.make_async_copy(w_hbm.at[:, pl.ds(c0, tn)], wf_ref.at[0], wsem.at[0])]

    _refresh_weights(copies, wf_ref, wb_ref)

    @pl.when(i < np_tiles)
    def _():
        op_ref[...] = jnp.dot(x_ref[...], wb_ref[...], preferred_element_type=F32)

    @pl.when(i >= np_tiles)
    def _():
        os_ref[...] = jnp.dot(x_ref[0:ms, :], wb_ref[...], preferred_element_type=F32)


def project(x, w, col0, n, mp, tm=PROJ_ROW_TILE, tn=PROJ_COL_TILE):
    m, k = x.shape
    ms = m - mp
    assert mp % tm == 0 and 0 < ms <= tm and n % tn == 0 and col0 % tn == 0
    np_tiles = mp // tm
    return pl.pallas_call(
        functools.partial(_proj_kernel, col0=col0, np_tiles=np_tiles, ms=ms),
        grid=(n // tn, np_tiles + 1),
        in_specs=[
            pl.BlockSpec((tm, k), lambda j, i: (i, 0)),
            pl.BlockSpec(memory_space=pl.ANY),
        ],
        out_specs=[
            pl.BlockSpec((tm, tn), lambda j, i: (jnp.minimum(i, np_tiles - 1), j)),
            pl.BlockSpec((ms, tn), lambda j, i: (0, j)),
        ],
        out_shape=[jax.ShapeDtypeStruct((mp, n), F32), jax.ShapeDtypeStruct((ms, n), F32)],
        scratch_shapes=[pltpu.VMEM((1, k, tn), F32), pltpu.SemaphoreType.DMA((1,)), pltpu.VMEM((k, tn), BF16)],
        compiler_params=_params("arbitrary", "arbitrary"),
        name="project",
    )(x, w)


def _outproj_kernel(*refs, np_tiles, nparts):
    ap = refs[0:nparts]
    a_s = refs[nparts:2 * nparts]
    w_hbm, rp_ref, rs_ref, op_ref, os_ref, wf_ref, wsem, wb_ref = refs[2 * nparts:]
    i = pl.program_id(1)
    kp, tn = wf_ref.shape[1], wf_ref.shape[2]

    def copies(jj):
        c0 = pl.multiple_of(jj * tn, tn)
        return [pltpu.make_async_copy(w_hbm.at[pl.ds(p * kp, kp), pl.ds(c0, tn)], wf_ref.at[p], wsem.at[p])
                for p in range(nparts)]

    _refresh_weights(copies, wf_ref, wb_ref)

    def run(a_refs, r_ref, o_ref):
        acc = r_ref[...]
        for p in range(nparts):
            acc = acc + jnp.dot(a_refs[p][...], wb_ref[:, p * tn:(p + 1) * tn], preferred_element_type=F32)
        o_ref[...] = acc

    @pl.when(i < np_tiles)
    def _():
        run(ap, rp_ref, op_ref)

    @pl.when(i >= np_tiles)
    def _():
        run(a_s, rs_ref, os_ref)


def out_project(parts_p, parts_s, w, res_p, res_s, tm=OUT_ROW_TILE, tn=512):
    nparts = len(parts_p)
    mp, kp = parts_p[0].shape
    ms = parts_s[0].shape[0]
    n = w.shape[1]
    assert mp % tm == 0 and n % tn == 0 and w.shape[0] == nparts * kp
    np_tiles = mp // tm
    pmap = lambda j, i: (jnp.minimum(i, np_tiles - 1), 0)
    pmap_o = lambda j, i: (jnp.minimum(i, np_tiles - 1), j)
    in_specs = ([pl.BlockSpec((tm, kp), pmap)] * nparts + [pl.BlockSpec((ms, kp), lambda j, i: (0, 0))] * nparts
                + [pl.BlockSpec(memory_space=pl.ANY),
                   pl.BlockSpec((tm, tn), pmap_o), pl.BlockSpec((ms, tn), lambda j, i: (0, j))])
    return pl.pallas_call(
        functools.partial(_outproj_kernel, np_tiles=np_tiles, nparts=nparts),
        grid=(n // tn, np_tiles + 1),
        in_specs=in_specs,
        out_specs=[pl.BlockSpec((tm, tn), pmap_o), pl.BlockSpec((ms, tn), lambda j, i: (0, j))],
        out_shape=[jax.ShapeDtypeStruct((mp, n), F32), jax.ShapeDtypeStruct((ms, n), F32)],
        scratch_shapes=[pltpu.VMEM((nparts, kp, tn), F32), pltpu.SemaphoreType.DMA((nparts,)),
                        pltpu.VMEM((kp, nparts * tn), BF16)],
        compiler_params=_params("arbitrary", "arbitrary"),
        name="out_project",
    )(*parts_p, *parts_s, w, res_p, res_s)


def _silu_mul(gc, u):
    return gc * (1.0 / (1.0 + jnp.exp(-gc))) * u


def _up_weight_copies(w_hbm, wf_ref, wsem, jj, *, layer, nj, tb):
    j0 = 2 * jj
    j1 = jnp.minimum(j0 + 1, nj - 1)
    cols = (j0, j1, nj + j0, nj + j1)
    return [pltpu.make_async_copy(w_hbm.at[layer, :, pl.ds(pl.multiple_of(c * tb, tb), tb)], wf_ref.at[n], wsem.at[n])
            for n, c in enumerate(cols)]


def _up_kernel(x_ref, w_hbm, wc_ref, hist_ref, h_ref, cp_ref, cs_ref, wf_ref, wsem, wb_ref, carry_ref,
               *, layer, nj, tb, np_tiles, tiles_per_seq, nb_s, ms, rc):
    i = pl.program_id(1)
    tm = x_ref.shape[0]
    tn = 2 * tb
    _refresh_weights(functools.partial(_up_weight_copies, w_hbm, wf_ref, wsem, layer=layer, nj=nj, tb=tb),
                     wf_ref, wb_ref)

    w0 = wc_ref[0:1, :]
    w1 = wc_ref[1:2, :]
    w2 = wc_ref[2:3, :]

    @pl.when(lax.rem(i, tiles_per_seq) == 0)
    def _():
        carry_ref[...] = jnp.zeros_like(carry_ref)

    @pl.when(i < np_tiles)
    def _():
        c0 = carry_ref[0:1, :]
        c1 = carry_ref[1:2, :]
        row = _iota((rc, tn), 0)
        for r in range(tm // rc):
            rows = slice(r * rc, (r + 1) * rc)
            gu = jnp.dot(x_ref[rows, :], wb_ref[...], preferred_element_type=F32)
            g = gu[:, :tn]
            u = gu[:, tn:]
            gm1 = jnp.where(row == 0, c1, pltpu.roll(g, 1, 0))
            gm2 = jnp.where(row == 0, c0, jnp.where(row == 1, c1, pltpu.roll(g, 2, 0)))
            gc = gm2 * w0 + gm1 * w1 + g * w2
            h_ref[rows, :] = _silu_mul(gc, u).astype(h_ref.dtype)
            c0 = g[rc - 2:rc - 1, :]
            c1 = g[rc - 1:rc, :]
        tail = jnp.concatenate([c0, c1], axis=0)
        carry_ref[0:2, :] = tail
        cp_ref[0] = tail

    @pl.when(i >= np_tiles)
    def _():
        gu = jnp.dot(x_ref[0:ms, :], wb_ref[...], preferred_element_type=F32)
        g = gu[:, :tn]
        u = gu[:, tn:]
        h0 = hist_ref[0]
        h1 = hist_ref[1]
        gm1 = jnp.concatenate([h1, g[:ms - nb_s]], axis=0)
        gm2 = jnp.concatenate([h0, h1, g[:ms - 2 * nb_s]], axis=0)
        gc = gm2 * w0 + gm1 * w1 + g * w2
        h_ref[0:ms, :] = _silu_mul(gc, u).astype(h_ref.dtype)
        cs_ref[0] = g[ms - 2 * nb_s:ms - nb_s]
        cs_ref[1] = g[ms - nb_s:]


def up_convglu(x, w_up, w_conv, layer, hist_t, mp, seq_len, tm=ROW_TILE, tb=UP_COL_BLOCK, rc=UP_ROW_CHUNK):
    m, k = x.shape
    f = w_up.shape[2] // 2
    nb_s = hist_t.shape[1]
    nb_p = mp // seq_len
    np_tiles = mp // tm
    ms = m - mp
    assert 0 < ms <= tm and seq_len % tm == 0 and tm % rc == 0 and f % tb == 0 and CONV_W == 3
    tiles_per_seq = seq_len // tm
    nj = f // tb
    tn = 2 * tb
    njj = -(-nj // 2)
    return pl.pallas_call(
        functools.partial(_up_kernel, layer=layer, nj=nj, tb=tb, np_tiles=np_tiles, tiles_per_seq=tiles_per_seq,
                          nb_s=nb_s, ms=ms, rc=rc),
        grid=(njj, np_tiles + 1),
        in_specs=[
            pl.BlockSpec((tm, k), lambda j, i: (i, 0)),
            pl.BlockSpec(memory_space=pl.ANY),
            pl.BlockSpec((None, CONV_W, tn), lambda j, i: (layer, 0, j)),
            pl.BlockSpec((CONV_W - 1, nb_s, tn), lambda j, i: (0, 0, j)),
        ],
        out_specs=[
            pl.BlockSpec((tm, tn), lambda j, i: (i, j)),
            pl.BlockSpec((1, CONV_W - 1, tn), lambda j, i: (jnp.minimum(i // tiles_per_seq, nb_p - 1), 0, j)),
            pl.BlockSpec((CONV_W - 1, nb_s, tn), lambda j, i: (0, 0, j)),
        ],
        out_shape=[
            jax.ShapeDtypeStruct((m, f), BF16),
            jax.ShapeDtypeStruct((nb_p, CONV_W - 1, f), F32),
            jax.ShapeDtypeStruct((CONV_W - 1, nb_s, f), F32),
        ],
        scratch_shapes=[pltpu.VMEM((4, k, tb), F32), pltpu.SemaphoreType.DMA((4,)),
                        pltpu.VMEM((k, 2 * tn), BF16), pltpu.VMEM((8, tn), F32)],
        compiler_params=_params("arbitrary", "arbitrary"),
        name="up_convglu",
    )(x, w_up, w_conv, hist_t)


def _down_kernel(a_ref, w_hbm, rp_ref, rs_ref, op_ref, os_ref, wf_ref, wsem, wb_ref, *, layer, np_tiles):
    i = pl.program_id(1)
    tn = wf_ref.shape[2]

    def copies(jj):
        c0 = pl.multiple_of(jj * tn, tn)
        return [pltpu.make_async_copy(w_hbm.at[layer, :, pl.ds(c0, tn)], wf_ref.at[0], wsem.at[0])]

    _refresh_weights(copies, wf_ref, wb_ref)

    @pl.when(i < np_tiles)
    def _():
        op_ref[...] = rp_ref[...] + jnp.dot(a_ref[...], wb_ref[...], preferred_element_type=F32)

    @pl.when(i >= np_tiles)
    def _():
        os_ref[...] = rs_ref[...] + jnp.dot(a_ref[...], wb_ref[...], preferred_element_type=F32)


def down_project(a, w, layer, res_p, res_s, tm=DOWN_ROW_TILE, tn=512):
    m, f = a.shape
    n = w.shape[2]
    mp = res_p.shape[0]
    ms = m - mp
    np_tiles, ns_tiles = mp // tm, ms // tm
    pmap_o = lambda j, i: (jnp.minimum(i, np_tiles - 1), j)
    smap_o = lambda j, i: (jnp.maximum(i - np_tiles, 0), j)
    return pl.pallas_call(
        functools.partial(_down_kernel, layer=layer, np_tiles=np_tiles),
        grid=(n // tn, np_tiles + ns_tiles),
        in_specs=[
            pl.BlockSpec((tm, f), lambda j, i: (i, 0)),
            pl.BlockSpec(memory_space=pl.ANY),
            pl.BlockSpec((tm, tn), pmap_o),
            pl.BlockSpec((tm, tn), smap_o),
        ],
        out_specs=[pl.BlockSpec((tm, tn), pmap_o), pl.BlockSpec((tm, tn), smap_o)],
        out_shape=[jax.ShapeDtypeStruct((mp, n), F32), jax.ShapeDtypeStruct((ms, n), F32)],
        scratch_shapes=[pltpu.VMEM((1, f, tn), F32), pltpu.SemaphoreType.DMA((1,)), pltpu.VMEM((f, tn), BF16)],
        compiler_params=_params("arbitrary", "arbitrary"),
        name="down_project",
    )(a, w, res_p, res_s)


def _sb_group(q, k, v, uu, run, acc, vis):
    nk = k.shape[0]
    cb = uu.shape[1]
    z2 = lax.dot_general(q, k, (((1,), (1,)), ((), ())), preferred_element_type=F32)
    s = jnp.maximum(z2, 0.0) + jnp.log2(1.0 + jnp.exp2(-jnp.abs(z2)))
    if vis is not None:
        s = jnp.where(vis, s, 0.0)
    sb16 = s.astype(BF16)
    afters = []
    for blk in reversed(range(nk // cb)):
        sl = slice(blk * cb, (blk + 1) * cb)
        cs = jnp.dot(sb16[:, sl], uu, preferred_element_type=F32)
        afters.append(cs + run)
        run = run + jnp.sum(s[:, sl], axis=-1, keepdims=True)
    after = jnp.concatenate(afters[::-1], axis=1)
    w = jnp.exp2(z2 - s - after)
    if vis is not None:
        w = jnp.where(vis, w, 0.0)
    acc = acc + jnp.dot(w.astype(BF16), v, preferred_element_type=F32)
    return run, acc


def _publish_heads(items, sem_ref):
    b, qi = pl.program_id(0), pl.program_id(2)
    copies = [pltpu.make_async_copy(src, out.at[b, :, h, :], sem_ref.at[n]) for n, (src, out, h) in enumerate(items)]

    @pl.when(qi == 0)
    def _():
        for cp in copies:
            cp.start()

    @pl.when(qi == pl.num_programs(2) - 1)
    def _():
        for cp in copies:
            cp.wait()


def _sb_prompt_kernel(q_ref, k_ref, v_ref, uu_ref, o_ref, ko_hbm, vo_hbm, kb_ref, vb_ref, osem, *, t, hg):
    hh, qi = pl.program_id(1), pl.program_id(2)
    d = D_HEAD
    cols = [slice(g * d, (g + 1) * d) for g in range(hg)]

    @pl.when(qi == 0)
    def _():
        for g in range(hg):
            kb_ref[g] = k_ref[:, cols[g]].astype(BF16)
            vb_ref[g] = v_ref[:, cols[g]].astype(BF16)

    _publish_heads([(ref.at[:, pl.ds(g * d, d)], out, hh * hg + g)
                    for ref, out in ((k_ref, ko_hbm), (v_ref, vo_hbm)) for g in range(hg)], osem)

    qs = [(q_ref[:, c] * (SCALE * LOG2E)).astype(BF16) for c in cols]
    uu = uu_ref[...]
    r0 = pl.multiple_of(qi * t, t)
    vis = _iota((t, t), 1) < _iota((t, t), 0)

    def groups(k0, state, mask):
        return tuple(_sb_group(qs[g], kb_ref[g, pl.ds(k0, t), :], vb_ref[g, pl.ds(k0, t), :], uu, run, acc, mask)
                     for g, (run, acc) in enumerate(state))

    state = groups(r0, tuple((jnp.zeros((t, 1), F32), jnp.zeros((t, d), F32)) for _ in cols), vis)
    state = lax.fori_loop(0, qi, lambda i, st: groups(pl.multiple_of((qi - 1 - i) * t, t), st, None), state)
    for c, (_, acc) in zip(cols, state):
        o_ref[:, c] = acc.astype(o_ref.dtype)


def _suffix_matrix(cb):
    j = jnp.arange(cb)[:, None]
    s = jnp.arange(cb)[None, :]
    return (j > s).astype(BF16)


def sb_attention_prompt(q, k, v, nb, seq, t=SB_T, cb=SB_CB, hg=SB_HEAD_GROUP):
    nh = q.shape[1] // D_HEAD
    assert seq % t == 0 and t % cb == 0 and nh % hg == 0
    kv_spec = pl.BlockSpec((seq, hg * D_HEAD), lambda b, h, qi: (b, h))
    q_spec = pl.BlockSpec((t, hg * D_HEAD), lambda b, h, qi: (b * (seq // t) + qi, h))
    hbm = pl.BlockSpec(memory_space=pl.ANY)
    cache = jax.ShapeDtypeStruct((nb, seq, nh, D_HEAD), F32)
    return pl.pallas_call(
        functools.partial(_sb_prompt_kernel, t=t, hg=hg),
        grid=(nb, nh // hg, seq // t),
        in_specs=[q_spec, kv_spec, kv_spec, pl.BlockSpec((cb, cb), lambda b, h, qi: (0, 0))],
        out_specs=[q_spec, hbm, hbm],
        out_shape=[jax.ShapeDtypeStruct(q.shape, BF16), cache, cache],
        scratch_shapes=[pltpu.VMEM((hg, seq, D_HEAD), BF16), pltpu.VMEM((hg, seq, D_HEAD), BF16),
                        pltpu.SemaphoreType.DMA((2 * hg,))],
        compiler_params=_params("arbitrary", "arbitrary", "arbitrary"),
        name="sb_attention_prompt",
    )(q, k, v, _suffix_matrix(cb))


def _head_copy(cache_hbm, buf_ref, sem_ref, b, h, slot, g):
    return pltpu.make_async_copy(cache_hbm.at[b, :, h, :], buf_ref.at[slot, g], sem_ref.at[slot, g])


def _prefetch_heads(caches, bufs, sems, hg):
    b, hh = pl.program_id(0), pl.program_id(1)
    nb, ng = pl.num_programs(0), pl.num_programs(1)
    step = b * ng + hh
    slot = lax.rem(step, 2)
    wrap = hh + 1 == ng
    b1 = jnp.where(wrap, b + 1, b)
    hh1 = jnp.where(wrap, 0, hh + 1)

    def copies(bb, hgrp, sl):
        return [_head_copy(c, buf, sem, bb, hgrp * hg + g, sl, g)
                for c, buf, sem in zip(caches, bufs, sems) for g in range(hg)]

    @pl.when(step == 0)
    def _():
        for cp in copies(b, hh, slot):
            cp.start()

    @pl.when(step + 1 < nb * ng)
    def _():
        for cp in copies(b1, hh1, 1 - slot):
            cp.start()

    for cp in copies(b, hh, slot):
        cp.wait()
    return slot


def _sb_sample_kernel(q_ref, kn_ref, vn_ref, kc_hbm, vc_hbm, uu_ref, o_ref, kbuf, vbuf, ksem, vsem, kb_ref, vb_ref,
                      *, ts, past, pad, hg):
    slot = _prefetch_heads((kc_hbm, vc_hbm), (kbuf, vbuf), (ksem, vsem), hg)
    d = D_HEAD
    nk = past + pad
    vis = _iota((ts, nk), 1) < past + _iota((ts, nk), 0)
    uu = uu_ref[...]
    for g in range(hg):
        cols = slice(g * d, (g + 1) * d)
        kb_ref[g, pl.ds(0, past), :] = kbuf[slot, g].astype(BF16)
        vb_ref[g, pl.ds(0, past), :] = vbuf[slot, g].astype(BF16)
        kb_ref[g, pl.ds(past, pad), :] = jnp.zeros((pad, d), BF16)
        vb_ref[g, pl.ds(past, pad), :] = jnp.zeros((pad, d), BF16)
        kb_ref[g, pl.ds(past, ts), :] = kn_ref[:, cols].astype(BF16)
        vb_ref[g, pl.ds(past, ts), :] = vn_ref[:, cols].astype(BF16)
        _, acc = _sb_group((q_ref[:, cols] * (SCALE * LOG2E)).astype(BF16), kb_ref[g], vb_ref[g], uu,
                           jnp.zeros((ts, 1), F32), jnp.zeros((ts, d), F32), vis)
        o_ref[:, cols] = acc.astype(o_ref.dtype)


def sb_attention_sample(q, kn, vn, kc, vc, nb, ts, past, nh, cb=SB_CB, hg=SAMPLE_HEAD_GROUP_128):
    assert past % cb == 0 and ts <= cb and nh % hg == 0
    new_spec = pl.BlockSpec((ts, hg * D_HEAD), lambda b, h: (b, h))
    hbm = pl.BlockSpec(memory_space=pl.ANY)
    return pl.pallas_call(
        functools.partial(_sb_sample_kernel, ts=ts, past=past, pad=cb, hg=hg),
        grid=(nb, nh // hg),
        in_specs=[new_spec, new_spec, new_spec, hbm, hbm, pl.BlockSpec((cb, cb), lambda b, h: (0, 0))],
        out_specs=new_spec,
        out_shape=jax.ShapeDtypeStruct(q.shape, BF16),
        scratch_shapes=[pltpu.VMEM((2, hg, past, D_HEAD), F32), pltpu.VMEM((2, hg, past, D_HEAD), F32),
                        pltpu.SemaphoreType.DMA((2, hg)), pltpu.SemaphoreType.DMA((2, hg)),
                        pltpu.VMEM((hg, past + cb, D_HEAD), BF16), pltpu.VMEM((hg, past + cb, D_HEAD), BF16)],
        compiler_params=_params("arbitrary", "arbitrary"),
        name="sb_attention_sample",
    )(q, kn, vn, kc, vc, _suffix_matrix(cb))


def _band_bias_kernel(tab_ref, o_ref, *, rows, cols, width):
    h = pl.program_id(0)
    band = BAND_CHUNKS * CHUNK
    x = _iota((8, width), 1)
    idx = jnp.clip(x - rows - band, -REL_CLIP, REL_CLIP) + REL_CLIP

    def body(r, g):
        return jnp.where(idx == r, tab_ref[h, r] * LOG2E, g)

    g = lax.fori_loop(0, 2 * REL_CLIP + 1, body, jnp.zeros((8, width), F32))
    gt = jnp.broadcast_to(g[0:1, :], (rows, width))
    gt = pltpu.roll(gt, 0, 1, stride=1, stride_axis=0)
    bias = gt[:, rows:rows + cols]
    dc = _iota((rows, cols), 1) // CHUNK - _iota((rows, cols), 0) // CHUNK
    o_ref[0] = jnp.where((dc >= 0) & (dc <= BAND_CHUNKS), bias, NEG)


def band_bias(table, qchunks):
    nh = table.shape[0]
    rows = qchunks * CHUNK
    cols = (qchunks + BAND_CHUNKS) * CHUNK
    width = rows + cols
    assert width % LANES == 0 and rows % LANES == 0
    return pl.pallas_call(
        functools.partial(_band_bias_kernel, rows=rows, cols=cols, width=width),
        grid=(nh,),
        in_specs=[pl.BlockSpec(memory_space=pltpu.SMEM)],
        out_specs=pl.BlockSpec((1, rows, cols), lambda h: (h, 0, 0)),
        out_shape=jax.ShapeDtypeStruct((nh, rows, cols), F32),
        compiler_params=_params("arbitrary"),
        name="band_bias",
    )(table)


def _softmax_rows_base2(s2):
    m = jnp.max(s2, axis=-1, keepdims=True)
    p = jnp.exp2(s2 - m)
    return p / jnp.sum(p, axis=-1, keepdims=True)


def _band_prompt_kernel(q_ref, k_ref, v_ref, bias_ref, gq_ref, gk_ref, o_ref, kt_ref, vt_ref, kb_ref, vb_ref,
                        *, seq, rows, cols, keep):
    band = BAND_CHUNKS * CHUNK
    kn = _rms(k_ref[...], gk_ref[...])
    kt_ref[...] = kn[seq - keep:, :]
    vt_ref[...] = v_ref[pl.ds(seq - keep, keep), :]
    kb_ref[pl.ds(0, band), :] = jnp.zeros((band, D_HEAD), BF16)
    vb_ref[pl.ds(0, band), :] = jnp.zeros((band, D_HEAD), BF16)
    kb_ref[pl.ds(band, seq), :] = kn.astype(BF16)
    vb_ref[pl.ds(band, seq), :] = v_ref[...].astype(BF16)
    bias = bias_ref[0]
    gq = gq_ref[...]
    col = _iota((rows, cols), 1)

    def body(t, carry):
        r0 = pl.multiple_of(t * rows, rows)
        q = (_rms(q_ref[pl.ds(r0, rows), :], gq) * (SCALE * LOG2E)).astype(BF16)
        k = kb_ref[pl.ds(r0, cols), :]
        v = vb_ref[pl.ds(r0, cols), :]
        s = lax.dot_general(q, k, (((1,), (1,)), ((), ())), preferred_element_type=F32) + bias
        s = jnp.where(col >= band - r0, s, NEG)
        p = _softmax_rows_base2(s)
        o_ref[pl.ds(r0, rows), :] = jnp.dot(p.astype(BF16), v, preferred_element_type=F32).astype(o_ref.dtype)
        return carry

    lax.fori_loop(0, seq // rows, body, 0, unroll=BAND_UNROLL)


def band_attention_prompt(q, k, v, bias, g_q, g_k, nb, seq, keep):
    nh = q.shape[1] // D_HEAD
    rows, cols = bias.shape[1], bias.shape[2]
    band = BAND_CHUNKS * CHUNK
    blk = pl.BlockSpec((seq, D_HEAD), lambda b, h: (b, h))
    tail = pl.BlockSpec((keep, D_HEAD), lambda b, h: (b, h))
    gspec = pl.BlockSpec((1, D_HEAD), lambda b, h: (0, 0))
    return pl.pallas_call(
        functools.partial(_band_prompt_kernel, seq=seq, rows=rows, cols=cols, keep=keep),
        grid=(nb, nh),
        in_specs=[blk, blk, blk, pl.BlockSpec((1, rows, cols), lambda b, h: (h, 0, 0)), gspec, gspec],
        out_specs=[blk, tail, tail],
        out_shape=[jax.ShapeDtypeStruct(q.shape, BF16),
                   jax.ShapeDtypeStruct((nb * keep, q.shape[1]), F32),
                   jax.ShapeDtypeStruct((nb * keep, q.shape[1]), F32)],
        scratch_shapes=[pltpu.VMEM((band + seq, D_HEAD), BF16), pltpu.VMEM((band + seq, D_HEAD), BF16)],
        compiler_params=_params("arbitrary", "arbitrary"),
        name="band_attention_prompt",
    )(q, k, v, bias, g_q.reshape(1, D_HEAD), g_k.reshape(1, D_HEAD))


def _band_sample_kernel(q_ref, kn_ref, vn_ref, kc_hbm, vc_hbm, bias_ref, gq_ref, gk_ref, o_ref, ko_ref,
                        kbuf, vbuf, ksem, vsem, *, ts, w, cols, hg):
    slot = _prefetch_heads((kc_hbm, vc_hbm), (kbuf, vbuf), (ksem, vsem), hg)
    d = D_HEAD
    gq, gk = gq_ref[...], gk_ref[...]
    zeros = jnp.zeros((cols - w - ts, d), BF16)
    valid = _iota((ts, cols), 1) < w + ts
    for g in range(hg):
        hs = slice(g * d, (g + 1) * d)
        kn = _rms(kn_ref[:, hs], gk)
        ko_ref[:, hs] = kn
        k = jnp.concatenate([kbuf[slot, g].astype(BF16), kn.astype(BF16), zeros], axis=0)
        v = jnp.concatenate([vbuf[slot, g].astype(BF16), vn_ref[:, hs].astype(BF16), zeros], axis=0)
        q = (_rms(q_ref[:, hs], gq) * (SCALE * LOG2E)).astype(BF16)
        s = lax.dot_general(q, k, (((1,), (1,)), ((), ())), preferred_element_type=F32) + bias_ref[g]
        p = _softmax_rows_base2(jnp.where(valid, s, NEG))
        o_ref[:, hs] = jnp.dot(p.astype(BF16), v, preferred_element_type=F32).astype(o_ref.dtype)


def band_attention_sample(q, kn, vn, kc, vc, bias, g_q, g_k, nb, ts, w, nh, hg=SAMPLE_HEAD_GROUP_128):
    cols = -(-(w + ts) // LANES) * LANES
    assert w == BAND_CHUNKS * CHUNK and ts <= CHUNK and cols <= bias.shape[2] and ts % 16 == 0 and nh % hg == 0
    new_spec = pl.BlockSpec((ts, hg * D_HEAD), lambda b, h: (b, h))
    hbm = pl.BlockSpec(memory_space=pl.ANY)
    gspec = pl.BlockSpec((1, D_HEAD), lambda b, h: (0, 0))
    return pl.pallas_call(
        functools.partial(_band_sample_kernel, ts=ts, w=w, cols=cols, hg=hg),
        grid=(nb, nh // hg),
        in_specs=[new_spec, new_spec, new_spec, hbm, hbm,
                  pl.BlockSpec((hg, ts, cols), lambda b, h: (h, 0, 0)), gspec, gspec],
        out_specs=[new_spec, new_spec],
        out_shape=[jax.ShapeDtypeStruct(q.shape, BF16), jax.ShapeDtypeStruct(q.shape, F32)],
        scratch_shapes=[pltpu.VMEM((2, hg, w, D_HEAD), F32), pltpu.VMEM((2, hg, w, D_HEAD), F32),
                        pltpu.SemaphoreType.DMA((2, hg)), pltpu.SemaphoreType.DMA((2, hg))],
        compiler_params=_params("arbitrary", "arbitrary"),
        name="band_attention_sample",
    )(q, kn, vn, kc, vc, bias, g_q.reshape(1, D_HEAD), g_k.reshape(1, D_HEAD))


def _norm_rope(x, g, cosf, sinf, rot):
    y = _rms(x, g)
    hi = y.astype(BF16)
    lo = (y - hi.astype(F32)).astype(BF16)
    swapped = jnp.dot(jnp.concatenate([hi, lo], axis=1), rot, preferred_element_type=F32)
    return y * cosf + swapped * sinf


def rope_tables(pos):
    half = ROT_DIM // 2
    inv_freq = ROPE_THETA ** (-jnp.arange(half, dtype=F32) * (2.0 / ROT_DIM))
    ang = pos.astype(F32)[:, None] * inv_freq[None, :]
    cos, sin = jnp.cos(ang), jnp.sin(ang)
    n = pos.shape[0]
    rest = D_HEAD - ROT_DIM
    cosf = jnp.concatenate([cos, cos, jnp.ones((n, rest), F32)], axis=1)
    sinf = jnp.concatenate([sin, sin, jnp.zeros((n, rest), F32)], axis=1)
    src = jnp.arange(D_HEAD)[:, None]
    dst = jnp.arange(D_HEAD)[None, :]
    rot = (jnp.where((dst < half) & (src == dst + half), -1.0, 0.0)
           + jnp.where((dst >= half) & (dst < ROT_DIM) & (src == dst - half), 1.0, 0.0)).astype(BF16)
    return cosf, sinf, jnp.concatenate([rot, rot], axis=0)


def _online_block(q, k, v, mask, m, l, acc):
    s = lax.dot_general(q, k, (((1,), (1,)), ((), ())), preferred_element_type=F32)
    if mask is not None:
        s = jnp.where(mask, s, NEG)
    m_new = jnp.maximum(m, jnp.max(s, axis=-1, keepdims=True))
    alpha = jnp.exp2(m - m_new)
    p = jnp.exp2(s - m_new)
    l = alpha * l + jnp.sum(p, axis=-1, keepdims=True)
    acc = alpha * acc + jnp.dot(p.astype(BF16), v, preferred_element_type=F32)
    return m_new, l, acc


def _diff_lambda(lq1, lk1, lq2, lk2):
    return (jnp.exp(jnp.sum(lq1 * lk1, axis=-1, keepdims=True))
            - jnp.exp(jnp.sum(lq2 * lk2, axis=-1, keepdims=True)) + LAMBDA_INIT)


def _diff_finish(state, lam, gsub):
    (_, l1, a1), (_, l2, a2) = state
    o = a1 / l1 - lam * (a2 / l2)
    return _rms(o, gsub) * (1.0 - LAMBDA_INIT)


def _diff_init(tq):
    return (jnp.full((tq, 1), NEG, F32), jnp.zeros((tq, 1), F32), jnp.zeros((tq, 2 * D_HEAD), F32))


def _diff_prompt_kernel(q_ref, k_ref, v_ref, cos_ref, sin_ref, rot_ref, gq_ref, gk_ref, lam_ref, gsub_ref,
                        o_ref, ko_hbm, vo_hbm, kf_ref, k1_ref, k2_ref, vb_ref, osem, *, t, tk, tr):
    qi = pl.program_id(2)
    d = D_HEAD

    @pl.when(qi == 0)
    def _():
        gk = gk_ref[...]
        cosf, sinf, rot = cos_ref[...], sin_ref[...], rot_ref[...]
        k1 = _norm_rope(k_ref[:, :d], gk, cosf, sinf, rot)
        k2 = _norm_rope(k_ref[:, d:], gk, cosf, sinf, rot)
        kf_ref[:, :d] = k1
        kf_ref[:, d:] = k2
        k1_ref[...] = k1.astype(BF16)
        k2_ref[...] = k2.astype(BF16)
        vb_ref[...] = v_ref[...].astype(BF16)

    _publish_heads([(kf_ref, ko_hbm, pl.program_id(1)), (v_ref, vo_hbm, pl.program_id(1))], osem)

    r0 = pl.multiple_of(qi * t, t)
    gq = gq_ref[...]
    cosf, sinf, rot = cos_ref[pl.ds(r0, t), :], sin_ref[pl.ds(r0, t), :], rot_ref[...]
    q1 = (_norm_rope(q_ref[:, :d], gq, cosf, sinf, rot) * (SCALE * LOG2E)).astype(BF16)
    q2 = (_norm_rope(q_ref[:, d:], gq, cosf, sinf, rot) * (SCALE * LOG2E)).astype(BF16)

    subs = [slice(r * tr, (r + 1) * tr) for r in range(t // tr)]

    def step(k0, state, mask):
        v = vb_ref[pl.ds(k0, tk), :]
        k1 = k1_ref[pl.ds(k0, tk), :]
        k2 = k2_ref[pl.ds(k0, tk), :]
        out = []
        for rows, (st1, st2) in zip(subs, state):
            mr = None if mask is None else mask[rows]
            out.append((_online_block(q1[rows], k1, v, mr, *st1), _online_block(q2[rows], k2, v, mr, *st2)))
        return tuple(out)

    def body(kb, state):
        return step(pl.multiple_of(kb * tk, tk), state, None)

    state = lax.fori_loop(0, qi * (t // tk), body, tuple((_diff_init(tr), _diff_init(tr)) for _ in subs))
    for c in range(t // tk):
        mask = ((c * tk + _iota((t, tk), 1)) // CHUNK) <= (_iota((t, tk), 0) // CHUNK)
        state = step(pl.multiple_of(r0 + c * tk, tk), state, mask)
    lam = _diff_lambda(lam_ref[0:1, :], lam_ref[1:2, :], lam_ref[2:3, :], lam_ref[3:4, :])
    gsub = gsub_ref[...]
    for rows, st in zip(subs, state):
        o_ref[rows, :] = _diff_finish(st, lam, gsub).astype(o_ref.dtype)


def diff_attention_prompt(q, k, v, tabs, g_q, g_k, lam4, g_sub, nb, seq, t=DIFF_T, tk=DIFF_TK, tr=DIFF_TR):
    d2 = 2 * D_HEAD
    nh = q.shape[1] // d2
    assert t % tk == 0 and tk % CHUNK == 0 and t % tr == 0
    kv_spec = pl.BlockSpec((seq, d2), lambda b, h, qi: (b, h))
    q_spec = pl.BlockSpec((t, d2), lambda b, h, qi: (b * (seq // t) + qi, h))
    tab_spec = pl.BlockSpec((seq, D_HEAD), lambda b, h, qi: (0, 0))
    g_spec = pl.BlockSpec((1, D_HEAD), lambda b, h, qi: (0, 0))
    hbm = pl.BlockSpec(memory_space=pl.ANY)
    cache = jax.ShapeDtypeStruct((nb, seq, nh, d2), F32)
    return pl.pallas_call(
        functools.partial(_diff_prompt_kernel, t=t, tk=tk, tr=tr),
        grid=(nb, nh, seq // t),
        in_specs=[q_spec, kv_spec, kv_spec, tab_spec, tab_spec,
                  pl.BlockSpec((d2, D_HEAD), lambda b, h, qi: (0, 0)), g_spec, g_spec,
                  pl.BlockSpec((4, D_HEAD), lambda b, h, qi: (0, 0)),
                  pl.BlockSpec((1, d2), lambda b, h, qi: (0, 0))],
        out_specs=[q_spec, hbm, hbm],
        out_shape=[jax.ShapeDtypeStruct(q.shape, BF16), cache, cache],
        scratch_shapes=[pltpu.VMEM((seq, d2), F32), pltpu.VMEM((seq, D_HEAD), BF16), pltpu.VMEM((seq, D_HEAD), BF16),
                        pltpu.VMEM((seq, d2), BF16), pltpu.SemaphoreType.DMA((2,))],
        compiler_params=_params("arbitrary", "arbitrary", "arbitrary"),
        name="diff_attention_prompt",
    )(q, k, v, *tabs, g_q.reshape(1, D_HEAD), g_k.reshape(1, D_HEAD), lam4, g_sub.reshape(1, d2))


def _diff_sample_kernel(q_ref, kn_ref, vn_ref, kc_hbm, vc_hbm, cos_ref, sin_ref, rot_ref, gq_ref, gk_ref,
                        lam_ref, gsub_ref, o_ref, ko_ref, kbuf, vbuf, ksem, vsem, k1_ref, k2_ref, vb_ref,
                        *, ts, past, tn, hg):
    slot = _prefetch_heads((kc_hbm, vc_hbm), (kbuf, vbuf), (ksem, vsem), hg)
    d = D_HEAD
    cosf, sinf, rot = cos_ref[...], sin_ref[...], rot_ref[...]
    gq, gk = gq_ref[...], gk_ref[...]
    lam = _diff_lambda(lam_ref[0:1, :], lam_ref[1:2, :], lam_ref[2:3, :], lam_ref[3:4, :])
    gsub = gsub_ref[...]
    nk = past + tn
    kpos = _iota((ts, nk), 1)
    qpos = past + _iota((ts, nk), 0)
    mask = ((kpos // CHUNK) <= (qpos // CHUNK)) & (kpos < past + ts)
    for g in range(hg):
        c1, c2 = slice(2 * g * d, (2 * g + 1) * d), slice((2 * g + 1) * d, (2 * g + 2) * d)
        cv = slice(2 * g * d, (2 * g + 2) * d)
        k1 = _norm_rope(kn_ref[:, c1], gk, cosf, sinf, rot)
        k2 = _norm_rope(kn_ref[:, c2], gk, cosf, sinf, rot)
        ko_ref[:, c1] = k1
        ko_ref[:, c2] = k2
        k1_ref[g, pl.ds(0, past), :] = kbuf[slot, g, :, :d].astype(BF16)
        k2_ref[g, pl.ds(0, past), :] = kbuf[slot, g, :, d:].astype(BF16)
        vb_ref[g, pl.ds(0, past), :] = vbuf[slot, g].astype(BF16)
        k1_ref[g, pl.ds(past, tn), :] = jnp.zeros((tn, d), BF16)
        k2_ref[g, pl.ds(past, tn), :] = jnp.zeros((tn, d), BF16)
        vb_ref[g, pl.ds(past, tn), :] = jnp.zeros((tn, 2 * d), BF16)
        k1_ref[g, pl.ds(past, ts), :] = k1.astype(BF16)
        k2_ref[g, pl.ds(past, ts), :] = k2.astype(BF16)
        vb_ref[g, pl.ds(past, ts), :] = vn_ref[:, cv].astype(BF16)
        q1 = (_norm_rope(q_ref[:, c1], gq, cosf, sinf, rot) * (SCALE * LOG2E)).astype(BF16)
        q2 = (_norm_rope(q_ref[:, c2], gq, cosf, sinf, rot) * (SCALE * LOG2E)).astype(BF16)
        v = vb_ref[g]
        state = (_online_block(q1, k1_ref[g], v, mask, *_diff_init(ts)),
                 _online_block(q2, k2_ref[g], v, mask, *_diff_init(ts)))
        o_ref[:, cv] = _diff_finish(state, lam, gsub).astype(o_ref.dtype)


def diff_attention_sample(q, kn, vn, kc, vc, tabs, g_q, g_k, lam4, g_sub, nb, ts, past, nh, tn=LANES,
                          hg=SAMPLE_HEAD_GROUP):
    d2 = 2 * D_HEAD
    assert past % LANES == 0 and ts <= tn and nh % hg == 0
    new_spec = pl.BlockSpec((ts, hg * d2), lambda b, h: (b, h))
    hbm = pl.BlockSpec(memory_space=pl.ANY)
    tab_spec = pl.BlockSpec((ts, D_HEAD), lambda b, h: (0, 0))
    g_spec = pl.BlockSpec((1, D_HEAD), lambda b, h: (0, 0))
    return pl.pallas_call(
        functools.partial(_diff_sample_kernel, ts=ts, past=past, tn=tn, hg=hg),
        grid=(nb, nh // hg),
        in_specs=[new_spec, new_spec, new_spec, hbm, hbm, tab_spec, tab_spec,
                  pl.BlockSpec((d2, D_HEAD), lambda b, h: (0, 0)),
                  g_spec, g_spec, pl.BlockSpec((4, D_HEAD), lambda b, h: (0, 0)),
                  pl.BlockSpec((1, d2), lambda b, h: (0, 0))],
        out_specs=[new_spec, new_spec],
        out_shape=[jax.ShapeDtypeStruct(q.shape, BF16), jax.ShapeDtypeStruct(q.shape, F32)],
        scratch_shapes=[pltpu.VMEM((2, hg, past, d2), F32), pltpu.VMEM((2, hg, past, d2), F32),
                        pltpu.SemaphoreType.DMA((2, hg)), pltpu.SemaphoreType.DMA((2, hg)),
                        pltpu.VMEM((hg, past + tn, D_HEAD), BF16), pltpu.VMEM((hg, past + tn, D_HEAD), BF16),
                        pltpu.VMEM((hg, past + tn, d2), BF16)],
        compiler_params=_params("arbitrary", "arbitrary"),
        name="diff_attention_sample",
    )(q, kn, vn, kc, vc, *tabs, g_q.reshape(1, D_HEAD), g_k.reshape(1, D_HEAD), lam4, g_sub.reshape(1, d2))


def _to_time_major(x, nb, ts):
    return x.reshape(nb, ts, -1).transpose(1, 0, 2).reshape(nb * ts, -1)


def _to_batch_major(x, nb, ts):
    return x.reshape(ts, nb, -1).transpose(1, 0, 2).reshape(nb * ts, -1)


def _conv_glu_layer(hp, hs, g_norm, w_up, w_conv, w_down, layer, state, nb_s, ts, seq):
    mp = hp.shape[0]
    hs_t = _to_time_major(hs, nb_s, ts)
    xn = rmsnorm_rows(hp, hs_t, g_norm)
    hidden, conv_p, conv_s_t = up_convglu(xn, w_up, w_conv, layer, state.transpose(1, 0, 2), mp, seq)
    hp, hs_t = down_project(hidden, w_down, layer, hp, hs_t)
    return hp, _to_batch_major(hs_t, nb_s, ts), conv_p, conv_s_t.transpose(1, 0, 2)


def kernel(x_prompt, x_sample, cache_k_sb, cache_v_sb, cache_k_band, cache_v_band, cache_k_diff, cache_v_diff, state_conv_ffn, norm_mix, norm_ffn, w_in_ab, w_out_ab, g_q_band, g_k_band, rel_bias_band, w_in_diff, w_out_diff, g_q_diff, g_k_diff, lambda_q1, lambda_k1, lambda_q2, lambda_k2, g_sub_diff, w_up, w_conv, w_down):
    bp, tp, dm = x_prompt.shape
    bs, ts, _ = x_sample.shape
    past = cache_k_sb.shape[1]
    h_sb, h_cb, h_df = cache_k_sb.shape[2], cache_k_band.shape[2], cache_k_diff.shape[2]
    d_sb, d_cb, d_df = h_sb * D_HEAD, h_cb * D_HEAD, h_df * 2 * D_HEAD
    w_band = cache_k_band.shape[1]
    keep = min(BAND_CHUNKS * CHUNK, tp)
    mp, ms = bp * tp, bs * ts

    hp = x_prompt.reshape(mp, dm)
    hs = x_sample.reshape(ms, dm)

    xn = rmsnorm_rows(hp, hs, norm_mix[0])
    sec = []
    col = 0
    for width in (d_sb, d_sb, d_sb, d_cb, d_cb, d_cb):
        sec.append(project(xn, w_in_ab, col, width, mp))
        col += width
    (qa_p, qa_s), (ka_p, ka_s), (va_p, va_s), (qb_p, qb_s), (kb_p, kb_s), (vb_p, vb_s) = sec

    oa_p, k_sb_p, v_sb_p = sb_attention_prompt(qa_p, ka_p, va_p, bp, tp)
    oa_s = sb_attention_sample(qa_s, ka_s, va_s, cache_k_sb, cache_v_sb, bs, ts, past, h_sb)
    bias = band_bias(rel_bias_band, BAND_QCHUNKS)
    ob_p, k_band_p, v_band_p = band_attention_prompt(qb_p, kb_p, vb_p, bias, g_q_band, g_k_band, bp, tp, keep)
    ob_s, k_band_s = band_attention_sample(qb_s, kb_s, vb_s, cache_k_band, cache_v_band, bias,
                                           g_q_band, g_k_band, bs, ts, w_band, h_cb)
    hp, hs = out_project([oa_p, ob_p], [oa_s, ob_s], w_out_ab, hp, hs)
    hp, hs, conv_p0, conv_s0 = _conv_glu_layer(hp, hs, norm_ffn[0], w_up, w_conv, w_down, 0,
                                               state_conv_ffn[0], bs, ts, tp)

    xn = rmsnorm_rows(hp, hs, norm_mix[1])
    (q_p, q_s), (k_p, k_s), (v_p, v_s) = [project(xn, w_in_diff, c * d_df, d_df, mp) for c in range(3)]
    lam4 = jnp.stack([lambda_q1, lambda_k1, lambda_q2, lambda_k2]).astype(F32)
    o_p, k_diff_p, v_diff_p = diff_attention_prompt(q_p, k_p, v_p, rope_tables(jnp.arange(tp)), g_q_diff,
                                                    g_k_diff, lam4, g_sub_diff, bp, tp)
    o_s, k_diff_s = diff_attention_sample(q_s, k_s, v_s, cache_k_diff, cache_v_diff,
                                          rope_tables(past + jnp.arange(ts)), g_q_diff, g_k_diff,
                                          lam4, g_sub_diff, bs, ts, past, h_df)
    hp, hs = out_project([o_p], [o_s], w_out_diff, hp, hs)
    hp, hs, conv_p1, conv_s1 = _conv_glu_layer(hp, hs, norm_ffn[1], w_up, w_conv, w_down, 1,
                                               state_conv_ffn[1], bs, ts, tp)

    return (hp.reshape(bp, tp, dm), hs.reshape(bs, ts, dm),
            k_sb_p, v_sb_p,
            k_band_p.reshape(bp, keep, h_cb, D_HEAD), v_band_p.reshape(bp, keep, h_cb, D_HEAD),
            k_diff_p, v_diff_p,
            jnp.stack([conv_p0, conv_p1]),
            ka_s.reshape(bs, ts, h_sb, D_HEAD), va_s.reshape(bs, ts, h_sb, D_HEAD),
            k_band_s.reshape(bs, ts, h_cb, D_HEAD), vb_s.reshape(bs, ts, h_cb, D_HEAD),
            k_diff_s.reshape(bs, ts, h_df, 2 * D_HEAD), v_s.reshape(bs, ts, h_df, 2 * D_HEAD),
            jnp.stack([conv_s0, conv_s1]))
```

```python
import functools
import math

import jax
import jax.numpy as jnp
from jax import lax
from jax.experimental import pallas as pl
from jax.experimental.pallas import tpu as pltpu

F32 = jnp.float32
BF16 = jnp.bfloat16

CHUNK = 64
D_HEAD = 128
BAND_CHUNKS = 8
REL_CLIP = 128
ROT_DIM = D_HEAD // 4
ROPE_THETA = 500000.0
CONV_W = 3
EPS = 1e-6
NEG = -1e30
LAMBDA_INIT = 0.8 - 0.6 * math.exp(-0.3 * 1)
SCALE = 1.0 / math.sqrt(D_HEAD)
LOG2E = math.log2(math.e)

V7X_VMEM_LIMIT_BYTES = 56 * 1024 * 1024
LANES = 128

ROW_TILE = 1024
PROJ_ROW_TILE = 512
PROJ_COL_TILE = 1024
OUT_ROW_TILE = 1024
UP_COL_BLOCK = 256
UP_ROW_CHUNK = 512
DOWN_ROW_TILE = 256
SB_T = 512
SB_CB = 256
SB_HEAD_GROUP = 2
BAND_QCHUNKS = 4
BAND_UNROLL = 8
SAMPLE_HEAD_GROUP = 4
SAMPLE_HEAD_GROUP_128 = 8
DIFF_T = 512
DIFF_TK = 512
DIFF_TR = 128
DIFF_HEAD_GROUP = 2


def _params(*sem):
    return pltpu.CompilerParams(dimension_semantics=sem, vmem_limit_bytes=V7X_VMEM_LIMIT_BYTES)


def _iota(shape, dim):
    return lax.broadcasted_iota(jnp.int32, shape, dim)


def _rms(x, g):
    return x * lax.rsqrt(jnp.mean(x * x, axis=-1, keepdims=True) + EPS) * g


def _rmsnorm_kernel(xp_ref, xs_ref, g_ref, o_ref, *, np_tiles):
    i = pl.program_id(0)

    @pl.when(i < np_tiles)
    def _():
        o_ref[...] = _rms(xp_ref[...], g_ref[...]).astype(o_ref.dtype)

    @pl.when(i >= np_tiles)
    def _():
        o_ref[...] = _rms(xs_ref[...], g_ref[...]).astype(o_ref.dtype)


def rmsnorm_rows(xp, xs, g, tm=256):
    mp, d = xp.shape
    ms = xs.shape[0]
    np_tiles, ns_tiles = mp // tm, ms // tm
    return pl.pallas_call(
        functools.partial(_rmsnorm_kernel, np_tiles=np_tiles),
        grid=(np_tiles + ns_tiles,),
        in_specs=[
            pl.BlockSpec((tm, d), lambda i: (jnp.minimum(i, np_tiles - 1), 0)),
            pl.BlockSpec((tm, d), lambda i: (jnp.maximum(i - np_tiles, 0), 0)),
            pl.BlockSpec((1, d), lambda i: (0, 0)),
        ],
        out_specs=pl.BlockSpec((tm, d), lambda i: (i, 0)),
        out_shape=jax.ShapeDtypeStruct((mp + ms, d), BF16),
        compiler_params=_params("arbitrary"),
        name="rmsnorm_rows",
    )(xp, xs, g.reshape(1, d))


def _refresh_weights(copies, wf_ref, wb_ref):
    jj, i = pl.program_id(0), pl.program_id(1)
    tb = wf_ref.shape[2]

    @pl.when(i == 0)
    def _():
        @pl.when(jj == 0)
        def _():
            for cp in copies(jj):
                cp.start()

        for cp in copies(jj):
            cp.wait()
        for n in range(wf_ref.shape[0]):
            wb_ref[:, n * tb:(n + 1) * tb] = wf_ref[n].astype(BF16)

        @pl.when(jj + 1 < pl.num_programs(0))
        def _():
            for cp in copies(jj + 1):
                cp.start()


def _proj_kernel(x_ref, w_hbm, op_ref, os_ref, wf_ref, wsem, wb_ref, *, col0, np_tiles, ms):
    i = pl.program_id(1)
    tn = wf_ref.shape[2]

    def copies(jj):
        c0 = pl.multiple_of(col0 + jj * tn, tn)
        return [pltpu.make_async_copy(w_hbm.at[:, pl.ds(c0, tn)], wf_ref.at[0], wsem.at[0])]

    _refresh_weights(copies, wf_ref, wb_ref)

    @pl.when(i < np_tiles)
    def _():
        op_ref[...] = jnp.dot(x_ref[...], wb_ref[...], preferred_element_type=F32)

    @pl.when(i >= np_tiles)
    def _():
        os_ref[...] = jnp.dot(x_ref[0:ms, :], wb_ref[...], preferred_element_type=F32)


def project(x, w, col0, n, mp, tm=PROJ_ROW_TILE, tn=PROJ_COL_TILE):
    m, k = x.shape
    ms = m - mp
    assert mp % tm == 0 and 0 < ms <= tm and n % tn == 0 and col0 % tn == 0
    np_tiles = mp // tm
    return pl.pallas_call(
        functools.partial(_proj_kernel, col0=col0, np_tiles=np_tiles, ms=ms),
        grid=(n // tn, np_tiles + 1),
        in_specs=[
            pl.BlockSpec((tm, k), lambda j, i: (i, 0)),
            pl.BlockSpec(memory_space=pl.ANY),
        ],
        out_specs=[
            pl.BlockSpec((tm, tn), lambda j, i: (jnp.minimum(i, np_tiles - 1), j)),
            pl.BlockSpec((ms, tn), lambda j, i: (0, j)),
        ],
        out_shape=[jax.ShapeDtypeStruct((mp, n), F32), jax.ShapeDtypeStruct((ms, n), F32)],
        scratch_shapes=[pltpu.VMEM((1, k, tn), F32), pltpu.SemaphoreType.DMA((1,)), pltpu.VMEM((k, tn), BF16)],
        compiler_params=_params("arbitrary", "arbitrary"),
        name="project",
    )(x, w)


def _outproj_kernel(*refs, np_tiles, nparts):
    ap = refs[0:nparts]
    a_s = refs[nparts:2 * nparts]
    w_hbm, rp_ref, rs_ref, op_ref, os_ref, wf_ref, wsem, wb_ref = refs[2 * nparts:]
    i = pl.program_id(1)
    kp, tn = wf_ref.shape[1], wf_ref.shape[2]

    def copies(jj):
        c0 = pl.multiple_of(jj * tn, tn)
        return [pltpu.make_async_copy(w_hbm.at[pl.ds(p * kp, kp), pl.ds(c0, tn)], wf_ref.at[p], wsem.at[p])
                for p in range(nparts)]

    _refresh_weights(copies, wf_ref, wb_ref)

    def run(a_refs, r_ref, o_ref):
        acc = r_ref[...]
        for p in range(nparts):
            acc = acc + jnp.dot(a_refs[p][...], wb_ref[:, p * tn:(p + 1) * tn], preferred_element_type=F32)
        o_ref[...] = acc

    @pl.when(i < np_tiles)
    def _():
        run(ap, rp_ref, op_ref)

    @pl.when(i >= np_tiles)
    def _():
        run(a_s, rs_ref, os_ref)


def out_project(parts_p, parts_s, w, res_p, res_s, tm=OUT_ROW_TILE, tn=512):
    nparts = len(parts_p)
    mp, kp = parts_p[0].shape
    ms = parts_s[0].shape[0]
    n = w.shape[1]
    assert mp % tm == 0 and n % tn == 0 and w.shape[0] == nparts * kp
    np_tiles = mp // tm
    pmap = lambda j, i: (jnp.minimum(i, np_tiles - 1), 0)
    pmap_o = lambda j, i: (jnp.minimum(i, np_tiles - 1), j)
    in_specs = ([pl.BlockSpec((tm, kp), pmap)] * nparts + [pl.BlockSpec((ms, kp), lambda j, i: (0, 0))] * nparts
                + [pl.BlockSpec(memory_space=pl.ANY),
                   pl.BlockSpec((tm, tn), pmap_o), pl.BlockSpec((ms, tn), lambda j, i: (0, j))])
    return pl.pallas_call(
        functools.partial(_outproj_kernel, np_tiles=np_tiles, nparts=nparts),
        grid=(n // tn, np_tiles + 1),
        in_specs=in_specs,
        out_specs=[pl.BlockSpec((tm, tn), pmap_o), pl.BlockSpec((ms, tn), lambda j, i: (0, j))],
        out_shape=[jax.ShapeDtypeStruct((mp, n), F32), jax.ShapeDtypeStruct((ms, n), F32)],
        scratch_shapes=[pltpu.VMEM((nparts, kp, tn), F32), pltpu.SemaphoreType.DMA((nparts,)),
                        pltpu.VMEM((kp, nparts * tn), BF16)],
        compiler_params=_params("arbitrary", "arbitrary"),
        name="out_project",
    )(*parts_p, *parts_s, w, res_p, res_s)


def _silu_mul(gc, u):
    return gc * (1.0 / (1.0 + jnp.exp(-gc))) * u


def _up_weight_copies(w_hbm, wf_ref, wsem, jj, *, layer, nj, tb):
    j0 = 2 * jj
    j1 = jnp.minimum(j0 + 1, nj - 1)
    cols = (j0, j1, nj + j0, nj + j1)
    return [pltpu.make_async_copy(w_hbm.at[layer, :, pl.ds(pl.multiple_of(c * tb, tb), tb)], wf_ref.at[n], wsem.at[n])
            for n, c in enumerate(cols)]


def _up_kernel(x_ref, w_hbm, wc_ref, hist_ref, h_ref, cp_ref, cs_ref, wf_ref, wsem, wb_ref, carry_ref,
               *, layer, nj, tb, np_tiles, tiles_per_seq, nb_s, ms, rc):
    i = pl.program_id(1)
    tm = x_ref.shape[0]
    tn = 2 * tb
    _refresh_weights(functools.partial(_up_weight_copies, w_hbm, wf_ref, wsem, layer=layer, nj=nj, tb=tb),
                     wf_ref, wb_ref)

    w0 = wc_ref[0:1, :]
    w1 = wc_ref[1:2, :]
    w2 = wc_ref[2:3, :]

    @pl.when(lax.rem(i, tiles_per_seq) == 0)
    def _():
        carry_ref[...] = jnp.zeros_like(carry_ref)

    @pl.when(i < np_tiles)
    def _():
        c0 = carry_ref[0:1, :]
        c1 = carry_ref[1:2, :]
        row = _iota((rc, tn), 0)
        for r in range(tm // rc):
            rows = slice(r * rc, (r + 1) * rc)
            gu = jnp.dot(x_ref[rows, :], wb_ref[...], preferred_element_type=F32)
            g = gu[:, :tn]
            u = gu[:, tn:]
            gm1 = jnp.where(row == 0, c1, pltpu.roll(g, 1, 0))
            gm2 = jnp.where(row == 0, c0, jnp.where(row == 1, c1, pltpu.roll(g, 2, 0)))
            gc = gm2 * w0 + gm1 * w1 + g * w2
            h_ref[rows, :] = _silu_mul(gc, u).astype(h_ref.dtype)
            c0 = g[rc - 2:rc - 1, :]
            c1 = g[rc - 1:rc, :]
        tail = jnp.concatenate([c0, c1], axis=0)
        carry_ref[0:2, :] = tail
        cp_ref[0] = tail

    @pl.when(i >= np_tiles)
    def _():
        gu = jnp.dot(x_ref[0:ms, :], wb_ref[...], preferred_element_type=F32)
        g = gu[:, :tn]
        u = gu[:, tn:]
        h0 = hist_ref[0]
        h1 = hist_ref[1]
        gm1 = jnp.concatenate([h1, g[:ms - nb_s]], axis=0)
        gm2 = jnp.concatenate([h0, h1, g[:ms - 2 * nb_s]], axis=0)
        gc = gm2 * w0 + gm1 * w1 + g * w2
        h_ref[0:ms, :] = _silu_mul(gc, u).astype(h_ref.dtype)
        cs_ref[0] = g[ms - 2 * nb_s:ms - nb_s]
        cs_ref[1] = g[ms - nb_s:]


def up_convglu(x, w_up, w_conv, layer, hist_t, mp, seq_len, tm=ROW_TILE, tb=UP_COL_BLOCK, rc=UP_ROW_CHUNK):
    m, k = x.shape
    f = w_up.shape[2] // 2
    nb_s = hist_t.shape[1]
    nb_p = mp // seq_len
    np_tiles = mp // tm
    ms = m - mp
    assert 0 < ms <= tm and seq_len % tm == 0 and tm % rc == 0 and f % tb == 0 and CONV_W == 3
    tiles_per_seq = seq_len // tm
    nj = f // tb
    tn = 2 * tb
    njj = -(-nj // 2)
    return pl.pallas_call(
        functools.partial(_up_kernel, layer=layer, nj=nj, tb=tb, np_tiles=np_tiles, tiles_per_seq=tiles_per_seq,
                          nb_s=nb_s, ms=ms, rc=rc),
        grid=(njj, np_tiles + 1),
        in_specs=[
            pl.BlockSpec((tm, k), lambda j, i: (i, 0)),
            pl.BlockSpec(memory_space=pl.ANY),
            pl.BlockSpec((None, CONV_W, tn), lambda j, i: (layer, 0, j)),
            pl.BlockSpec((CONV_W - 1, nb_s, tn), lambda j, i: (0, 0, j)),
        ],
        out_specs=[
            pl.BlockSpec((tm, tn), lambda j, i: (i, j)),
            pl.BlockSpec((1, CONV_W - 1, tn), lambda j, i: (jnp.minimum(i // tiles_per_seq, nb_p - 1), 0, j)),
            pl.BlockSpec((CONV_W - 1, nb_s, tn), lambda j, i: (0, 0, j)),
        ],
        out_shape=[
            jax.ShapeDtypeStruct((m, f), BF16),
            jax.ShapeDtypeStruct((nb_p, CONV_W - 1, f), F32),
            jax.ShapeDtypeStruct((CONV_W - 1, nb_s, f), F32),
        ],
        scratch_shapes=[pltpu.VMEM((4, k, tb), F32), pltpu.SemaphoreType.DMA((4,)),
                        pltpu.VMEM((k, 2 * tn), BF16), pltpu.VMEM((8, tn), F32)],
        compiler_params=_params("arbitrary", "arbitrary"),
        name="up_convglu",
    )(x, w_up, w_conv, hist_t)


def _down_kernel(a_ref, w_hbm, rp_ref, rs_ref, op_ref, os_ref, wf_ref, wsem, wb_ref, *, layer, np_tiles):
    i = pl.program_id(1)
    tn = wf_ref.shape[2]

    def copies(jj):
        c0 = pl.multiple_of(jj * tn, tn)
        return [pltpu.make_async_copy(w_hbm.at[layer, :, pl.ds(c0, tn)], wf_ref.at[0], wsem.at[0])]

    _refresh_weights(copies, wf_ref, wb_ref)

    @pl.when(i < np_tiles)
    def _():
        op_ref[...] = rp_ref[...] + jnp.dot(a_ref[...], wb_ref[...], preferred_element_type=F32)

    @pl.when(i >= np_tiles)
    def _():
        os_ref[...] = rs_ref[...] + jnp.dot(a_ref[...], wb_ref[...], preferred_element_type=F32)


def down_project(a, w, layer, res_p, res_s, tm=DOWN_ROW_TILE, tn=512):
    m, f = a.shape
    n = w.shape[2]
    mp = res_p.shape[0]
    ms = m - mp
    np_tiles, ns_tiles = mp // tm, ms // tm
    pmap_o = lambda j, i: (jnp.minimum(i, np_tiles - 1), j)
    smap_o = lambda j, i: (jnp.maximum(i - np_tiles, 0), j)
    return pl.pallas_call(
        functools.partial(_down_kernel, layer=layer, np_tiles=np_tiles),
        grid=(n // tn, np_tiles + ns_tiles),
        in_specs=[
            pl.BlockSpec((tm, f), lambda j, i: (i, 0)),
            pl.BlockSpec(memory_space=pl.ANY),
            pl.BlockSpec((tm, tn), pmap_o),
            pl.BlockSpec((tm, tn), smap_o),
        ],
        out_specs=[pl.BlockSpec((tm, tn), pmap_o), pl.BlockSpec((tm, tn), smap_o)],
        out_shape=[jax.ShapeDtypeStruct((mp, n), F32), jax.ShapeDtypeStruct((ms, n), F32)],
        scratch_shapes=[pltpu.VMEM((1, f, tn), F32), pltpu.SemaphoreType.DMA((1,)), pltpu.VMEM((f, tn), BF16)],
        compiler_params=_params("arbitrary", "arbitrary"),
        name="down_project",
    )(a, w, res_p, res_s)


def _sb_group(q, k, v, uu, run, acc, vis):
    nk = k.shape[0]
    cb = uu.shape[1]
    z2 = lax.dot_general(q, k, (((1,), (1,)), ((), ())), preferred_element_type=F32)
    s = jnp.maximum(z2, 0.0) + jnp.log2(1.0 + jnp.exp2(-jnp.abs(z2)))
    if vis is not None:
        s = jnp.where(vis, s, 0.0)
    sb16 = s.astype(BF16)
    afters = []
    for blk in reversed(range(nk // cb)):
        sl = slice(blk * cb, (blk + 1) * cb)
        cs = jnp.dot(sb16[:, sl], uu, preferred_element_type=F32)
        afters.append(cs + run)
        run = run + jnp.sum(s[:, sl], axis=-1, keepdims=True)
    after = jnp.concatenate(afters[::-1], axis=1)
    w = jnp.exp2(z2 - s - after)
    if vis is not None:
        w = jnp.where(vis, w, 0.0)
    acc = acc + jnp.dot(w.astype(BF16), v, preferred_element_type=F32)
    return run, acc


def _publish_heads(items, sem_ref):
    b, qi = pl.program_id(0), pl.program_id(2)
    copies = [pltpu.make_async_copy(src, out.at[b, :, h, :], sem_ref.at[n]) for n, (src, out, h) in enumerate(items)]

    @pl.when(qi == 0)
    def _():
        for cp in copies:
            cp.start()

    @pl.when(qi == pl.num_programs(2) - 1)
    def _():
        for cp in copies:
            cp.wait()


def _sb_prompt_kernel(q_ref, k_ref, v_ref, uu_ref, o_ref, ko_hbm, vo_hbm, kb_ref, vb_ref, osem, *, t, hg):
    hh, qi = pl.program_id(1), pl.program_id(2)
    d = D_HEAD
    cols = [slice(g * d, (g + 1) * d) for g in range(hg)]

    @pl.when(qi == 0)
    def _():
        for g in range(hg):
            kb_ref[g] = k_ref[:, cols[g]].astype(BF16)
            vb_ref[g] = v_ref[:, cols[g]].astype(BF16)

    _publish_heads([(ref.at[:, pl.ds(g * d, d)], out, hh * hg + g)
                    for ref, out in ((k_ref, ko_hbm), (v_ref, vo_hbm)) for g in range(hg)], osem)

    qs = [(q_ref[:, c] * (SCALE * LOG2E)).astype(BF16) for c in cols]
    uu = uu_ref[...]
    r0 = pl.multiple_of(qi * t, t)
    vis = _iota((t, t), 1) < _iota((t, t), 0)

    def groups(k0, state, mask):
        return tuple(_sb_group(qs[g], kb_ref[g, pl.ds(k0, t), :], vb_ref[g, pl.ds(k0, t), :], uu, run, acc, mask)
                     for g, (run, acc) in enumerate(state))

    state = groups(r0, tuple((jnp.zeros((t, 1), F32), jnp.zeros((t, d), F32)) for _ in cols), vis)
    state = lax.fori_loop(0, qi, lambda i, st: groups(pl.multiple_of((qi - 1 - i) * t, t), st, None), state)
    for c, (_, acc) in zip(cols, state):
        o_ref[:, c] = acc.astype(o_ref.dtype)


def _suffix_matrix(cb):
    j = jnp.arange(cb)[:, None]
    s = jnp.arange(cb)[None, :]
    return (j > s).astype(BF16)


def sb_attention_prompt(q, k, v, nb, seq, t=SB_T, cb=SB_CB, hg=SB_HEAD_GROUP):
    nh = q.shape[1] // D_HEAD
    assert seq % t == 0 and t % cb == 0 and nh % hg == 0
    kv_spec = pl.BlockSpec((seq, hg * D_HEAD), lambda b, h, qi: (b, h))
    q_spec = pl.BlockSpec((t, hg * D_HEAD), lambda b, h, qi: (b * (seq // t) + qi, h))
    hbm = pl.BlockSpec(memory_space=pl.ANY)
    cache = jax.ShapeDtypeStruct((nb, seq, nh, D_HEAD), F32)
    return pl.pallas_call(
        functools.partial(_sb_prompt_kernel, t=t, hg=hg),
        grid=(nb, nh // hg, seq // t),
        in_specs=[q_spec, kv_spec, kv_spec, pl.BlockSpec((cb, cb), lambda b, h, qi: (0, 0))],
        out_specs=[q_spec, hbm, hbm],
        out_shape=[jax.ShapeDtypeStruct(q.shape, BF16), cache, cache],
        scratch_shapes=[pltpu.VMEM((hg, seq, D_HEAD), BF16), pltpu.VMEM((hg, seq, D_HEAD), BF16),
                        pltpu.SemaphoreType.DMA((2 * hg,))],
        compiler_params=_params("arbitrary", "arbitrary", "arbitrary"),
        name="sb_attention_prompt",
    )(q, k, v, _suffix_matrix(cb))


def _head_copy(cache_hbm, buf_ref, sem_ref, b, h, slot, g):
    return pltpu.make_async_copy(cache_hbm.at[b, :, h, :], buf_ref.at[slot, g], sem_ref.at[slot, g])


def _prefetch_heads(caches, bufs, sems, hg):
    b, hh = pl.program_id(0), pl.program_id(1)
    nb, ng = pl.num_programs(0), pl.num_programs(1)
    step = b * ng + hh
    slot = lax.rem(step, 2)
    wrap = hh + 1 == ng
    b1 = jnp.where(wrap, b + 1, b)
    hh1 = jnp.where(wrap, 0, hh + 1)

    def copies(bb, hgrp, sl):
        return [_head_copy(c, buf, sem, bb, hgrp * hg + g, sl, g)
                for c, buf, sem in zip(caches, bufs, sems) for g in range(hg)]

    @pl.when(step == 0)
    def _():
        for cp in copies(b, hh, slot):
            cp.start()

    @pl.when(step + 1 < nb * ng)
    def _():
        for cp in copies(b1, hh1, 1 - slot):
            cp.start()

    for cp in copies(b, hh, slot):
        cp.wait()
    return slot


def _sb_sample_kernel(q_ref, kn_ref, vn_ref, kc_hbm, vc_hbm, uu_ref, o_ref, kbuf, vbuf, ksem, vsem, kb_ref, vb_ref,
                      *, ts, past, pad, hg):
    slot = _prefetch_heads((kc_hbm, vc_hbm), (kbuf, vbuf), (ksem, vsem), hg)
    d = D_HEAD
    nk = past + pad
    vis = _iota((ts, nk), 1) < past + _iota((ts, nk), 0)
    uu = uu_ref[...]
    for g in range(hg):
        cols = slice(g * d, (g + 1) * d)
        kb_ref[g, pl.ds(0, past), :] = kbuf[slot, g].astype(BF16)
        vb_ref[g, pl.ds(0, past), :] = vbuf[slot, g].astype(BF16)
        kb_ref[g, pl.ds(past, pad), :] = jnp.zeros((pad, d), BF16)
        vb_ref[g, pl.ds(past, pad), :] = jnp.zeros((pad, d), BF16)
        kb_ref[g, pl.ds(past, ts), :] = kn_ref[:, cols].astype(BF16)
        vb_ref[g, pl.ds(past, ts), :] = vn_ref[:, cols].astype(BF16)
        _, acc = _sb_group((q_ref[:, cols] * (SCALE * LOG2E)).astype(BF16), kb_ref[g], vb_ref[g], uu,
                           jnp.zeros((ts, 1), F32), jnp.zeros((ts, d), F32), vis)
        o_ref[:, cols] = acc.astype(o_ref.dtype)


def sb_attention_sample(q, kn, vn, kc, vc, nb, ts, past, nh, cb=SB_CB, hg=SAMPLE_HEAD_GROUP_128):
    assert past % cb == 0 and ts <= cb and nh % hg == 0
    new_spec = pl.BlockSpec((ts, hg * D_HEAD), lambda b, h: (b, h))
    hbm = pl.BlockSpec(memory_space=pl.ANY)
    return pl.pallas_call(
        functools.partial(_sb_sample_kernel, ts=ts, past=past, pad=cb, hg=hg),
        grid=(nb, nh // hg),
        in_specs=[new_spec, new_spec, new_spec, hbm, hbm, pl.BlockSpec((cb, cb), lambda b, h: (0, 0))],
        out_specs=new_spec,
        out_shape=jax.ShapeDtypeStruct(q.shape, BF16),
        scratch_shapes=[pltpu.VMEM((2, hg, past, D_HEAD), F32), pltpu.VMEM((2, hg, past, D_HEAD), F32),
                        pltpu.SemaphoreType.DMA((2, hg)), pltpu.SemaphoreType.DMA((2, hg)),
                        pltpu.VMEM((hg, past + cb, D_HEAD), BF16), pltpu.VMEM((hg, past + cb, D_HEAD), BF16)],
        compiler_params=_params("arbitrary", "arbitrary"),
        name="sb_attention_sample",
    )(q, kn, vn, kc, vc, _suffix_matrix(cb))


def _band_bias_kernel(tab_ref, o_ref, *, rows, cols, width):
    h = pl.program_id(0)
    band = BAND_CHUNKS * CHUNK
    x = _iota((8, width), 1)
    idx = jnp.clip(x - rows - band, -REL_CLIP, REL_CLIP) + REL_CLIP

    def body(r, g):
        return jnp.where(idx == r, tab_ref[h, r] * LOG2E, g)

    g = lax.fori_loop(0, 2 * REL_CLIP + 1, body, jnp.zeros((8, width), F32))
    gt = jnp.broadcast_to(g[0:1, :], (rows, width))
    gt = pltpu.roll(gt, 0, 1, stride=1, stride_axis=0)
    bias = gt[:, rows:rows + cols]
    dc = _iota((rows, cols), 1) // CHUNK - _iota((rows, cols), 0) // CHUNK
    o_ref[0] = jnp.where((dc >= 0) & (dc <= BAND_CHUNKS), bias, NEG)


def band_bias(table, qchunks):
    nh = table.shape[0]
    rows = qchunks * CHUNK
    cols = (qchunks + BAND_CHUNKS) * CHUNK
    width = rows + cols
    assert width % LANES == 0 and rows % LANES == 0
    return pl.pallas_call(
        functools.partial(_band_bias_kernel, rows=rows, cols=cols, width=width),
        grid=(nh,),
        in_specs=[pl.BlockSpec(memory_space=pltpu.SMEM)],
        out_specs=pl.BlockSpec((1, rows, cols), lambda h: (h, 0, 0)),
        out_shape=jax.ShapeDtypeStruct((nh, rows, cols), F32),
        compiler_params=_params("arbitrary"),
        name="band_bias",
    )(table)


def _softmax_rows_base2(s2):
    m = jnp.max(s2, axis=-1, keepdims=True)
    p = jnp.exp2(s2 - m)
    return p / jnp.sum(p, axis=-1, keepdims=True)


def _band_prompt_kernel(q_ref, k_ref, v_ref, bias_ref, gq_ref, gk_ref, o_ref, kt_ref, vt_ref, kb_ref, vb_ref,
                        *, seq, rows, cols, keep):
    band = BAND_CHUNKS * CHUNK
    kn = _rms(k_ref[...], gk_ref[...])
    kt_ref[...] = kn[seq - keep:, :]
    vt_ref[...] = v_ref[pl.ds(seq - keep, keep), :]
    kb_ref[pl.ds(0, band), :] = jnp.zeros((band, D_HEAD), BF16)
    vb_ref[pl.ds(0, band), :] = jnp.zeros((band, D_HEAD), BF16)
    kb_ref[pl.ds(band, seq), :] = kn.astype(BF16)
    vb_ref[pl.ds(band, seq), :] = v_ref[...].astype(BF16)
    bias = bias_ref[0]
    gq = gq_ref[...]
    col = _iota((rows, cols), 1)

    def body(t, carry):
        r0 = pl.multiple_of(t * rows, rows)
        q = (_rms(q_ref[pl.ds(r0, rows), :], gq) * (SCALE * LOG2E)).astype(BF16)
        k = kb_ref[pl.ds(r0, cols), :]
        v = vb_ref[pl.ds(r0, cols), :]
        s = lax.dot_general(q, k, (((1,), (1,)), ((), ())), preferred_element_type=F32) + bias
        s = jnp.where(col >= band - r0, s, NEG)
        p = _softmax_rows_base2(s)
        o_ref[pl.ds(r0, rows), :] = jnp.dot(p.astype(BF16), v, preferred_element_type=F32).astype(o_ref.dtype)
        return carry

    lax.fori_loop(0, seq // rows, body, 0, unroll=BAND_UNROLL)


def band_attention_prompt(q, k, v, bias, g_q, g_k, nb, seq, keep):
    nh = q.shape[1] // D_HEAD
    rows, cols = bias.shape[1], bias.shape[2]
    band = BAND_CHUNKS * CHUNK
    blk = pl.BlockSpec((seq, D_HEAD), lambda b, h: (b, h))
    tail = pl.BlockSpec((keep, D_HEAD), lambda b, h: (b, h))
    gspec = pl.BlockSpec((1, D_HEAD), lambda b, h: (0, 0))
    return pl.pallas_call(
        functools.partial(_band_prompt_kernel, seq=seq, rows=rows, cols=cols, keep=keep),
        grid=(nb, nh),
        in_specs=[blk, blk, blk, pl.BlockSpec((1, rows, cols), lambda b, h: (h, 0, 0)), gspec, gspec],
        out_specs=[blk, tail, tail],
        out_shape=[jax.ShapeDtypeStruct(q.shape, BF16),
                   jax.ShapeDtypeStruct((nb * keep, q.shape[1]), F32),
                   jax.ShapeDtypeStruct((nb * keep, q.shape[1]), F32)],
        scratch_shapes=[pltpu.VMEM((band + seq, D_HEAD), BF16), pltpu.VMEM((band + seq, D_HEAD), BF16)],
        compiler_params=_params("arbitrary", "arbitrary"),
        name="band_attention_prompt",
    )(q, k, v, bias, g_q.reshape(1, D_HEAD), g_k.reshape(1, D_HEAD))


def _band_sample_kernel(q_ref, kn_ref, vn_ref, kc_hbm, vc_hbm, bias_ref, gq_ref, gk_ref, o_ref, ko_ref,
                        kbuf, vbuf, ksem, vsem, *, ts, w, cols, hg):
    slot = _prefetch_heads((kc_hbm, vc_hbm), (kbuf, vbuf), (ksem, vsem), hg)
    d = D_HEAD
    gq, gk = gq_ref[...], gk_ref[...]
    zeros = jnp.zeros((cols - w - ts, d), BF16)
    valid = _iota((ts, cols), 1) < w + ts
    for g in range(hg):
        hs = slice(g * d, (g + 1) * d)
        kn = _rms(kn_ref[:, hs], gk)
        ko_ref[:, hs] = kn
        k = jnp.concatenate([kbuf[slot, g].astype(BF16), kn.astype(BF16), zeros], axis=0)
        v = jnp.concatenate([vbuf[slot, g].astype(BF16), vn_ref[:, hs].astype(BF16), zeros], axis=0)
        q = (_rms(q_ref[:, hs], gq) * (SCALE * LOG2E)).astype(BF16)
        s = lax.dot_general(q, k, (((1,), (1,)), ((), ())), preferred_element_type=F32) + bias_ref[g]
        p = _softmax_rows_base2(jnp.where(valid, s, NEG))
        o_ref[:, hs] = jnp.dot(p.astype(BF16), v, preferred_element_type=F32).astype(o_ref.dtype)


def band_attention_sample(q, kn, vn, kc, vc, bias, g_q, g_k, nb, ts, w, nh, hg=SAMPLE_HEAD_GROUP_128):
    cols = -(-(w + ts) // LANES) * LANES
    assert w == BAND_CHUNKS * CHUNK and ts <= CHUNK and cols <= bias.shape[2] and ts % 16 == 0 and nh % hg == 0
    new_spec = pl.BlockSpec((ts, hg * D_HEAD), lambda b, h: (b, h))
    hbm = pl.BlockSpec(memory_space=pl.ANY)
    gspec = pl.BlockSpec((1, D_HEAD), lambda b, h: (0, 0))
    return pl.pallas_call(
        functools.partial(_band_sample_kernel, ts=ts, w=w, cols=cols, hg=hg),
        grid=(nb, nh // hg),
        in_specs=[new_spec, new_spec, new_spec, hbm, hbm,
                  pl.BlockSpec((hg, ts, cols), lambda b, h: (h, 0, 0)), gspec, gspec],
        out_specs=[new_spec, new_spec],
        out_shape=[jax.ShapeDtypeStruct(q.shape, BF16), jax.ShapeDtypeStruct(q.shape, F32)],
        scratch_shapes=[pltpu.VMEM((2, hg, w, D_HEAD), F32), pltpu.VMEM((2, hg, w, D_HEAD), F32),
                        pltpu.SemaphoreType.DMA((2, hg)), pltpu.SemaphoreType.DMA((2, hg))],
        compiler_params=_params("arbitrary", "arbitrary"),
        name="band_attention_sample",
    )(q, kn, vn, kc, vc, bias, g_q.reshape(1, D_HEAD), g_k.reshape(1, D_HEAD))


def _norm_rope(x, g, cosf, sinf, rot):
    y = _rms(x, g)
    hi = y.astype(BF16)
    lo = (y - hi.astype(F32)).astype(BF16)
    swapped = jnp.dot(jnp.concatenate([hi, lo], axis=1), rot, preferred_element_type=F32)
    return y * cosf + swapped * sinf


def rope_tables(pos):
    half = ROT_DIM // 2
    inv_freq = ROPE_THETA ** (-jnp.arange(half, dtype=F32) * (2.0 / ROT_DIM))
    ang = pos.astype(F32)[:, None] * inv_freq[None, :]
    cos, sin = jnp.cos(ang), jnp.sin(ang)
    n = pos.shape[0]
    rest = D_HEAD - ROT_DIM
    cosf = jnp.concatenate([cos, cos, jnp.ones((n, rest), F32)], axis=1)
    sinf = jnp.concatenate([sin, sin, jnp.zeros((n, rest), F32)], axis=1)
    src = jnp.arange(D_HEAD)[:, None]
    dst = jnp.arange(D_HEAD)[None, :]
    rot = (jnp.where((dst < half) & (src == dst + half), -1.0, 0.0)
           + jnp.where((dst >= half) & (dst < ROT_DIM) & (src == dst - half), 1.0, 0.0)).astype(BF16)
    return cosf, sinf, jnp.concatenate([rot, rot], axis=0)


def _online_block(q, k, v, mask, m, l, acc):
    s = lax.dot_general(q, k, (((1,), (1,)), ((), ())), preferred_element_type=F32)
    if mask is not None:
        s = jnp.where(mask, s, NEG)
    m_new = jnp.maximum(m, jnp.max(s, axis=-1, keepdims=True))
    alpha = jnp.exp2(m - m_new)
    p = jnp.exp2(s - m_new)
    l = alpha * l + jnp.sum(p, axis=-1, keepdims=True)
    acc = alpha * acc + jnp.dot(p.astype(BF16), v, preferred_element_type=F32)
    return m_new, l, acc


def _diff_lambda(lq1, lk1, lq2, lk2):
    return (jnp.exp(jnp.sum(lq1 * lk1, axis=-1, keepdims=True))
            - jnp.exp(jnp.sum(lq2 * lk2, axis=-1, keepdims=True)) + LAMBDA_INIT)


def _diff_finish(state, lam, gsub):
    (_, l1, a1), (_, l2, a2) = state
    o = a1 / l1 - lam * (a2 / l2)
    return _rms(o, gsub) * (1.0 - LAMBDA_INIT)


def _diff_init(tq):
    return (jnp.full((tq, 1), NEG, F32), jnp.zeros((tq, 1), F32), jnp.zeros((tq, 2 * D_HEAD), F32))


def _diff_prompt_kernel(q_ref, k_ref, v_ref, cos_ref, sin_ref, rot_ref, gq_ref, gk_ref, lam_ref, gsub_ref,
                        o_ref, ko_hbm, vo_hbm, kf_ref, k1_ref, k2_ref, vb_ref, osem, *, t, tk, tr, hg):
    hh, qi = pl.program_id(1), pl.program_id(2)
    d = D_HEAD

    @pl.when(qi == 0)
    def _():
        gk = gk_ref[...]
        cosf, sinf, rot = cos_ref[...], sin_ref[...], rot_ref[...]
        for g in range(hg):
            k1 = _norm_rope(k_ref[:, 2 * g * d:(2 * g + 1) * d], gk, cosf, sinf, rot)
            k2 = _norm_rope(k_ref[:, (2 * g + 1) * d:(2 * g + 2) * d], gk, cosf, sinf, rot)
            kf_ref[g, :, :d] = k1
            kf_ref[g, :, d:] = k2
            k1_ref[g] = k1.astype(BF16)
            k2_ref[g] = k2.astype(BF16)
            vb_ref[g] = v_ref[:, 2 * g * d:(2 * g + 2) * d].astype(BF16)

    _publish_heads([(kf_ref.at[g], ko_hbm, hh * hg + g) for g in range(hg)]
                   + [(v_ref.at[:, pl.ds(2 * g * d, 2 * d)], vo_hbm, hh * hg + g) for g in range(hg)], osem)

    r0 = pl.multiple_of(qi * t, t)
    gq = gq_ref[...]
    cosf, sinf, rot = cos_ref[pl.ds(r0, t), :], sin_ref[pl.ds(r0, t), :], rot_ref[...]
    qs = [tuple((_norm_rope(q_ref[:, (2 * g + c) * d:(2 * g + c + 1) * d], gq, cosf, sinf, rot)
                 * (SCALE * LOG2E)).astype(BF16) for c in range(2)) for g in range(hg)]

    subs = [slice(r * tr, (r + 1) * tr) for r in range(t // tr)]

    def step(k0, state, mask):
        out = []
        for g in range(hg):
            v = vb_ref[g, pl.ds(k0, tk), :]
            k1 = k1_ref[g, pl.ds(k0, tk), :]
            k2 = k2_ref[g, pl.ds(k0, tk), :]
            q1, q2 = qs[g]
            for rows, (st1, st2) in zip(subs, state[g * len(subs):(g + 1) * len(subs)]):
                mr = None if mask is None else mask[rows]
                out.append((_online_block(q1[rows], k1, v, mr, *st1), _online_block(q2[rows], k2, v, mr, *st2)))
        return tuple(out)

    def body(kb, state):
        return step(pl.multiple_of(kb * tk, tk), state, None)

    state = lax.fori_loop(0, qi * (t // tk), body,
                          tuple((_diff_init(tr), _diff_init(tr)) for _ in range(hg * len(subs))))
    for c in range(t // tk):
        mask = ((c * tk + _iota((t, tk), 1)) // CHUNK) <= (_iota((t, tk), 0) // CHUNK)
        state = step(pl.multiple_of(r0 + c * tk, tk), state, mask)
    lam = _diff_lambda(lam_ref[0:1, :], lam_ref[1:2, :], lam_ref[2:3, :], lam_ref[3:4, :])
    gsub = gsub_ref[...]
    for g in range(hg):
        for rows, st in zip(subs, state[g * len(subs):(g + 1) * len(subs)]):
            o_ref[rows, 2 * g * d:(2 * g + 2) * d] = _diff_finish(st, lam, gsub).astype(o_ref.dtype)


def diff_attention_prompt(q, k, v, tabs, g_q, g_k, lam4, g_sub, nb, seq, t=DIFF_T, tk=DIFF_TK, tr=DIFF_TR,
                          hg=DIFF_HEAD_GROUP):
    d2 = 2 * D_HEAD
    nh = q.shape[1] // d2
    assert t % tk == 0 and tk % CHUNK == 0 and t % tr == 0 and nh % hg == 0
    kv_spec = pl.BlockSpec((seq, hg * d2), lambda b, h, qi: (b, h))
    q_spec = pl.BlockSpec((t, hg * d2), lambda b, h, qi: (b * (seq // t) + qi, h))
    tab_spec = pl.BlockSpec((seq, D_HEAD), lambda b, h, qi: (0, 0))
    g_spec = pl.BlockSpec((1, D_HEAD), lambda b, h, qi: (0, 0))
    hbm = pl.BlockSpec(memory_space=pl.ANY)
    cache = jax.ShapeDtypeStruct((nb, seq, nh, d2), F32)
    return pl.pallas_call(
        functools.partial(_diff_prompt_kernel, t=t, tk=tk, tr=tr, hg=hg),
        grid=(nb, nh // hg, seq // t),
        in_specs=[q_spec, kv_spec, kv_spec, tab_spec, tab_spec,
                  pl.BlockSpec((d2, D_HEAD), lambda b, h, qi: (0, 0)), g_spec, g_spec,
                  pl.BlockSpec((4, D_HEAD), lambda b, h, qi: (0, 0)),
                  pl.BlockSpec((1, d2), lambda b, h, qi: (0, 0))],
        out_specs=[q_spec, hbm, hbm],
        out_shape=[jax.ShapeDtypeStruct(q.shape, BF16), cache, cache],
        scratch_shapes=[pltpu.VMEM((hg, seq, d2), F32), pltpu.VMEM((hg, seq, D_HEAD), BF16),
                        pltpu.VMEM((hg, seq, D_HEAD), BF16), pltpu.VMEM((hg, seq, d2), BF16),
                        pltpu.SemaphoreType.DMA((2 * hg,))],
        compiler_params=_params("arbitrary", "arbitrary", "arbitrary"),
        name="diff_attention_prompt",
    )(q, k, v, *tabs, g_q.reshape(1, D_HEAD), g_k.reshape(1, D_HEAD), lam4, g_sub.reshape(1, d2))


def _diff_sample_kernel(q_ref, kn_ref, vn_ref, kc_hbm, vc_hbm, cos_ref, sin_ref, rot_ref, gq_ref, gk_ref,
                        lam_ref, gsub_ref, o_ref, ko_ref, kbuf, vbuf, ksem, vsem, k1_ref, k2_ref, vb_ref,
                        *, ts, past, tn, hg):
    slot = _prefetch_heads((kc_hbm, vc_hbm), (kbuf, vbuf), (ksem, vsem), hg)
    d = D_HEAD
    cosf, sinf, rot = cos_ref[...], sin_ref[...], rot_ref[...]
    gq, gk = gq_ref[...], gk_ref[...]
    lam = _diff_lambda(lam_ref[0:1, :], lam_ref[1:2, :], lam_ref[2:3, :], lam_ref[3:4, :])
    gsub = gsub_ref[...]
    nk = past + tn
    kpos = _iota((ts, nk), 1)
    qpos = past + _iota((ts, nk), 0)
    mask = ((kpos // CHUNK) <= (qpos // CHUNK)) & (kpos < past + ts)
    for g in range(hg):
        c1, c2 = slice(2 * g * d, (2 * g + 1) * d), slice((2 * g + 1) * d, (2 * g + 2) * d)
        cv = slice(2 * g * d, (2 * g + 2) * d)
        k1 = _norm_rope(kn_ref[:, c1], gk, cosf, sinf, rot)
        k2 = _norm_rope(kn_ref[:, c2], gk, cosf, sinf, rot)
        ko_ref[:, c1] = k1
        ko_ref[:, c2] = k2
        k1_ref[g, pl.ds(0, past), :] = kbuf[slot, g, :, :d].astype(BF16)
        k2_ref[g, pl.ds(0, past), :] = kbuf[slot, g, :, d:].astype(BF16)
        vb_ref[g, pl.ds(0, past), :] = vbuf[slot, g].astype(BF16)
        k1_ref[g, pl.ds(past, tn), :] = jnp.zeros((tn, d), BF16)
        k2_ref[g, pl.ds(past, tn), :] = jnp.zeros((tn, d), BF16)
        vb_ref[g, pl.ds(past, tn), :] = jnp.zeros((tn, 2 * d), BF16)
        k1_ref[g, pl.ds(past, ts), :] = k1.astype(BF16)
        k2_ref[g, pl.ds(past, ts), :] = k2.astype(BF16)
        vb_ref[g, pl.ds(past, ts), :] = vn_ref[:, cv].astype(BF16)
        q1 = (_norm_rope(q_ref[:, c1], gq, cosf, sinf, rot) * (SCALE * LOG2E)).astype(BF16)
        q2 = (_norm_rope(q_ref[:, c2], gq, cosf, sinf, rot) * (SCALE * LOG2E)).astype(BF16)
        v = vb_ref[g]
        state = (_online_block(q1, k1_ref[g], v, mask, *_diff_init(ts)),
                 _online_block(q2, k2_ref[g], v, mask, *_diff_init(ts)))
        o_ref[:, cv] = _diff_finish(state, lam, gsub).astype(o_ref.dtype)


def diff_attention_sample(q, kn, vn, kc, vc, tabs, g_q, g_k, lam4, g_sub, nb, ts, past, nh, tn=LANES,
                          hg=SAMPLE_HEAD_GROUP):
    d2 = 2 * D_HEAD
    assert past % LANES == 0 and ts <= tn and nh % hg == 0
    new_spec = pl.BlockSpec((ts, hg * d2), lambda b, h: (b, h))
    hbm = pl.BlockSpec(memory_space=pl.ANY)
    tab_spec = pl.BlockSpec((ts, D_HEAD), lambda b, h: (0, 0))
    g_spec = pl.BlockSpec((1, D_HEAD), lambda b, h: (0, 0))
    return pl.pallas_call(
        functools.partial(_diff_sample_kernel, ts=ts, past=past, tn=tn, hg=hg),
        grid=(nb, nh // hg),
        in_specs=[new_spec, new_spec, new_spec, hbm, hbm, tab_spec, tab_spec,
                  pl.BlockSpec((d2, D_HEAD), lambda b, h: (0, 0)),
                  g_spec, g_spec, pl.BlockSpec((4, D_HEAD), lambda b, h: (0, 0)),
                  pl.BlockSpec((1, d2), lambda b, h: (0, 0))],
        out_specs=[new_spec, new_spec],
        out_shape=[jax.ShapeDtypeStruct(q.shape, BF16), jax.ShapeDtypeStruct(q.shape, F32)],
        scratch_shapes=[pltpu.VMEM((2, hg, past, d2), F32), pltpu.VMEM((2, hg, past, d2), F32),
                        pltpu.SemaphoreType.DMA((2, hg)), pltpu.SemaphoreType.DMA((2, hg)),
                        pltpu.VMEM((hg, past + tn, D_HEAD), BF16), pltpu.VMEM((hg, past + tn, D_HEAD), BF16),
                        pltpu.VMEM((hg, past + tn, d2), BF16)],
        compiler_params=_params("arbitrary", "arbitrary"),
        name="diff_attention_sample",
    )(q, kn, vn, kc, vc, *tabs, g_q.reshape(1, D_HEAD), g_k.reshape(1, D_HEAD), lam4, g_sub.reshape(1, d2))


def _to_time_major(x, nb, ts):
    return x.reshape(nb, ts, -1).transpose(1, 0, 2).reshape(nb * ts, -1)


def _to_batch_major(x, nb, ts):
    return x.reshape(ts, nb, -1).transpose(1, 0, 2).reshape(nb * ts, -1)


def _conv_glu_layer(hp, hs, g_norm, w_up, w_conv, w_down, layer, state, nb_s, ts, seq):
    mp = hp.shape[0]
    hs_t = _to_time_major(hs, nb_s, ts)
    xn = rmsnorm_rows(hp, hs_t, g_norm)
    hidden, conv_p, conv_s_t = up_convglu(xn, w_up, w_conv, layer, state.transpose(1, 0, 2), mp, seq)
    hp, hs_t = down_project(hidden, w_down, layer, hp, hs_t)
    return hp, _to_batch_major(hs_t, nb_s, ts), conv_p, conv_s_t.transpose(1, 0, 2)


def kernel(x_prompt, x_sample, cache_k_sb, cache_v_sb, cache_k_band, cache_v_band, cache_k_diff, cache_v_diff, state_conv_ffn, norm_mix, norm_ffn, w_in_ab, w_out_ab, g_q_band, g_k_band, rel_bias_band, w_in_diff, w_out_diff, g_q_diff, g_k_diff, lambda_q1, lambda_k1, lambda_q2, lambda_k2, g_sub_diff, w_up, w_conv, w_down):
    bp, tp, dm = x_prompt.shape
    bs, ts, _ = x_sample.shape
    past = cache_k_sb.shape[1]
    h_sb, h_cb, h_df = cache_k_sb.shape[2], cache_k_band.shape[2], cache_k_diff.shape[2]
    d_sb, d_cb, d_df = h_sb * D_HEAD, h_cb * D_HEAD, h_df * 2 * D_HEAD
    w_band = cache_k_band.shape[1]
    keep = min(BAND_CHUNKS * CHUNK, tp)
    mp, ms = bp * tp, bs * ts

    hp = x_prompt.reshape(mp, dm)
    hs = x_sample.reshape(ms, dm)

    xn = rmsnorm_rows(hp, hs, norm_mix[0])
    sec = []
    col = 0
    for width in (d_sb, d_sb, d_sb, d_cb, d_cb, d_cb):
        sec.append(project(xn, w_in_ab, col, width, mp))
        col += width
    (qa_p, qa_s), (ka_p, ka_s), (va_p, va_s), (qb_p, qb_s), (kb_p, kb_s), (vb_p, vb_s) = sec

    oa_p, k_sb_p, v_sb_p = sb_attention_prompt(qa_p, ka_p, va_p, bp, tp)
    oa_s = sb_attention_sample(qa_s, ka_s, va_s, cache_k_sb, cache_v_sb, bs, ts, past, h_sb)
    bias = band_bias(rel_bias_band, BAND_QCHUNKS)
    ob_p, k_band_p, v_band_p = band_attention_prompt(qb_p, kb_p, vb_p, bias, g_q_band, g_k_band, bp, tp, keep)
    ob_s, k_band_s = band_attention_sample(qb_s, kb_s, vb_s, cache_k_band, cache_v_band, bias,
                                           g_q_band, g_k_band, bs, ts, w_band, h_cb)
    hp, hs = out_project([oa_p, ob_p], [oa_s, ob_s], w_out_ab, hp, hs)
    hp, hs, conv_p0, conv_s0 = _conv_glu_layer(hp, hs, norm_ffn[0], w_up, w_conv, w_down, 0,
                                               state_conv_ffn[0], bs, ts, tp)

    xn = rmsnorm_rows(hp, hs, norm_mix[1])
    (q_p, q_s), (k_p, k_s), (v_p, v_s) = [project(xn, w_in_diff, c * d_df, d_df, mp) for c in range(3)]
    lam4 = jnp.stack([lambda_q1, lambda_k1, lambda_q2, lambda_k2]).astype(F32)
    o_p, k_diff_p, v_diff_p = diff_attention_prompt(q_p, k_p, v_p, rope_tables(jnp.arange(tp)), g_q_diff,
                                                    g_k_diff, lam4, g_sub_diff, bp, tp)
    o_s, k_diff_s = diff_attention_sample(q_s, k_s, v_s, cache_k_diff, cache_v_diff,
                                          rope_tables(past + jnp.arange(ts)), g_q_diff, g_k_diff,
                                          lam4, g_sub_diff, bs, ts, past, h_df)
    hp, hs = out_project([o_p], [o_s], w_out_diff, hp, hs)
    hp, hs, conv_p1, conv_s1 = _conv_glu_layer(hp, hs, norm_ffn[1], w_up, w_conv, w_down, 1,
                                               state_conv_ffn[1], bs, ts, tp)

    return (hp.reshape(bp, tp, dm), hs.reshape(bs, ts, dm),
            k_sb_p, v_sb_p,
            k_band_p.reshape(bp, keep, h_cb, D_HEAD), v_band_p.reshape(bp, keep, h_cb, D_HEAD),
            k_diff_p, v_diff_p,
            jnp.stack([conv_p0, conv_p1]),
            ka_s.reshape(bs, ts, h_sb, D_HEAD), va_s.reshape(bs, ts, h_sb, D_HEAD),
            k_band_s.reshape(bs, ts, h_cb, D_HEAD), vb_s.reshape(bs, ts, h_cb, D_HEAD),
            k_diff_s.reshape(bs, ts, h_df, 2 * D_HEAD), v_s.reshape(bs, ts, h_df, 2 * D_HEAD),
            jnp.stack([conv_s0, conv_s1]))
```

```python
import functools
import math

import jax
import jax.numpy as jnp
from jax import lax
from jax.experimental import pallas as pl
from jax.experimental.pallas import tpu as pltpu

F32 = jnp.float32
BF16 = jnp.bfloat16

CHUNK = 64
D_HEAD = 128
BAND_CHUNKS = 8
REL_CLIP = 128
ROT_DIM = D_HEAD // 4
ROPE_THETA = 500000.0
CONV_W = 3
EPS = 1e-6
NEG = -1e30
LAMBDA_INIT = 0.8 - 0.6 * math.exp(-0.3 * 1)
SCALE = 1.0 / math.sqrt(D_HEAD)
LOG2E = math.log2(math.e)

V7X_VMEM_LIMIT_BYTES = 56 * 1024 * 1024
LANES = 128

ROW_TILE = 1024
PROJ_ROW_TILE = 512
PROJ_COL_TILE = 1024
OUT_ROW_TILE = 1024
UP_COL_BLOCK = 256
UP_ROW_CHUNK = 512
DOWN_ROW_TILE = 256
SB_T = 512
SB_CB = 256
SB_HEAD_GROUP = 2
BAND_QCHUNKS = 4
BAND_UNROLL = 8
SAMPLE_HEAD_GROUP = 4
SAMPLE_HEAD_GROUP_128 = 8
DIFF_T = 512
DIFF_TK = 512
DIFF_TR = 128
DIFF_HEAD_GROUP = 2


def _params(*sem):
    return pltpu.CompilerParams(dimension_semantics=sem, vmem_limit_bytes=V7X_VMEM_LIMIT_BYTES)


def _iota(shape, dim):
    return lax.broadcasted_iota(jnp.int32, shape, dim)


def _rms(x, g):
    return x * lax.rsqrt(jnp.mean(x * x, axis=-1, keepdims=True) + EPS) * g


def _rmsnorm_kernel(xp_ref, xs_ref, g_ref, o_ref, *, np_tiles):
    i = pl.program_id(0)

    @pl.when(i < np_tiles)
    def _():
        o_ref[...] = _rms(xp_ref[...], g_ref[...]).astype(o_ref.dtype)

    @pl.when(i >= np_tiles)
    def _():
        o_ref[...] = _rms(xs_ref[...], g_ref[...]).astype(o_ref.dtype)


def rmsnorm_rows(xp, xs, g, tm=256):
    mp, d = xp.shape
    ms = xs.shape[0]
    np_tiles, ns_tiles = mp // tm, ms // tm
    return pl.pallas_call(
        functools.partial(_rmsnorm_kernel, np_tiles=np_tiles),
        grid=(np_tiles + ns_tiles,),
        in_specs=[
            pl.BlockSpec((tm, d), lambda i: (jnp.minimum(i, np_tiles - 1), 0)),
            pl.BlockSpec((tm, d), lambda i: (jnp.maximum(i - np_tiles, 0), 0)),
            pl.BlockSpec((1, d), lambda i: (0, 0)),
        ],
        out_specs=pl.BlockSpec((tm, d), lambda i: (i, 0)),
        out_shape=jax.ShapeDtypeStruct((mp + ms, d), BF16),
        compiler_params=_params("arbitrary"),
        name="rmsnorm_rows",
    )(xp, xs, g.reshape(1, d))


def _refresh_weights(copies, wf_ref, wb_ref):
    jj, i = pl.program_id(0), pl.program_id(1)
    tb = wf_ref.shape[2]

    @pl.when(i == 0)
    def _():
        @pl.when(jj == 0)
        def _():
            for cp in copies(jj):
                cp.start()

        for cp in copies(jj):
            cp.wait()
        for n in range(wf_ref.shape[0]):
            wb_ref[:, n * tb:(n + 1) * tb] = wf_ref[n].astype(BF16)

        @pl.when(jj + 1 < pl.num_programs(0))
        def _():
            for cp in copies(jj + 1):
                cp.start()


def _proj_kernel(x_ref, w_hbm, op_ref, os_ref, wf_ref, wsem, wb_ref, *, col0, np_tiles, ms):
    i = pl.program_id(1)
    tn = wf_ref.shape[2]

    def copies(jj):
        c0 = pl.multiple_of(col0 + jj * tn, tn)
        return [pltpu.make_async_copy(w_hbm.at[:, pl.ds(c0, tn)], wf_ref.at[0], wsem.at[0])]

    _refresh_weights(copies, wf_ref, wb_ref)

    @pl.when(i < np_tiles)
    def _():
        op_ref[...] = jnp.dot(x_ref[...], wb_ref[...], preferred_element_type=F32)

    @pl.when(i >= np_tiles)
    def _():
        os_ref[...] = jnp.dot(x_ref[0:ms, :], wb_ref[...], preferred_element_type=F32)


def project(x, w, col0, n, mp, tm=PROJ_ROW_TILE, tn=PROJ_COL_TILE):
    m, k = x.shape
    ms = m - mp
    assert mp % tm == 0 and 0 < ms <= tm and n % tn == 0 and col0 % tn == 0
    np_tiles = mp // tm
    return pl.pallas_call(
        functools.partial(_proj_kernel, col0=col0, np_tiles=np_tiles, ms=ms),
        grid=(n // tn, np_tiles + 1),
        in_specs=[
            pl.BlockSpec((tm, k), lambda j, i: (i, 0)),
            pl.BlockSpec(memory_space=pl.ANY),
        ],
        out_specs=[
            pl.BlockSpec((tm, tn), lambda j, i: (jnp.minimum(i, np_tiles - 1), j)),
            pl.BlockSpec((ms, tn), lambda j, i: (0, j)),
        ],
        out_shape=[jax.ShapeDtypeStruct((mp, n), F32), jax.ShapeDtypeStruct((ms, n), F32)],
        scratch_shapes=[pltpu.VMEM((1, k, tn), F32), pltpu.SemaphoreType.DMA((1,)), pltpu.VMEM((k, tn), BF16)],
        compiler_params=_params("arbitrary", "arbitrary"),
        name="project",
    )(x, w)


def _outproj_kernel(*refs, np_tiles, nparts):
    ap = refs[0:nparts]
    a_s = refs[nparts:2 * nparts]
    w_hbm, rp_ref, rs_ref, op_ref, os_ref, wf_ref, wsem, wb_ref = refs[2 * nparts:]
    i = pl.program_id(1)
    kp, tn = wf_ref.shape[1], wf_ref.shape[2]

    def copies(jj):
        c0 = pl.multiple_of(jj * tn, tn)
        return [pltpu.make_async_copy(w_hbm.at[pl.ds(p * kp, kp), pl.ds(c0, tn)], wf_ref.at[p], wsem.at[p])
                for p in range(nparts)]

    _refresh_weights(copies, wf_ref, wb_ref)

    def run(a_refs, r_ref, o_ref):
        acc = r_ref[...]
        for p in range(nparts):
            acc = acc + jnp.dot(a_refs[p][...], wb_ref[:, p * tn:(p + 1) * tn], preferred_element_type=F32)
        o_ref[...] = acc

    @pl.when(i < np_tiles)
    def _():
        run(ap, rp_ref, op_ref)

    @pl.when(i >= np_tiles)
    def _():
        run(a_s, rs_ref, os_ref)


def out_project(parts_p, parts_s, w, res_p, res_s, tm=OUT_ROW_TILE, tn=512):
    nparts = len(parts_p)
    mp, kp = parts_p[0].shape
    ms = parts_s[0].shape[0]
    n = w.shape[1]
    assert mp % tm == 0 and n % tn == 0 and w.shape[0] == nparts * kp
    np_tiles = mp // tm
    pmap = lambda j, i: (jnp.minimum(i, np_tiles - 1), 0)
    pmap_o = lambda j, i: (jnp.minimum(i, np_tiles - 1), j)
    in_specs = ([pl.BlockSpec((tm, kp), pmap)] * nparts + [pl.BlockSpec((ms, kp), lambda j, i: (0, 0))] * nparts
                + [pl.BlockSpec(memory_space=pl.ANY),
                   pl.BlockSpec((tm, tn), pmap_o), pl.BlockSpec((ms, tn), lambda j, i: (0, j))])
    return pl.pallas_call(
        functools.partial(_outproj_kernel, np_tiles=np_tiles, nparts=nparts),
        grid=(n // tn, np_tiles + 1),
        in_specs=in_specs,
        out_specs=[pl.BlockSpec((tm, tn), pmap_o), pl.BlockSpec((ms, tn), lambda j, i: (0, j))],
        out_shape=[jax.ShapeDtypeStruct((mp, n), F32), jax.ShapeDtypeStruct((ms, n), F32)],
        scratch_shapes=[pltpu.VMEM((nparts, kp, tn), F32), pltpu.SemaphoreType.DMA((nparts,)),
                        pltpu.VMEM((kp, nparts * tn), BF16)],
        compiler_params=_params("arbitrary", "arbitrary"),
        name="out_project",
    )(*parts_p, *parts_s, w, res_p, res_s)


def _silu_mul(gc, u):
    return gc * (1.0 / (1.0 + jnp.exp(-gc))) * u


def _up_weight_copies(w_hbm, wf_ref, wsem, jj, *, layer, nj, tb):
    j0 = 2 * jj
    j1 = jnp.minimum(j0 + 1, nj - 1)
    cols = (j0, j1, nj + j0, nj + j1)
    return [pltpu.make_async_copy(w_hbm.at[layer, :, pl.ds(pl.multiple_of(c * tb, tb), tb)], wf_ref.at[n], wsem.at[n])
            for n, c in enumerate(cols)]


def _up_kernel(x_ref, w_hbm, wc_ref, hist_ref, h_ref, cp_ref, cs_ref, wf_ref, wsem, wb_ref, carry_ref,
               *, layer, nj, tb, np_tiles, tiles_per_seq, nb_s, ms, rc):
    i = pl.program_id(1)
    tm = x_ref.shape[0]
    tn = 2 * tb
    _refresh_weights(functools.partial(_up_weight_copies, w_hbm, wf_ref, wsem, layer=layer, nj=nj, tb=tb),
                     wf_ref, wb_ref)

    w0 = wc_ref[0:1, :]
    w1 = wc_ref[1:2, :]
    w2 = wc_ref[2:3, :]

    @pl.when(lax.rem(i, tiles_per_seq) == 0)
    def _():
        carry_ref[...] = jnp.zeros_like(carry_ref)

    @pl.when(i < np_tiles)
    def _():
        c0 = carry_ref[0:1, :]
        c1 = carry_ref[1:2, :]
        row = _iota((rc, tn), 0)
        for r in range(tm // rc):
            rows = slice(r * rc, (r + 1) * rc)
            gu = jnp.dot(x_ref[rows, :], wb_ref[...], preferred_element_type=F32)
            g = gu[:, :tn]
            u = gu[:, tn:]
            gm1 = jnp.where(row == 0, c1, pltpu.roll(g, 1, 0))
            gm2 = jnp.where(row == 0, c0, jnp.where(row == 1, c1, pltpu.roll(g, 2, 0)))
            gc = gm2 * w0 + gm1 * w1 + g * w2
            h_ref[rows, :] = _silu_mul(gc, u).astype(h_ref.dtype)
            c0 = g[rc - 2:rc - 1, :]
            c1 = g[rc - 1:rc, :]
        tail = jnp.concatenate([c0, c1], axis=0)
        carry_ref[0:2, :] = tail
        cp_ref[0] = tail

    @pl.when(i >= np_tiles)
    def _():
        gu = jnp.dot(x_ref[0:ms, :], wb_ref[...], preferred_element_type=F32)
        g = gu[:, :tn]
        u = gu[:, tn:]
        h0 = hist_ref[0]
        h1 = hist_ref[1]
        gm1 = jnp.concatenate([h1, g[:ms - nb_s]], axis=0)
        gm2 = jnp.concatenate([h0, h1, g[:ms - 2 * nb_s]], axis=0)
        gc = gm2 * w0 + gm1 * w1 + g * w2
        h_ref[0:ms, :] = _silu_mul(gc, u).astype(h_ref.dtype)
        cs_ref[0] = g[ms - 2 * nb_s:ms - nb_s]
        cs_ref[1] = g[ms - nb_s:]


def up_convglu(x, w_up, w_conv, layer, hist_t, mp, seq_len, tm=ROW_TILE, tb=UP_COL_BLOCK, rc=UP_ROW_CHUNK):
    m, k = x.shape
    f = w_up.shape[2] // 2
    nb_s = hist_t.shape[1]
    nb_p = mp // seq_len
    np_tiles = mp // tm
    ms = m - mp
    assert 0 < ms <= tm and seq_len % tm == 0 and tm % rc == 0 and f % tb == 0 and CONV_W == 3
    tiles_per_seq = seq_len // tm
    nj = f // tb
    tn = 2 * tb
    njj = -(-nj // 2)
    return pl.pallas_call(
        functools.partial(_up_kernel, layer=layer, nj=nj, tb=tb, np_tiles=np_tiles, tiles_per_seq=tiles_per_seq,
                          nb_s=nb_s, ms=ms, rc=rc),
        grid=(njj, np_tiles + 1),
        in_specs=[
            pl.BlockSpec((tm, k), lambda j, i: (i, 0)),
            pl.BlockSpec(memory_space=pl.ANY),
            pl.BlockSpec((None, CONV_W, tn), lambda j, i: (layer, 0, j)),
            pl.BlockSpec((CONV_W - 1, nb_s, tn), lambda j, i: (0, 0, j)),
        ],
        out_specs=[
            pl.BlockSpec((tm, tn), lambda j, i: (i, j)),
            pl.BlockSpec((1, CONV_W - 1, tn), lambda j, i: (jnp.minimum(i // tiles_per_seq, nb_p - 1), 0, j)),
            pl.BlockSpec((CONV_W - 1, nb_s, tn), lambda j, i: (0, 0, j)),
        ],
        out_shape=[
            jax.ShapeDtypeStruct((m, f), BF16),
            jax.ShapeDtypeStruct((nb_p, CONV_W - 1, f), F32),
            jax.ShapeDtypeStruct((CONV_W - 1, nb_s, f), F32),
        ],
        scratch_shapes=[pltpu.VMEM((4, k, tb), F32), pltpu.SemaphoreType.DMA((4,)),
                        pltpu.VMEM((k, 2 * tn), BF16), pltpu.VMEM((8, tn), F32)],
        compiler_params=_params("arbitrary", "arbitrary"),
        name="up_convglu",
    )(x, w_up, w_conv, hist_t)


def _down_kernel(a_ref, w_hbm, rp_ref, rs_ref, op_ref, os_ref, wf_ref, wsem, wb_ref, *, layer, np_tiles):
    i = pl.program_id(1)
    tn = wf_ref.shape[2]

    def copies(jj):
        c0 = pl.multiple_of(jj * tn, tn)
        return [pltpu.make_async_copy(w_hbm.at[layer, :, pl.ds(c0, tn)], wf_ref.at[0], wsem.at[0])]

    _refresh_weights(copies, wf_ref, wb_ref)

    @pl.when(i < np_tiles)
    def _():
        op_ref[...] = rp_ref[...] + jnp.dot(a_ref[...], wb_ref[...], preferred_element_type=F32)

    @pl.when(i >= np_tiles)
    def _():
        os_ref[...] = rs_ref[...] + jnp.dot(a_ref[...], wb_ref[...], preferred_element_type=F32)


def down_project(a, w, layer, res_p, res_s, tm=DOWN_ROW_TILE, tn=512):
    m, f = a.shape
    n = w.shape[2]
    mp = res_p.shape[0]
    ms = m - mp
    np_tiles, ns_tiles = mp // tm, ms // tm
    pmap_o = lambda j, i: (jnp.minimum(i, np_tiles - 1), j)
    smap_o = lambda j, i: (jnp.maximum(i - np_tiles, 0), j)
    return pl.pallas_call(
        functools.partial(_down_kernel, layer=layer, np_tiles=np_tiles),
        grid=(n // tn, np_tiles + ns_tiles),
        in_specs=[
            pl.BlockSpec((tm, f), lambda j, i: (i, 0)),
            pl.BlockSpec(memory_space=pl.ANY),
            pl.BlockSpec((tm, tn), pmap_o),
            pl.BlockSpec((tm, tn), smap_o),
        ],
        out_specs=[pl.BlockSpec((tm, tn), pmap_o), pl.BlockSpec((tm, tn), smap_o)],
        out_shape=[jax.ShapeDtypeStruct((mp, n), F32), jax.ShapeDtypeStruct((ms, n), F32)],
        scratch_shapes=[pltpu.VMEM((1, f, tn), F32), pltpu.SemaphoreType.DMA((1,)), pltpu.VMEM((f, tn), BF16)],
        compiler_params=_params("arbitrary", "arbitrary"),
        name="down_project",
    )(a, w, res_p, res_s)


def _sb_group(q, k, v, uu, run, acc, vis):
    nk = k.shape[0]
    cb = uu.shape[1]
    z2 = lax.dot_general(q, k, (((1,), (1,)), ((), ())), preferred_element_type=F32)
    s = jnp.maximum(z2, 0.0) + jnp.log2(1.0 + jnp.exp2(-jnp.abs(z2)))
    if vis is not None:
        s = jnp.where(vis, s, 0.0)
    sb16 = s.astype(BF16)
    afters = []
    for blk in reversed(range(nk // cb)):
        sl = slice(blk * cb, (blk + 1) * cb)
        cs = jnp.dot(sb16[:, sl], uu, preferred_element_type=F32)
        afters.append(cs + run)
        run = run + jnp.sum(s[:, sl], axis=-1, keepdims=True)
    after = jnp.concatenate(afters[::-1], axis=1)
    w = jnp.exp2(z2 - s - after)
    if vis is not None:
        w = jnp.where(vis, w, 0.0)
    acc = acc + jnp.dot(w.astype(BF16), v, preferred_element_type=F32)
    return run, acc


def _publish_heads(items, sem_ref):
    b, qi = pl.program_id(0), pl.program_id(2)
    copies = [pltpu.make_async_copy(src, out.at[b, :, h, :], sem_ref.at[n]) for n, (src, out, h) in enumerate(items)]

    @pl.when(qi == 0)
    def _():
        for cp in copies:
            cp.start()

    @pl.when(qi == pl.num_programs(2) - 1)
    def _():
        for cp in copies:
            cp.wait()


def _sb_prompt_kernel(q_ref, k_ref, v_ref, uu_ref, o_ref, ko_hbm, vo_hbm, kb_ref, vb_ref, osem, *, t, hg):
    hh, qi = pl.program_id(1), pl.program_id(2)
    d = D_HEAD
    cols = [slice(g * d, (g + 1) * d) for g in range(hg)]

    @pl.when(qi == 0)
    def _():
        for g in range(hg):
            kb_ref[g] = k_ref[:, cols[g]].astype(BF16)
            vb_ref[g] = v_ref[:, cols[g]].astype(BF16)

    _publish_heads([(ref.at[:, pl.ds(g * d, d)], out, hh * hg + g)
                    for ref, out in ((k_ref, ko_hbm), (v_ref, vo_hbm)) for g in range(hg)], osem)

    qs = [(q_ref[:, c] * (SCALE * LOG2E)).astype(BF16) for c in cols]
    uu = uu_ref[...]
    r0 = pl.multiple_of(qi * t, t)
    vis = _iota((t, t), 1) < _iota((t, t), 0)

    def groups(k0, state, mask):
        return tuple(_sb_group(qs[g], kb_ref[g, pl.ds(k0, t), :], vb_ref[g, pl.ds(k0, t), :], uu, run, acc, mask)
                     for g, (run, acc) in enumerate(state))

    state = groups(r0, tuple((jnp.zeros((t, 1), F32), jnp.zeros((t, d), F32)) for _ in cols), vis)
    state = lax.fori_loop(0, qi, lambda i, st: groups(pl.multiple_of((qi - 1 - i) * t, t), st, None), state)
    for c, (_, acc) in zip(cols, state):
        o_ref[:, c] = acc.astype(o_ref.dtype)


def _suffix_matrix(cb):
    j = jnp.arange(cb)[:, None]
    s = jnp.arange(cb)[None, :]
    return (j > s).astype(BF16)


def sb_attention_prompt(q, k, v, nb, seq, t=SB_T, cb=SB_CB, hg=SB_HEAD_GROUP):
    nh = q.shape[1] // D_HEAD
    assert seq % t == 0 and t % cb == 0 and nh % hg == 0
    kv_spec = pl.BlockSpec((seq, hg * D_HEAD), lambda b, h, qi: (b, h))
    q_spec = pl.BlockSpec((t, hg * D_HEAD), lambda b, h, qi: (b * (seq // t) + qi, h))
    hbm = pl.BlockSpec(memory_space=pl.ANY)
    cache = jax.ShapeDtypeStruct((nb, seq, nh, D_HEAD), F32)
    return pl.pallas_call(
        functools.partial(_sb_prompt_kernel, t=t, hg=hg),
        grid=(nb, nh // hg, seq // t),
        in_specs=[q_spec, kv_spec, kv_spec, pl.BlockSpec((cb, cb), lambda b, h, qi: (0, 0))],
        out_specs=[q_spec, hbm, hbm],
        out_shape=[jax.ShapeDtypeStruct(q.shape, BF16), cache, cache],
        scratch_shapes=[pltpu.VMEM((hg, seq, D_HEAD), BF16), pltpu.VMEM((hg, seq, D_HEAD), BF16),
                        pltpu.SemaphoreType.DMA((2 * hg,))],
        compiler_params=_params("arbitrary", "arbitrary", "arbitrary"),
        name="sb_attention_prompt",
    )(q, k, v, _suffix_matrix(cb))


def _head_copy(cache_hbm, buf_ref, sem_ref, b, h, slot, g):
    return pltpu.make_async_copy(cache_hbm.at[b, :, h, :], buf_ref.at[slot, g], sem_ref.at[slot, g])


def _prefetch_heads(caches, bufs, sems, hg):
    b, hh = pl.program_id(0), pl.program_id(1)
    nb, ng = pl.num_programs(0), pl.num_programs(1)
    step = b * ng + hh
    slot = lax.rem(step, 2)
    wrap = hh + 1 == ng
    b1 = jnp.where(wrap, b + 1, b)
    hh1 = jnp.where(wrap, 0, hh + 1)

    def copies(bb, hgrp, sl):
        return [_head_copy(c, buf, sem, bb, hgrp * hg + g, sl, g)
                for c, buf, sem in zip(caches, bufs, sems) for g in range(hg)]

    @pl.when(step == 0)
    def _():
        for cp in copies(b, hh, slot):
            cp.start()

    @pl.when(step + 1 < nb * ng)
    def _():
        for cp in copies(b1, hh1, 1 - slot):
            cp.start()

    def wait_head(g):
        for c, buf, sem in zip(caches, bufs, sems):
            _head_copy(c, buf, sem, b, hh * hg + g, slot, g).wait()

    return slot, wait_head


def _sb_sample_kernel(q_ref, kn_ref, vn_ref, kc_hbm, vc_hbm, uu_ref, o_ref, kbuf, vbuf, ksem, vsem, kb_ref, vb_ref,
                      *, ts, past, pad, hg):
    slot, wait_head = _prefetch_heads((kc_hbm, vc_hbm), (kbuf, vbuf), (ksem, vsem), hg)
    d = D_HEAD
    nk = past + pad
    vis = _iota((ts, nk), 1) < past + _iota((ts, nk), 0)
    uu = uu_ref[...]
    for g in range(hg):
        cols = slice(g * d, (g + 1) * d)
        wait_head(g)
        kb_ref[g, pl.ds(0, past), :] = kbuf[slot, g].astype(BF16)
        vb_ref[g, pl.ds(0, past), :] = vbuf[slot, g].astype(BF16)
        kb_ref[g, pl.ds(past, pad), :] = jnp.zeros((pad, d), BF16)
        vb_ref[g, pl.ds(past, pad), :] = jnp.zeros((pad, d), BF16)
        kb_ref[g, pl.ds(past, ts), :] = kn_ref[:, cols].astype(BF16)
        vb_ref[g, pl.ds(past, ts), :] = vn_ref[:, cols].astype(BF16)
        _, acc = _sb_group((q_ref[:, cols] * (SCALE * LOG2E)).astype(BF16), kb_ref[g], vb_ref[g], uu,
                           jnp.zeros((ts, 1), F32), jnp.zeros((ts, d), F32), vis)
        o_ref[:, cols] = acc.astype(o_ref.dtype)


def sb_attention_sample(q, kn, vn, kc, vc, nb, ts, past, nh, cb=SB_CB, hg=SAMPLE_HEAD_GROUP_128):
    assert past % cb == 0 and ts <= cb and nh % hg == 0
    new_spec = pl.BlockSpec((ts, hg * D_HEAD), lambda b, h: (b, h))
    hbm = pl.BlockSpec(memory_space=pl.ANY)
    return pl.pallas_call(
        functools.partial(_sb_sample_kernel, ts=ts, past=past, pad=cb, hg=hg),
        grid=(nb, nh // hg),
        in_specs=[new_spec, new_spec, new_spec, hbm, hbm, pl.BlockSpec((cb, cb), lambda b, h: (0, 0))],
        out_specs=new_spec,
        out_shape=jax.ShapeDtypeStruct(q.shape, BF16),
        scratch_shapes=[pltpu.VMEM((2, hg, past, D_HEAD), F32), pltpu.VMEM((2, hg, past, D_HEAD), F32),
                        pltpu.SemaphoreType.DMA((2, hg)), pltpu.SemaphoreType.DMA((2, hg)),
                        pltpu.VMEM((hg, past + cb, D_HEAD), BF16), pltpu.VMEM((hg, past + cb, D_HEAD), BF16)],
        compiler_params=_params("arbitrary", "arbitrary"),
        name="sb_attention_sample",
    )(q, kn, vn, kc, vc, _suffix_matrix(cb))


def _band_bias_kernel(tab_ref, o_ref, *, rows, cols, width):
    h = pl.program_id(0)
    band = BAND_CHUNKS * CHUNK
    x = _iota((8, width), 1)
    idx = jnp.clip(x - rows - band, -REL_CLIP, REL_CLIP) + REL_CLIP

    def body(r, g):
        return jnp.where(idx == r, tab_ref[h, r] * LOG2E, g)

    g = lax.fori_loop(0, 2 * REL_CLIP + 1, body, jnp.zeros((8, width), F32))
    gt = jnp.broadcast_to(g[0:1, :], (rows, width))
    gt = pltpu.roll(gt, 0, 1, stride=1, stride_axis=0)
    bias = gt[:, rows:rows + cols]
    dc = _iota((rows, cols), 1) // CHUNK - _iota((rows, cols), 0) // CHUNK
    o_ref[0] = jnp.where((dc >= 0) & (dc <= BAND_CHUNKS), bias, NEG)


def band_bias(table, qchunks):
    nh = table.shape[0]
    rows = qchunks * CHUNK
    cols = (qchunks + BAND_CHUNKS) * CHUNK
    width = rows + cols
    assert width % LANES == 0 and rows % LANES == 0
    return pl.pallas_call(
        functools.partial(_band_bias_kernel, rows=rows, cols=cols, width=width),
        grid=(nh,),
        in_specs=[pl.BlockSpec(memory_space=pltpu.SMEM)],
        out_specs=pl.BlockSpec((1, rows, cols), lambda h: (h, 0, 0)),
        out_shape=jax.ShapeDtypeStruct((nh, rows, cols), F32),
        compiler_params=_params("arbitrary"),
        name="band_bias",
    )(table)


def _softmax_rows_base2(s2):
    m = jnp.max(s2, axis=-1, keepdims=True)
    p = jnp.exp2(s2 - m)
    return p / jnp.sum(p, axis=-1, keepdims=True)


def _band_prompt_kernel(q_ref, k_ref, v_ref, bias_ref, gq_ref, gk_ref, o_ref, kt_ref, vt_ref, kb_ref, vb_ref,
                        *, seq, rows, cols, keep):
    band = BAND_CHUNKS * CHUNK
    kn = _rms(k_ref[...], gk_ref[...])
    kt_ref[...] = kn[seq - keep:, :]
    vt_ref[...] = v_ref[pl.ds(seq - keep, keep), :]
    kb_ref[pl.ds(0, band), :] = jnp.zeros((band, D_HEAD), BF16)
    vb_ref[pl.ds(0, band), :] = jnp.zeros((band, D_HEAD), BF16)
    kb_ref[pl.ds(band, seq), :] = kn.astype(BF16)
    vb_ref[pl.ds(band, seq), :] = v_ref[...].astype(BF16)
    bias = bias_ref[0]
    gq = gq_ref[...]
    col = _iota((rows, cols), 1)

    def body(t, carry):
        r0 = pl.multiple_of(t * rows, rows)
        q = (_rms(q_ref[pl.ds(r0, rows), :], gq) * (SCALE * LOG2E)).astype(BF16)
        k = kb_ref[pl.ds(r0, cols), :]
        v = vb_ref[pl.ds(r0, cols), :]
        s = lax.dot_general(q, k, (((1,), (1,)), ((), ())), preferred_element_type=F32) + bias
        s = jnp.where(col >= band - r0, s, NEG)
        p = _softmax_rows_base2(s)
        o_ref[pl.ds(r0, rows), :] = jnp.dot(p.astype(BF16), v, preferred_element_type=F32).astype(o_ref.dtype)
        return carry

    lax.fori_loop(0, seq // rows, body, 0, unroll=BAND_UNROLL)


def band_attention_prompt(q, k, v, bias, g_q, g_k, nb, seq, keep):
    nh = q.shape[1] // D_HEAD
    rows, cols = bias.shape[1], bias.shape[2]
    band = BAND_CHUNKS * CHUNK
    blk = pl.BlockSpec((seq, D_HEAD), lambda b, h: (b, h))
    tail = pl.BlockSpec((keep, D_HEAD), lambda b, h: (b, h))
    gspec = pl.BlockSpec((1, D_HEAD), lambda b, h: (0, 0))
    return pl.pallas_call(
        functools.partial(_band_prompt_kernel, seq=seq, rows=rows, cols=cols, keep=keep),
        grid=(nb, nh),
        in_specs=[blk, blk, blk, pl.BlockSpec((1, rows, cols), lambda b, h: (h, 0, 0)), gspec, gspec],
        out_specs=[blk, tail, tail],
        out_shape=[jax.ShapeDtypeStruct(q.shape, BF16),
                   jax.ShapeDtypeStruct((nb * keep, q.shape[1]), F32),
                   jax.ShapeDtypeStruct((nb * keep, q.shape[1]), F32)],
        scratch_shapes=[pltpu.VMEM((band + seq, D_HEAD), BF16), pltpu.VMEM((band + seq, D_HEAD), BF16)],
        compiler_params=_params("arbitrary", "arbitrary"),
        name="band_attention_prompt",
    )(q, k, v, bias, g_q.reshape(1, D_HEAD), g_k.reshape(1, D_HEAD))


def _band_sample_kernel(q_ref, kn_ref, vn_ref, kc_hbm, vc_hbm, bias_ref, gq_ref, gk_ref, o_ref, ko_ref,
                        kbuf, vbuf, ksem, vsem, *, ts, w, cols, hg):
    slot, wait_head = _prefetch_heads((kc_hbm, vc_hbm), (kbuf, vbuf), (ksem, vsem), hg)
    d = D_HEAD
    gq, gk = gq_ref[...], gk_ref[...]
    zeros = jnp.zeros((cols - w - ts, d), BF16)
    valid = _iota((ts, cols), 1) < w + ts
    for g in range(hg):
        hs = slice(g * d, (g + 1) * d)
        wait_head(g)
        kn = _rms(kn_ref[:, hs], gk)
        ko_ref[:, hs] = kn
        k = jnp.concatenate([kbuf[slot, g].astype(BF16), kn.astype(BF16), zeros], axis=0)
        v = jnp.concatenate([vbuf[slot, g].astype(BF16), vn_ref[:, hs].astype(BF16), zeros], axis=0)
        q = (_rms(q_ref[:, hs], gq) * (SCALE * LOG2E)).astype(BF16)
        s = lax.dot_general(q, k, (((1,), (1,)), ((), ())), preferred_element_type=F32) + bias_ref[g]
        p = _softmax_rows_base2(jnp.where(valid, s, NEG))
        o_ref[:, hs] = jnp.dot(p.astype(BF16), v, preferred_element_type=F32).astype(o_ref.dtype)


def band_attention_sample(q, kn, vn, kc, vc, bias, g_q, g_k, nb, ts, w, nh, hg=SAMPLE_HEAD_GROUP_128):
    cols = -(-(w + ts) // LANES) * LANES
    assert w == BAND_CHUNKS * CHUNK and ts <= CHUNK and cols <= bias.shape[2] and ts % 16 == 0 and nh % hg == 0
    new_spec = pl.BlockSpec((ts, hg * D_HEAD), lambda b, h: (b, h))
    hbm = pl.BlockSpec(memory_space=pl.ANY)
    gspec = pl.BlockSpec((1, D_HEAD), lambda b, h: (0, 0))
    return pl.pallas_call(
        functools.partial(_band_sample_kernel, ts=ts, w=w, cols=cols, hg=hg),
        grid=(nb, nh // hg),
        in_specs=[new_spec, new_spec, new_spec, hbm, hbm,
                  pl.BlockSpec((hg, ts, cols), lambda b, h: (h, 0, 0)), gspec, gspec],
        out_specs=[new_spec, new_spec],
        out_shape=[jax.ShapeDtypeStruct(q.shape, BF16), jax.ShapeDtypeStruct(q.shape, F32)],
        scratch_shapes=[pltpu.VMEM((2, hg, w, D_HEAD), F32), pltpu.VMEM((2, hg, w, D_HEAD), F32),
                        pltpu.SemaphoreType.DMA((2, hg)), pltpu.SemaphoreType.DMA((2, hg))],
        compiler_params=_params("arbitrary", "arbitrary"),
        name="band_attention_sample",
    )(q, kn, vn, kc, vc, bias, g_q.reshape(1, D_HEAD), g_k.reshape(1, D_HEAD))


def _norm_rope(x, g, cosf, sinf, rot):
    y = _rms(x, g)
    hi = y.astype(BF16)
    lo = (y - hi.astype(F32)).astype(BF16)
    swapped = jnp.dot(jnp.concatenate([hi, lo], axis=1), rot, preferred_element_type=F32)
    return y * cosf + swapped * sinf


def rope_tables(pos):
    half = ROT_DIM // 2
    inv_freq = ROPE_THETA ** (-jnp.arange(half, dtype=F32) * (2.0 / ROT_DIM))
    ang = pos.astype(F32)[:, None] * inv_freq[None, :]
    cos, sin = jnp.cos(ang), jnp.sin(ang)
    n = pos.shape[0]
    rest = D_HEAD - ROT_DIM
    cosf = jnp.concatenate([cos, cos, jnp.ones((n, rest), F32)], axis=1)
    sinf = jnp.concatenate([sin, sin, jnp.zeros((n, rest), F32)], axis=1)
    src = jnp.arange(D_HEAD)[:, None]
    dst = jnp.arange(D_HEAD)[None, :]
    rot = (jnp.where((dst < half) & (src == dst + half), -1.0, 0.0)
           + jnp.where((dst >= half) & (dst < ROT_DIM) & (src == dst - half), 1.0, 0.0)).astype(BF16)
    return cosf, sinf, jnp.concatenate([rot, rot], axis=0)


def _online_block(q, k, v, mask, m, l, acc):
    s = lax.dot_general(q, k, (((1,), (1,)), ((), ())), preferred_element_type=F32)
    if mask is not None:
        s = jnp.where(mask, s, NEG)
    m_new = jnp.maximum(m, jnp.max(s, axis=-1, keepdims=True))
    alpha = jnp.exp2(m - m_new)
    p = jnp.exp2(s - m_new)
    l = alpha * l + jnp.sum(p, axis=-1, keepdims=True)
    acc = alpha * acc + jnp.dot(p.astype(BF16), v, preferred_element_type=F32)
    return m_new, l, acc


def _diff_lambda(lq1, lk1, lq2, lk2):
    return (jnp.exp(jnp.sum(lq1 * lk1, axis=-1, keepdims=True))
            - jnp.exp(jnp.sum(lq2 * lk2, axis=-1, keepdims=True)) + LAMBDA_INIT)


def _diff_finish(state, lam, gsub):
    (_, l1, a1), (_, l2, a2) = state
    o = a1 / l1 - lam * (a2 / l2)
    return _rms(o, gsub) * (1.0 - LAMBDA_INIT)


def _diff_init(tq):
    return (jnp.full((tq, 1), NEG, F32), jnp.zeros((tq, 1), F32), jnp.zeros((tq, 2 * D_HEAD), F32))


def _diff_prompt_kernel(q_ref, k_ref, v_ref, cos_ref, sin_ref, rot_ref, gq_ref, gk_ref, lam_ref, gsub_ref,
                        o_ref, ko_hbm, vo_hbm, kf_ref, k1_ref, k2_ref, vb_ref, osem, *, t, tk, tr, hg):
    hh, qi = pl.program_id(1), pl.program_id(2)
    d = D_HEAD

    @pl.when(qi == 0)
    def _():
        gk = gk_ref[...]
        cosf, sinf, rot = cos_ref[...], sin_ref[...], rot_ref[...]
        for g in range(hg):
            k1 = _norm_rope(k_ref[:, 2 * g * d:(2 * g + 1) * d], gk, cosf, sinf, rot)
            k2 = _norm_rope(k_ref[:, (2 * g + 1) * d:(2 * g + 2) * d], gk, cosf, sinf, rot)
            kf_ref[g, :, :d] = k1
            kf_ref[g, :, d:] = k2
            k1_ref[g] = k1.astype(BF16)
            k2_ref[g] = k2.astype(BF16)
            vb_ref[g] = v_ref[:, 2 * g * d:(2 * g + 2) * d].astype(BF16)

    _publish_heads([(kf_ref.at[g], ko_hbm, hh * hg + g) for g in range(hg)]
                   + [(v_ref.at[:, pl.ds(2 * g * d, 2 * d)], vo_hbm, hh * hg + g) for g in range(hg)], osem)

    r0 = pl.multiple_of(qi * t, t)
    gq = gq_ref[...]
    cosf, sinf, rot = cos_ref[pl.ds(r0, t), :], sin_ref[pl.ds(r0, t), :], rot_ref[...]
    qs = [tuple((_norm_rope(q_ref[:, (2 * g + c) * d:(2 * g + c + 1) * d], gq, cosf, sinf, rot)
                 * (SCALE * LOG2E)).astype(BF16) for c in range(2)) for g in range(hg)]

    subs = [slice(r * tr, (r + 1) * tr) for r in range(t // tr)]

    def step(k0, state, mask):
        out = []
        for g in range(hg):
            v = vb_ref[g, pl.ds(k0, tk), :]
            k1 = k1_ref[g, pl.ds(k0, tk), :]
            k2 = k2_ref[g, pl.ds(k0, tk), :]
            q1, q2 = qs[g]
            for rows, (st1, st2) in zip(subs, state[g * len(subs):(g + 1) * len(subs)]):
                mr = None if mask is None else mask[rows]
                out.append((_online_block(q1[rows], k1, v, mr, *st1), _online_block(q2[rows], k2, v, mr, *st2)))
        return tuple(out)

    def body(kb, state):
        return step(pl.multiple_of(kb * tk, tk), state, None)

    state = lax.fori_loop(0, qi * (t // tk), body,
                          tuple((_diff_init(tr), _diff_init(tr)) for _ in range(hg * len(subs))))
    for c in range(t // tk):
        mask = ((c * tk + _iota((t, tk), 1)) // CHUNK) <= (_iota((t, tk), 0) // CHUNK)
        state = step(pl.multiple_of(r0 + c * tk, tk), state, mask)
    lam = _diff_lambda(lam_ref[0:1, :], lam_ref[1:2, :], lam_ref[2:3, :], lam_ref[3:4, :])
    gsub = gsub_ref[...]
    for g in range(hg):
        for rows, st in zip(subs, state[g * len(subs):(g + 1) * len(subs)]):
            o_ref[rows, 2 * g * d:(2 * g + 2) * d] = _diff_finish(st, lam, gsub).astype(o_ref.dtype)


def diff_attention_prompt(q, k, v, tabs, g_q, g_k, lam4, g_sub, nb, seq, t=DIFF_T, tk=DIFF_TK, tr=DIFF_TR,
                          hg=DIFF_HEAD_GROUP):
    d2 = 2 * D_HEAD
    nh = q.shape[1] // d2
    assert t % tk == 0 and tk % CHUNK == 0 and t % tr == 0 and nh % hg == 0
    kv_spec = pl.BlockSpec((seq, hg * d2), lambda b, h, qi: (b, h))
    q_spec = pl.BlockSpec((t, hg * d2), lambda b, h, qi: (b * (seq // t) + qi, h))
    tab_spec = pl.BlockSpec((seq, D_HEAD), lambda b, h, qi: (0, 0))
    g_spec = pl.BlockSpec((1, D_HEAD), lambda b, h, qi: (0, 0))
    hbm = pl.BlockSpec(memory_space=pl.ANY)
    cache = jax.ShapeDtypeStruct((nb, seq, nh, d2), F32)
    return pl.pallas_call(
        functools.partial(_diff_prompt_kernel, t=t, tk=tk, tr=tr, hg=hg),
        grid=(nb, nh // hg, seq // t),
        in_specs=[q_spec, kv_spec, kv_spec, tab_spec, tab_spec,
                  pl.BlockSpec((d2, D_HEAD), lambda b, h, qi: (0, 0)), g_spec, g_spec,
                  pl.BlockSpec((4, D_HEAD), lambda b, h, qi: (0, 0)),
                  pl.BlockSpec((1, d2), lambda b, h, qi: (0, 0))],
        out_specs=[q_spec, hbm, hbm],
        out_shape=[jax.ShapeDtypeStruct(q.shape, BF16), cache, cache],
        scratch_shapes=[pltpu.VMEM((hg, seq, d2), F32), pltpu.VMEM((hg, seq, D_HEAD), BF16),
                        pltpu.VMEM((hg, seq, D_HEAD), BF16), pltpu.VMEM((hg, seq, d2), BF16),
                        pltpu.SemaphoreType.DMA((2 * hg,))],
        compiler_params=_params("arbitrary", "arbitrary", "arbitrary"),
        name="diff_attention_prompt",
    )(q, k, v, *tabs, g_q.reshape(1, D_HEAD), g_k.reshape(1, D_HEAD), lam4, g_sub.reshape(1, d2))


def _diff_sample_kernel(q_ref, kn_ref, vn_ref, kc_hbm, vc_hbm, cos_ref, sin_ref, rot_ref, gq_ref, gk_ref,
                        lam_ref, gsub_ref, o_ref, ko_ref, kbuf, vbuf, ksem, vsem, k1_ref, k2_ref, vb_ref,
                        *, ts, past, tn, hg):
    slot, wait_head = _prefetch_heads((kc_hbm, vc_hbm), (kbuf, vbuf), (ksem, vsem), hg)
    d = D_HEAD
    cosf, sinf, rot = cos_ref[...], sin_ref[...], rot_ref[...]
    gq, gk = gq_ref[...], gk_ref[...]
    lam = _diff_lambda(lam_ref[0:1, :], lam_ref[1:2, :], lam_ref[2:3, :], lam_ref[3:4, :])
    gsub = gsub_ref[...]
    nk = past + tn
    kpos = _iota((ts, nk), 1)
    qpos = past + _iota((ts, nk), 0)
    mask = ((kpos // CHUNK) <= (qpos // CHUNK)) & (kpos < past + ts)
    for g in range(hg):
        c1, c2 = slice(2 * g * d, (2 * g + 1) * d), slice((2 * g + 1) * d, (2 * g + 2) * d)
        cv = slice(2 * g * d, (2 * g + 2) * d)
        wait_head(g)
        k1 = _norm_rope(kn_ref[:, c1], gk, cosf, sinf, rot)
        k2 = _norm_rope(kn_ref[:, c2], gk, cosf, sinf, rot)
        ko_ref[:, c1] = k1
        ko_ref[:, c2] = k2
        k1_ref[g, pl.ds(0, past), :] = kbuf[slot, g, :, :d].astype(BF16)
        k2_ref[g, pl.ds(0, past), :] = kbuf[slot, g, :, d:].astype(BF16)
        vb_ref[g, pl.ds(0, past), :] = vbuf[slot, g].astype(BF16)
        k1_ref[g, pl.ds(past, tn), :] = jnp.zeros((tn, d), BF16)
        k2_ref[g, pl.ds(past, tn), :] = jnp.zeros((tn, d), BF16)
        vb_ref[g, pl.ds(past, tn), :] = jnp.zeros((tn, 2 * d), BF16)
        k1_ref[g, pl.ds(past, ts), :] = k1.astype(BF16)
        k2_ref[g, pl.ds(past, ts), :] = k2.astype(BF16)
        vb_ref[g, pl.ds(past, ts), :] = vn_ref[:, cv].astype(BF16)
        q1 = (_norm_rope(q_ref[:, c1], gq, cosf, sinf, rot) * (SCALE * LOG2E)).astype(BF16)
        q2 = (_norm_rope(q_ref[:, c2], gq, cosf, sinf, rot) * (SCALE * LOG2E)).astype(BF16)
        v = vb_ref[g]
        state = (_online_block(q1, k1_ref[g], v, mask, *_diff_init(ts)),
                 _online_block(q2, k2_ref[g], v, mask, *_diff_init(ts)))
        o_ref[:, cv] = _diff_finish(state, lam, gsub).astype(o_ref.dtype)


def diff_attention_sample(q, kn, vn, kc, vc, tabs, g_q, g_k, lam4, g_sub, nb, ts, past, nh, tn=LANES,
                          hg=SAMPLE_HEAD_GROUP):
    d2 = 2 * D_HEAD
    assert past % LANES == 0 and ts <= tn and nh % hg == 0
    new_spec = pl.BlockSpec((ts, hg * d2), lambda b, h: (b, h))
    hbm = pl.BlockSpec(memory_space=pl.ANY)
    tab_spec = pl.BlockSpec((ts, D_HEAD), lambda b, h: (0, 0))
    g_spec = pl.BlockSpec((1, D_HEAD), lambda b, h: (0, 0))
    return pl.pallas_call(
        functools.partial(_diff_sample_kernel, ts=ts, past=past, tn=tn, hg=hg),
        grid=(nb, nh // hg),
        in_specs=[new_spec, new_spec, new_spec, hbm, hbm, tab_spec, tab_spec,
                  pl.BlockSpec((d2, D_HEAD), lambda b, h: (0, 0)),
                  g_spec, g_spec, pl.BlockSpec((4, D_HEAD), lambda b, h: (0, 0)),
                  pl.BlockSpec((1, d2), lambda b, h: (0, 0))],
        out_specs=[new_spec, new_spec],
        out_shape=[jax.ShapeDtypeStruct(q.shape, BF16), jax.ShapeDtypeStruct(q.shape, F32)],
        scratch_shapes=[pltpu.VMEM((2, hg, past, d2), F32), pltpu.VMEM((2, hg, past, d2), F32),
                        pltpu.SemaphoreType.DMA((2, hg)), pltpu.SemaphoreType.DMA((2, hg)),
                        pltpu.VMEM((hg, past + tn, D_HEAD), BF16), pltpu.VMEM((hg, past + tn, D_HEAD), BF16),
                        pltpu.VMEM((hg, past + tn, d2), BF16)],
        compiler_params=_params("arbitrary", "arbitrary"),
        name="diff_attention_sample",
    )(q, kn, vn, kc, vc, *tabs, g_q.reshape(1, D_HEAD), g_k.reshape(1, D_HEAD), lam4, g_sub.reshape(1, d2))


def _to_time_major(x, nb, ts):
    return x.reshape(nb, ts, -1).transpose(1, 0, 2).reshape(nb * ts, -1)


def _to_batch_major(x, nb, ts):
    return x.reshape(ts, nb, -1).transpose(1, 0, 2).reshape(nb * ts, -1)


def _conv_glu_layer(hp, hs, g_norm, w_up, w_conv, w_down, layer, state, nb_s, ts, seq):
    mp = hp.shape[0]
    hs_t = _to_time_major(hs, nb_s, ts)
    xn = rmsnorm_rows(hp, hs_t, g_norm)
    hidden, conv_p, conv_s_t = up_convglu(xn, w_up, w_conv, layer, state.transpose(1, 0, 2), mp, seq)
    hp, hs_t = down_project(hidden, w_down, layer, hp, hs_t)
    return hp, _to_batch_major(hs_t, nb_s, ts), conv_p, conv_s_t.transpose(1, 0, 2)


def kernel(x_prompt, x_sample, cache_k_sb, cache_v_sb, cache_k_band, cache_v_band, cache_k_diff, cache_v_diff, state_conv_ffn, norm_mix, norm_ffn, w_in_ab, w_out_ab, g_q_band, g_k_band, rel_bias_band, w_in_diff, w_out_diff, g_q_diff, g_k_diff, lambda_q1, lambda_k1, lambda_q2, lambda_k2, g_sub_diff, w_up, w_conv, w_down):
    bp, tp, dm = x_prompt.shape
    bs, ts, _ = x_sample.shape
    past = cache_k_sb.shape[1]
    h_sb, h_cb, h_df = cache_k_sb.shape[2], cache_k_band.shape[2], cache_k_diff.shape[2]
    d_sb, d_cb, d_df = h_sb * D_HEAD, h_cb * D_HEAD, h_df * 2 * D_HEAD
    w_band = cache_k_band.shape[1]
    keep = min(BAND_CHUNKS * CHUNK, tp)
    mp, ms = bp * tp, bs * ts

    hp = x_prompt.reshape(mp, dm)
    hs = x_sample.reshape(ms, dm)

    xn = rmsnorm_rows(hp, hs, norm_mix[0])
    sec = []
    col = 0
    for width in (d_sb, d_sb, d_sb, d_cb, d_cb, d_cb):
        sec.append(project(xn, w_in_ab, col, width, mp))
        col += width
    (qa_p, qa_s), (ka_p, ka_s), (va_p, va_s), (qb_p, qb_s), (kb_p, kb_s), (vb_p, vb_s) = sec

    oa_p, k_sb_p, v_sb_p = sb_attention_prompt(qa_p, ka_p, va_p, bp, tp)
    oa_s = sb_attention_sample(qa_s, ka_s, va_s, cache_k_sb, cache_v_sb, bs, ts, past, h_sb)
    bias = band_bias(rel_bias_band, BAND_QCHUNKS)
    ob_p, k_band_p, v_band_p = band_attention_prompt(qb_p, kb_p, vb_p, bias, g_q_band, g_k_band, bp, tp, keep)
    ob_s, k_band_s = band_attention_sample(qb_s, kb_s, vb_s, cache_k_band, cache_v_band, bias,
                                           g_q_band, g_k_band, bs, ts, w_band, h_cb)
    hp, hs = out_project([oa_p, ob_p], [oa_s, ob_s], w_out_ab, hp, hs)
    hp, hs, conv_p0, conv_s0 = _conv_glu_layer(hp, hs, norm_ffn[0], w_up, w_conv, w_down, 0,
                                               state_conv_ffn[0], bs, ts, tp)

    xn = rmsnorm_rows(hp, hs, norm_mix[1])
    (q_p, q_s), (k_p, k_s), (v_p, v_s) = [project(xn, w_in_diff, c * d_df, d_df, mp) for c in range(3)]
    lam4 = jnp.stack([lambda_q1, lambda_k1, lambda_q2, lambda_k2]).astype(F32)
    o_p, k_diff_p, v_diff_p = diff_attention_prompt(q_p, k_p, v_p, rope_tables(jnp.arange(tp)), g_q_diff,
                                                    g_k_diff, lam4, g_sub_diff, bp, tp)
    o_s, k_diff_s = diff_attention_sample(q_s, k_s, v_s, cache_k_diff, cache_v_diff,
                                          rope_tables(past + jnp.arange(ts)), g_q_diff, g_k_diff,
                                          lam4, g_sub_diff, bs, ts, past, h_df)
    hp, hs = out_project([o_p], [o_s], w_out_diff, hp, hs)
    hp, hs, conv_p1, conv_s1 = _conv_glu_layer(hp, hs, norm_ffn[1], w_up, w_conv, w_down, 1,
                                               state_conv_ffn[1], bs, ts, tp)

    return (hp.reshape(bp, tp, dm), hs.reshape(bs, ts, dm),
            k_sb_p, v_sb_p,
            k_band_p.reshape(bp, keep, h_cb, D_HEAD), v_band_p.reshape(bp, keep, h_cb, D_HEAD),
            k_diff_p, v_diff_p,
            jnp.stack([conv_p0, conv_p1]),
            ka_s.reshape(bs, ts, h_sb, D_HEAD), va_s.reshape(bs, ts, h_sb, D_HEAD),
            k_band_s.reshape(bs, ts, h_cb, D_HEAD), vb_s.reshape(bs, ts, h_cb, D_HEAD),
            k_diff_s.reshape(bs, ts, h_df, 2 * D_HEAD), v_s.reshape(bs, ts, h_df, 2 * D_HEAD),
            jnp.stack([conv_s0, conv_s1]))
```
